```python
import math, functools
import jax, jax.numpy as jnp
from jax import lax
import numpy as np

D_MODEL = 1024
BATCH = 4
SEQ = 8192
DEPTH = 1
DEC_BATCH = 128
DEC_SEQ = 8
PAST_LEN = 8192
PAGE_SIZE = 128

HEAD_DIM = 64
SSD_HEADS = 8
SSD_GROUPS = 2
SSD_STATE = 128
SSD_CONV = 4
SSD_CHUNK = 128
NSA_HEADS = 8
NSA_KV_HEADS = 2
CMP_BLOCK = 32
CMP_STRIDE = 16
CMP_HIDDEN = 256
SLC_BLOCK = 64
SLC_TOPN = 16
WINDOW = 512
REL_BUCKETS = 32
REL_MAX_DIST = 128
MEM_TOKENS = 256
MEM_HEADS = 4
D_FF = 2816
Q_BLOCK = 128
EPS = 1e-6
NEG = -1e30

SSD_INNER = SSD_HEADS * HEAD_DIM
NSA_INNER = NSA_HEADS * HEAD_DIM
MIX_WIDTH = SSD_INNER + NSA_INNER
SSD_CONV_DIM = SSD_INNER + 2 * SSD_GROUPS * SSD_STATE
KV_WIDTH = NSA_KV_HEADS * HEAD_DIM
NSA_GROUP = NSA_HEADS // NSA_KV_HEADS
MEM_INNER = MEM_HEADS * HEAD_DIM
IN_SPLITS = (SSD_INNER, SSD_CONV_DIM, SSD_HEADS, NSA_INNER, KV_WIDTH, KV_WIDTH, KV_WIDTH, KV_WIDTH, KV_WIDTH, KV_WIDTH, 3 * NSA_HEADS)
IN_COLS = sum(IN_SPLITS)

kernel_name = 'hymba_ssd_nsa_macaron_decode_step'


def rmsnorm(x, g):
    xf = x.astype(jnp.float32)
    xf = xf * lax.rsqrt(jnp.mean(xf * xf, axis=-1, keepdims=True) + EPS)
    return (xf * g.astype(jnp.float32)).astype(x.dtype)


def masked_softmax(logits, mask, axis):
    lf = jnp.where(mask, logits.astype(jnp.float32), NEG)
    m = jnp.max(lf, axis=axis, keepdims=True)
    e = jnp.where(mask, jnp.exp(lf - m), 0.0)
    return e / jnp.maximum(jnp.sum(e, axis=axis, keepdims=True), 1e-30)


def swiglu_ffn(x, g, w_in, w_out):
    h = rmsnorm(x, g) @ w_in
    gate, up = jnp.split(h, 2, axis=-1)
    return (jax.nn.silu(gate) * up) @ w_out


def t5_bucket(dist):
    n = jnp.maximum(dist, 0)
    exact = REL_BUCKETS // 2
    nf = jnp.maximum(n, exact).astype(jnp.float32)
    large = exact + (jnp.log(nf / exact) / math.log(REL_MAX_DIST / exact) * (REL_BUCKETS - exact)).astype(jnp.int32)
    return jnp.where(n < exact, n, jnp.minimum(large, REL_BUCKETS - 1))


def rel_bias_2d(rel_bias, dist):
    b = rel_bias[t5_bucket(dist)]
    return jnp.moveaxis(b, -1, 0).reshape(NSA_KV_HEADS, NSA_GROUP, dist.shape[0], dist.shape[1])


def project_mix(x, g, w_in):
    h = rmsnorm(x, g) @ w_in
    parts = []
    start = 0
    for size in IN_SPLITS:
        parts.append(h[..., start:start + size])
        start += size
    return parts


def causal_dwconv(xpad, w, b):
    out = lax.conv_general_dilated(xpad, w[:, None, :], (1,), 'VALID',
                                   dimension_numbers=('NWC', 'WIO', 'NWC'),
                                   feature_group_count=xpad.shape[-1])
    return out + b


def ssd_scan(x, dt, a, bm, cm, h0):
    bsz, L, H, P = x.shape
    G, N = bm.shape[2], bm.shape[3]
    R = H // G
    Q = SSD_CHUNK if L % SSD_CHUNK == 0 else L
    nc = L // Q
    x = x.reshape(bsz, nc, Q, G, R, P)
    dt = dt.reshape(bsz, nc, Q, G, R)
    bm = bm.reshape(bsz, nc, Q, G, N)
    cm = cm.reshape(bsz, nc, Q, G, N)
    acs = jnp.cumsum(dt.astype(jnp.float32) * a.reshape(G, R).astype(jnp.float32), axis=2)
    causal = jnp.tril(jnp.ones((Q, Q), bool))[:, :, None, None]
    seg = acs[:, :, :, None] - acs[:, :, None, :]
    decay = jnp.where(causal, jnp.exp(jnp.where(causal, seg, 0.0)), 0.0).astype(x.dtype)
    cb = jnp.einsum('bctgn,bcsgn->bctsg', cm, bm)
    w = cb[..., None] * decay * dt[:, :, None]
    y_intra = jnp.einsum('bctsgr,bcsgrp->bctgrp', w, x)
    to_end = jnp.exp(acs[:, :, -1:] - acs).astype(x.dtype) * dt
    states = jnp.einsum('bcsgn,bcsgr,bcsgrp->bcgrpn', bm, to_end, x)
    chunk_decay = jnp.exp(acs[:, :, -1]).astype(x.dtype)

    def step(h, inp):
        st, cd = inp
        return h * cd[..., None, None] + st, h

    h_final, h_prev = lax.scan(step, h0.reshape(bsz, G, R, P, N),
                               (jnp.moveaxis(states, 1, 0), jnp.moveaxis(chunk_decay, 1, 0)))
    h_prev = jnp.moveaxis(h_prev, 0, 1)
    y_inter = jnp.einsum('bctgn,bctgr,bcgrpn->bctgrp', cm, jnp.exp(acs).astype(x.dtype), h_prev)
    return (y_intra + y_inter).reshape(bsz, L, H, P), h_final.reshape(bsz, H, P, N)


def ssd_branch(z, xbc, dt_raw, conv_buf, h0, conv_w, conv_b, dt_bias, a_log, d_skip, norm_g):
    bsz, L, _ = xbc.shape
    xpad = jnp.concatenate([conv_buf, xbc], axis=1)
    new_conv = xpad[:, L:]
    xc = jax.nn.silu(causal_dwconv(xpad, conv_w, conv_b))
    xs = xc[..., :SSD_INNER].reshape(bsz, L, SSD_HEADS, HEAD_DIM)
    bm = xc[..., SSD_INNER:SSD_INNER + SSD_GROUPS * SSD_STATE].reshape(bsz, L, SSD_GROUPS, SSD_STATE)
    cm = xc[..., SSD_INNER + SSD_GROUPS * SSD_STATE:].reshape(bsz, L, SSD_GROUPS, SSD_STATE)
    dt = jax.nn.softplus(dt_raw + dt_bias)
    a = -jnp.exp(a_log)
    y, h = ssd_scan(xs, dt, a, bm, cm, h0)
    y = y + d_skip[:, None] * xs
    y = rmsnorm(y.reshape(bsz, L, SSD_INNER) * jax.nn.silu(z), norm_g)
    return y, h, new_conv


def compress(rows, pe, w1, w2):
    bsz, L = rows.shape[:2]
    r = CMP_BLOCK // CMP_STRIDE
    nch = L // CMP_STRIDE
    n_cmp = nch - r + 1
    chunks = rows[:, :nch * CMP_STRIDE].reshape(bsz, nch, CMP_STRIDE, NSA_KV_HEADS, HEAD_DIM)
    pe_r = pe.reshape(r, CMP_STRIDE, HEAD_DIM)
    w1_r = w1.reshape(r, CMP_STRIDE, HEAD_DIM, CMP_HIDDEN)
    hid = jnp.einsum('bnsgd,sdh->bngh', chunks[:, :n_cmp] + pe_r[0][:, None, :], w1_r[0])
    for j in range(1, r):
        hid = hid + jnp.einsum('bnsgd,sdh->bngh', chunks[:, j:j + n_cmp] + pe_r[j][:, None, :], w1_r[j])
    return jax.nn.silu(hid) @ w2


def nsa_core(q, q_pos, kc, vc, ks_blk, vs_blk, kw, vw, kw_pos, gates, rel_bias):
    bsz, tq = q.shape[:2]
    G = NSA_KV_HEADS
    scale = HEAD_DIM ** -0.5
    n_cmp = kc.shape[1]
    c_start = jnp.arange(n_cmp) * CMP_STRIDE
    c_end = c_start + CMP_BLOCK - 1
    dist_c = q_pos[:, None] - c_end[None, :]
    logit_c = jnp.einsum('bqgrd,bkgd->bgrqk', q, kc) * scale + rel_bias_2d(rel_bias, dist_c)
    p_c = masked_softmax(logit_c, dist_c >= 0, axis=-1)
    o_cmp = jnp.einsum('bgrqk,bkgd->bqgrd', p_c.astype(vc.dtype), vc)
    n_slc = ks_blk.shape[1]
    s_start = jnp.arange(n_slc) * SLC_BLOCK
    overlap = ((c_start[:, None] < s_start[None, :] + SLC_BLOCK) & (c_end[:, None] >= s_start[None, :])).astype(jnp.float32)
    imp = jnp.einsum('bgrqk,ks->bgqs', p_c, overlap)
    q_blk = q_pos // SLC_BLOCK
    blk = jnp.arange(n_slc)[None, :]
    forced = (blk == 0) | (blk == q_blk[:, None]) | (blk == q_blk[:, None] - 1)
    valid = s_start[None, :] <= q_pos[:, None]
    imp = jnp.where(forced, jnp.inf, imp)
    imp = jnp.where(valid, imp, -jnp.inf)
    _, idx = lax.top_k(imp, min(SLC_TOPN, n_slc))
    bi = jnp.arange(bsz)[:, None, None, None]
    gi = jnp.arange(G)[None, :, None, None]
    k_sel = jnp.moveaxis(ks_blk, 3, 1)[bi, gi, idx]
    v_sel = jnp.moveaxis(vs_blk, 3, 1)[bi, gi, idx]
    pos_sel = idx[..., None] * SLC_BLOCK + jnp.arange(SLC_BLOCK)
    dist_s = q_pos[:, None, None] - pos_sel
    tbl = rel_bias.reshape(REL_BUCKETS, G, NSA_GROUP)
    bias_s = tbl[t5_bucket(dist_s), gi[..., None]]
    logit_s = jnp.einsum('bqgrd,bgqkpd->bgqkpr', q, k_sel) * scale + bias_s
    kk = idx.shape[-1] * SLC_BLOCK
    p_s = masked_softmax(logit_s.reshape(bsz, G, tq, kk, NSA_GROUP), (dist_s >= 0).reshape(bsz, G, tq, kk, 1), axis=3)
    o_slc = jnp.einsum('bgqnr,bgqnd->bqgrd', p_s.astype(vs_blk.dtype), v_sel.reshape(bsz, G, tq, kk, HEAD_DIM))
    dist_w = q_pos[:, None] - kw_pos[None, :]
    mask_w = (dist_w >= 0) & (dist_w < WINDOW) & (kw_pos[None, :] >= 0)
    logit_w = jnp.einsum('bqgrd,bkgd->bgrqk', q, kw) * scale + rel_bias_2d(rel_bias, dist_w)
    p_w = masked_softmax(logit_w, mask_w, axis=-1)
    o_win = jnp.einsum('bgrqk,bkgd->bqgrd', p_w.astype(vw.dtype), vw)
    return gates[..., 0:1] * o_cmp + gates[..., 1:2] * o_slc + gates[..., 2:3] * o_win


def nsa_prompt(qh, kv_c, kv_s, kv_w, gates, cmp_params, rel_bias):
    bsz, L = qh.shape[:2]
    pe_k, w1_k, w2_k, pe_v, w1_v, w2_v = cmp_params
    kc = compress(kv_c[:, :, 0], pe_k, w1_k, w2_k)
    vc = compress(kv_c[:, :, 1], pe_v, w1_v, w2_v)
    n_slc = L // SLC_BLOCK
    ks_blk = kv_s[:, :n_slc * SLC_BLOCK, 0].reshape(bsz, n_slc, SLC_BLOCK, NSA_KV_HEADS, HEAD_DIM)
    vs_blk = kv_s[:, :n_slc * SLC_BLOCK, 1].reshape(bsz, n_slc, SLC_BLOCK, NSA_KV_HEADS, HEAD_DIM)
    kv_wp = jnp.pad(kv_w, ((0, 0), (WINDOW, 0), (0, 0), (0, 0), (0, 0)))

    def block(i):
        s = i * Q_BLOCK
        q_b = lax.dynamic_slice_in_dim(qh, s, Q_BLOCK, axis=1)
        g_b = lax.dynamic_slice_in_dim(gates, s, Q_BLOCK, axis=1)
        w_b = lax.dynamic_slice_in_dim(kv_wp, s, Q_BLOCK + WINDOW, axis=1)
        q_pos = s + jnp.arange(Q_BLOCK)
        kw_pos = s - WINDOW + jnp.arange(Q_BLOCK + WINDOW)
        return nsa_core(q_b, q_pos, kc, vc, ks_blk, vs_blk, w_b[:, :, 0], w_b[:, :, 1], kw_pos, g_b, rel_bias)

    o = lax.map(block, jnp.arange(L // Q_BLOCK))
    o = jnp.moveaxis(o, 0, 1).reshape(bsz, L, NSA_INNER)
    return o, (kv_c, kv_s, kv_w[:, L - min(WINDOW, L):])


def nsa_sample(qh, kv_c, kv_s, kv_w, gates, cmp_params, rel_bias, pool_c, pool_s, win_buf, page_table):
    bsz, L = qh.shape[:2]
    pe_k, w1_k, w2_k, pe_v, w1_v, w2_v = cmp_params
    total = PAST_LEN + L
    past_c = pool_c[page_table].reshape(bsz, PAST_LEN, 2, NSA_KV_HEADS, HEAD_DIM)
    full_c = jnp.concatenate([past_c, kv_c], axis=1)
    kc = compress(full_c[:, :, 0], pe_k, w1_k, w2_k)
    vc = compress(full_c[:, :, 1], pe_v, w1_v, w2_v)
    n_slc = -(-total // SLC_BLOCK)
    past_s = pool_s[page_table].reshape(bsz, PAST_LEN, 2, NSA_KV_HEADS, HEAD_DIM)
    full_s = jnp.pad(jnp.concatenate([past_s, kv_s], axis=1),
                     ((0, 0), (0, n_slc * SLC_BLOCK - total), (0, 0), (0, 0), (0, 0)))
    ks_blk = full_s[:, :, 0].reshape(bsz, n_slc, SLC_BLOCK, NSA_KV_HEADS, HEAD_DIM)
    vs_blk = full_s[:, :, 1].reshape(bsz, n_slc, SLC_BLOCK, NSA_KV_HEADS, HEAD_DIM)
    w_kept = win_buf.shape[1]
    win = jnp.concatenate([win_buf, kv_w], axis=1)
    q_pos = PAST_LEN + jnp.arange(L)
    kw_pos = PAST_LEN - w_kept + jnp.arange(w_kept + L)
    o = nsa_core(qh, q_pos, kc, vc, ks_blk, vs_blk, win[:, :, 0], win[:, :, 1], kw_pos, gates, rel_bias)
    return o.reshape(bsz, L, NSA_INNER), (kv_c, kv_s, win[:, L:])


def mem_kv_proj(mem, g, w_kv):
    bsz, m, _ = mem.shape
    return (rmsnorm(mem, g) @ w_kv).reshape(bsz, m, 2, MEM_HEADS, HEAD_DIM)


def mem_attend(x, kv, g, w_q, w_o):
    bsz, L, _ = x.shape
    q = (rmsnorm(x, g) @ w_q).reshape(bsz, L, MEM_HEADS, HEAD_DIM)
    logits = jnp.einsum('blhd,bmhd->bhlm', q, kv[:, :, 0]) * (HEAD_DIM ** -0.5)
    p = jax.nn.softmax(logits.astype(jnp.float32), axis=-1).astype(x.dtype)
    o = jnp.einsum('bhlm,bmhd->blhd', p, kv[:, :, 1]).reshape(bsz, L, MEM_INNER)
    return o @ w_o


def trunk_layer(x, lw, conv_buf, h0, nsa_fn, mem_kv):
    bsz, L, _ = x.shape
    x = x + 0.5 * swiglu_ffn(x, lw['norm_ffn1'], lw['w_ffn1_in'], lw['w_ffn1_out'])
    z, xbc, dt_raw, q, kc, vc, ks, vs, kw, vw, g_logit = project_mix(x, lw['norm_mix'], lw['w_in'])
    y_ssd, h_new, conv_new = ssd_branch(z, xbc, dt_raw, conv_buf, h0, lw['conv_w'], lw['conv_b'],
                                        lw['dt_bias'], lw['a_log'], lw['d_skip'], lw['norm_ssd_out'])

    def rows(kt, vt):
        return jnp.stack([kt.reshape(bsz, L, NSA_KV_HEADS, HEAD_DIM), vt.reshape(bsz, L, NSA_KV_HEADS, HEAD_DIM)], axis=2)

    qh = q.reshape(bsz, L, NSA_KV_HEADS, NSA_GROUP, HEAD_DIM)
    gates = jax.nn.sigmoid(g_logit).reshape(bsz, L, NSA_KV_HEADS, NSA_GROUP, 3)
    o_nsa, nsa_state = nsa_fn(qh, rows(kc, vc), rows(ks, vs), rows(kw, vw), gates)
    y_nsa = rmsnorm(o_nsa, lw['norm_nsa_out'])
    x = x + jnp.concatenate([y_ssd, y_nsa], axis=-1) @ lw['w_out']
    x = x + mem_attend(x, mem_kv, lw['norm_mem_q'], lw['w_mem_q'], lw['w_mem_out'])
    x = x + 0.5 * swiglu_ffn(x, lw['norm_ffn2'], lw['w_ffn2_in'], lw['w_ffn2_out'])
    return x, h_new, conv_new, nsa_state


def setup_inputs(seed: int = 0) -> dict:
    key = jax.random.key(seed)
    keys = iter(jax.random.split(key, 64))

    def nrm(shape, scale):
        return scale * jax.random.normal(next(keys), shape, jnp.float32)

    def gain(shape):
        return 1.0 + 0.1 * jax.random.normal(next(keys), shape, jnp.float32)

    n_pages = PAST_LEN // PAGE_SIZE
    n_used = DEC_BATCH * n_pages
    n_pool = n_used + max(1, n_used // 4)
    win_kept = min(WINDOW, PAST_LEN)
    page_table = jax.random.permutation(next(keys), n_pool)[:n_used].reshape(DEC_BATCH, n_pages).astype(jnp.int32)
    dt0 = jnp.exp(jax.random.uniform(next(keys), (DEPTH, SSD_HEADS), jnp.float32, math.log(1e-3), math.log(1e-1)))
    dt_bias = dt0 + jnp.log(-jnp.expm1(-dt0))
    a_log = jnp.log(jax.random.uniform(next(keys), (DEPTH, SSD_HEADS), jnp.float32, 1.0, 16.0))
    return {
        'x_prompt': nrm((BATCH, SEQ, D_MODEL), 1.0),
        'x_sample': nrm((DEC_BATCH, DEC_SEQ, D_MODEL), 1.0),
        'cache_kv_cmp': nrm((DEPTH, n_pool, PAGE_SIZE, 2, NSA_KV_HEADS, HEAD_DIM), 1.0),
        'cache_kv_slc': nrm((DEPTH, n_pool, PAGE_SIZE, 2, NSA_KV_HEADS, HEAD_DIM), 1.0),
        'state_win_kv': nrm((DEPTH, DEC_BATCH, win_kept, 2, NSA_KV_HEADS, HEAD_DIM), 1.0),
        'state_ssd': nrm((DEPTH, DEC_BATCH, SSD_HEADS, HEAD_DIM, SSD_STATE), 0.5),
        'state_conv': nrm((DEPTH, DEC_BATCH, SSD_CONV - 1, SSD_CONV_DIM), 1.0),
        'cache_mem_kv': nrm((DEPTH, DEC_BATCH, MEM_TOKENS, 2, MEM_HEADS, HEAD_DIM), 1.0),
        'page_table': page_table,
        'mem_prompt': nrm((BATCH, MEM_TOKENS, D_MODEL), 1.0),
        'norm_ffn1': gain((DEPTH, D_MODEL)),
        'w_ffn1_in': nrm((DEPTH, D_MODEL, 2 * D_FF), D_MODEL ** -0.5),
        'w_ffn1_out': nrm((DEPTH, D_FF, D_MODEL), D_FF ** -0.5),
        'norm_mix': gain((DEPTH, D_MODEL)),
        'w_in': nrm((DEPTH, D_MODEL, IN_COLS), D_MODEL ** -0.5),
        'conv_w': nrm((DEPTH, SSD_CONV, SSD_CONV_DIM), SSD_CONV ** -0.5),
        'conv_b': nrm((DEPTH, SSD_CONV_DIM), 0.02),
        'dt_bias': dt_bias,
        'a_log': a_log,
        'd_skip': gain((DEPTH, SSD_HEADS)),
        'norm_ssd_out': gain((DEPTH, SSD_INNER)),
        'cmp_pos_k': nrm((DEPTH, CMP_BLOCK, HEAD_DIM), 0.5),
        'cmp_w1_k': nrm((DEPTH, CMP_BLOCK * HEAD_DIM, CMP_HIDDEN), (CMP_BLOCK * HEAD_DIM) ** -0.5),
        'cmp_w2_k': nrm((DEPTH, CMP_HIDDEN, HEAD_DIM), CMP_HIDDEN ** -0.5),
        'cmp_pos_v': nrm((DEPTH, CMP_BLOCK, HEAD_DIM), 0.5),
        'cmp_w1_v': nrm((DEPTH, CMP_BLOCK * HEAD_DIM, CMP_HIDDEN), (CMP_BLOCK * HEAD_DIM) ** -0.5),
        'cmp_w2_v': nrm((DEPTH, CMP_HIDDEN, HEAD_DIM), CMP_HIDDEN ** -0.5),
        'norm_nsa_out': gain((DEPTH, NSA_INNER)),
        'w_out': nrm((DEPTH, MIX_WIDTH, D_MODEL), MIX_WIDTH ** -0.5),
        'norm_mem_q': gain((DEPTH, D_MODEL)),
        'norm_mem_kv': gain((DEPTH, D_MODEL)),
        'w_mem_q': nrm((DEPTH, D_MODEL, MEM_INNER), D_MODEL ** -0.5),
        'w_mem_kv': nrm((DEPTH, D_MODEL, 2 * MEM_INNER), D_MODEL ** -0.5),
        'w_mem_out': nrm((DEPTH, MEM_INNER, D_MODEL), MEM_INNER ** -0.5),
        'norm_ffn2': gain((DEPTH, D_MODEL)),
        'w_ffn2_in': nrm((DEPTH, D_MODEL, 2 * D_FF), D_MODEL ** -0.5),
        'w_ffn2_out': nrm((DEPTH, D_FF, D_MODEL), D_FF ** -0.5),
        'rel_bias': nrm((REL_BUCKETS, NSA_HEADS), 0.5),
        'norm_final': gain((D_MODEL,)),
    }


def reference(x_prompt, x_sample, cache_kv_cmp, cache_kv_slc, state_win_kv, state_ssd, state_conv, cache_mem_kv,
              page_table, mem_prompt, norm_ffn1, w_ffn1_in, w_ffn1_out, norm_mix, w_in, conv_w, conv_b, dt_bias,
              a_log, d_skip, norm_ssd_out, cmp_pos_k, cmp_w1_k, cmp_w2_k, cmp_pos_v, cmp_w1_v, cmp_w2_v,
              norm_nsa_out, w_out, norm_mem_q, norm_mem_kv, w_mem_q, w_mem_kv, w_mem_out, norm_ffn2, w_ffn2_in,
              w_ffn2_out, rel_bias, norm_final):
    xp, xs = x_prompt, x_sample
    outs_p = [[] for _ in range(6)]
    outs_s = [[] for _ in range(5)]
    for layer in range(DEPTH):
        lw = {
            'norm_ffn1': norm_ffn1[layer], 'w_ffn1_in': w_ffn1_in[layer], 'w_ffn1_out': w_ffn1_out[layer],
            'norm_mix': norm_mix[layer], 'w_in': w_in[layer], 'conv_w': conv_w[layer], 'conv_b': conv_b[layer],
            'dt_bias': dt_bias[layer], 'a_log': a_log[layer], 'd_skip': d_skip[layer],
            'norm_ssd_out': norm_ssd_out[layer], 'norm_nsa_out': norm_nsa_out[layer], 'w_out': w_out[layer],
            'norm_mem_q': norm_mem_q[layer], 'w_mem_q': w_mem_q[layer], 'w_mem_out': w_mem_out[layer],
            'norm_ffn2': norm_ffn2[layer], 'w_ffn2_in': w_ffn2_in[layer], 'w_ffn2_out': w_ffn2_out[layer],
        }
        cmp_params = (cmp_pos_k[layer], cmp_w1_k[layer], cmp_w2_k[layer], cmp_pos_v[layer], cmp_w1_v[layer], cmp_w2_v[layer])
        mem_kv_p = mem_kv_proj(mem_prompt, norm_mem_kv[layer], w_mem_kv[layer])
        conv0 = jnp.zeros((xp.shape[0], SSD_CONV - 1, SSD_CONV_DIM), xp.dtype)
        h0 = jnp.zeros((xp.shape[0], SSD_HEADS, HEAD_DIM, SSD_STATE), xp.dtype)
        nsa_p = functools.partial(nsa_prompt, cmp_params=cmp_params, rel_bias=rel_bias)
        xp, h_p, conv_p, (c_p, s_p, w_p) = trunk_layer(xp, lw, conv0, h0, nsa_p, mem_kv_p)
        nsa_s = functools.partial(nsa_sample, cmp_params=cmp_params, rel_bias=rel_bias, pool_c=cache_kv_cmp[layer],
                                  pool_s=cache_kv_slc[layer], win_buf=state_win_kv[layer], page_table=page_table)
        xs, h_s, conv_s, (c_s, s_s, w_s) = trunk_layer(xs, lw, state_conv[layer], state_ssd[layer], nsa_s, cache_mem_kv[layer])
        for lst, v in zip(outs_p, (c_p, s_p, w_p, h_p, conv_p, mem_kv_p)):
            lst.append(v)
        for lst, v in zip(outs_s, (c_s, s_s, w_s, h_s, conv_s)):
            lst.append(v)
    y_prompt = rmsnorm(xp, norm_final)
    y_sample = rmsnorm(xs, norm_final)
    return (y_prompt, y_sample,
            jnp.stack(outs_p[0]), jnp.stack(outs_p[1]), jnp.stack(outs_p[2]), jnp.stack(outs_p[3]),
            jnp.stack(outs_p[4]), jnp.stack(outs_p[5]),
            jnp.stack(outs_s[0]), jnp.stack(outs_s[1]), jnp.stack(outs_s[2]), jnp.stack(outs_s[3]),
            jnp.stack(outs_s[4]))
```

```python
import functools
import math

import numpy as np
import jax
import jax.numpy as jnp
from jax import lax
from jax.experimental import pallas as pl
from jax.experimental.pallas import tpu as pltpu

F32 = jnp.float32
MXU = jnp.bfloat16

HEAD_DIM = 64
SSD_HEADS = 8
SSD_GROUPS = 2
SSD_STATE = 128
SSD_CONV = 4
SSD_CHUNK = 128
NSA_HEADS = 8
NSA_KV_HEADS = 2
NSA_GROUP = NSA_HEADS // NSA_KV_HEADS
CMP_BLOCK = 32
CMP_STRIDE = 16
CMP_HIDDEN = 256
SLC_BLOCK = 64
SLC_TOPN = 16
WINDOW = 512
REL_BUCKETS = 32
REL_MAX_DIST = 128
MEM_HEADS = 4
Q_BLOCK = 128
EPS = 1e-6
NEG = -1e30

SSD_INNER = SSD_HEADS * HEAD_DIM
NSA_INNER = NSA_HEADS * HEAD_DIM
KV_WIDTH = NSA_KV_HEADS * HEAD_DIM
SSD_CONV_DIM = SSD_INNER + 2 * SSD_GROUPS * SSD_STATE
MEM_INNER = MEM_HEADS * HEAD_DIM
LANES = 128
VMEM_LIMIT = 56 * 1024 * 1024


def _cparams(*sem):
    return pltpu.CompilerParams(dimension_semantics=sem, vmem_limit_bytes=VMEM_LIMIT)


def _const_spec(shape):
    nd = len(shape)
    return pl.BlockSpec(shape, lambda *a: (0,) * nd, pipeline_mode=pl.Buffered(1))


def _mm(a, b):
    return jnp.dot(a.astype(MXU), b.astype(MXU), preferred_element_type=F32)


def _mm_nt(a, b):
    return lax.dot_general(a.astype(MXU), b.astype(MXU), (((1,), (1,)), ((), ())), preferred_element_type=F32)


def _rms(x, g):
    return x * lax.rsqrt(jnp.mean(x * x, axis=-1, keepdims=True) + EPS) * g


def _silu(x):
    return x * jax.nn.sigmoid(x)


FFN_CHUNK = 256


def _ffn_kernel(x_ref, g_ref, win_ref, wout_ref, *rest, nchunks, final_norm):
    if final_norm:
        gf_ref, o_ref, h_ref = rest
    else:
        o_ref, h_ref = rest
    x = x_ref[...]
    h_ref[...] = _rms(x, g_ref[...]).astype(h_ref.dtype)
    o_ref[...] = jnp.zeros_like(o_ref)

    def body(j, c):
        h = h_ref[...]
        gate = jnp.dot(h, win_ref[j], preferred_element_type=F32)
        up = jnp.dot(h, win_ref[nchunks + j], preferred_element_type=F32)
        a = (_silu(gate) * up).astype(h.dtype)
        o_ref[...] += jnp.dot(a, wout_ref[j], preferred_element_type=F32)
        return c

    lax.fori_loop(0, nchunks, body, 0)
    y = x_ref[...] + 0.5 * o_ref[...]
    if final_norm:
        y = _rms(y, gf_ref[...])
    o_ref[...] = y


def _ffn(x, g, w_in, w_out, g_final=None, tm=512):
    t, d = x.shape
    f = w_out.shape[0]
    nchunks = f // FFN_CHUNK
    tm = min(tm, t)
    assert nchunks * FFN_CHUNK == f and t % tm == 0
    win = w_in.astype(MXU).reshape(d, 2 * nchunks, FFN_CHUNK).transpose(1, 0, 2)
    wout = w_out.astype(MXU).reshape(nchunks, FFN_CHUNK, d)
    final_norm = g_final is not None
    args = [x, g.reshape(1, d), win, wout]
    specs = [pl.BlockSpec((tm, d), lambda i: (i, 0)), _const_spec((1, d)),
             _const_spec(win.shape), _const_spec(wout.shape)]
    if final_norm:
        args.append(g_final.reshape(1, d))
        specs.append(_const_spec((1, d)))
    return pl.pallas_call(
        functools.partial(_ffn_kernel, nchunks=nchunks, final_norm=final_norm),
        grid=(t // tm,),
        in_specs=specs,
        out_specs=pl.BlockSpec((tm, d), lambda i: (i, 0)),
        out_shape=jax.ShapeDtypeStruct((t, d), F32),
        scratch_shapes=[pltpu.VMEM((tm, d), MXU)],
        compiler_params=_cparams("parallel"),
    )(*args)


PROJ_COLS = 3200
_O_Z, _O_XBC, _O_Q, _O_KVC, _O_KVS, _O_KVW, _O_G, _O_DT = 0, 512, 1536, 2048, 2304, 2560, 2816, 3072


def _regroup_w_in(w_in):
    o = np.cumsum([0, SSD_INNER, SSD_CONV_DIM, SSD_HEADS, NSA_INNER] + [KV_WIDTH] * 6 + [3 * NSA_HEADS])
    z, xbc, dt, q, kvs, gl = (w_in[:, o[0]:o[1]], w_in[:, o[1]:o[2]], w_in[:, o[2]:o[3]], w_in[:, o[3]:o[4]],
                              w_in[:, o[4]:o[10]], w_in[:, o[10]:o[11]])
    pad = lambda w: jnp.pad(w, ((0, 0), (0, LANES - w.shape[1])))
    ng = 3 * NSA_GROUP
    gls = [pad(gl[:, g * ng:(g + 1) * ng]) for g in range(NSA_KV_HEADS)]
    return jnp.concatenate([z, xbc, q, kvs] + gls + [pad(dt)], axis=1).astype(MXU)


def _proj_kernel(x_ref, g_ref, w_ref, z_ref, xbc_ref, q_ref, kvc_ref, kvs_ref, kvw_ref, gate_ref, dt_ref, *aug,
                 tiles_per_seq):
    h = _rms(x_ref[...], g_ref[...]).astype(w_ref.dtype)
    res = jnp.dot(h, w_ref[...], preferred_element_type=F32)
    tm = res.shape[0]
    z_ref[...] = res[:, _O_Z:_O_XBC]
    xbc_ref[...] = res[:, _O_XBC:_O_Q]
    q_ref[...] = res[:, _O_Q:_O_KVC] * (HEAD_DIM ** -0.5)
    kvc_ref[...] = res[:, _O_KVC:_O_KVS]
    kvs = res[:, _O_KVS:_O_KVW]
    kvw = res[:, _O_KVW:_O_G]
    kvs_ref[...] = kvs
    kvw_ref[...] = kvw
    gate_ref[...] = jax.nn.sigmoid(res[:, _O_G:_O_DT])
    dt_ref[...] = res[:, _O_DT:PROJ_COLS]
    if aug:
        ksa_ref, vsa_ref, kw_ref, vwa_ref = aug
        lane = lax.broadcasted_iota(jnp.int32, (tm, LANES), 1)
        row = lax.broadcasted_iota(jnp.int32, (tm, LANES), 0)
        pos = (pl.program_id(0) % tiles_per_seq) * tm + row
        onehot = jnp.where(lane == pos // SLC_BLOCK, 1.0, 0.0)
        ones_col = jnp.where(lane == 0, 1.0, 0.0)
        dt_ = ksa_ref.dtype
        ksa_ref[:, 0:LANES] = kvs[:, 0:LANES].astype(dt_)
        ksa_ref[:, LANES:] = onehot.astype(dt_)
        vsa_ref[:, 0:LANES] = kvs[:, LANES:].astype(dt_)
        vsa_ref[:, LANES:] = ones_col.astype(dt_)
        kw_ref[...] = kvw[:, 0:LANES].astype(dt_)
        vwa_ref[:, 0:LANES] = kvw[:, LANES:].astype(dt_)
        vwa_ref[:, LANES:] = ones_col.astype(dt_)


def _proj(x, g, w, seq_len=None, tm=512):
    t, d = x.shape
    tm = min(tm, t)
    assert t % tm == 0
    widths = [512, 1024, 512, 256, 256, 256, NSA_KV_HEADS * LANES, LANES]
    shapes = [jax.ShapeDtypeStruct((t, wd), F32) for wd in widths]
    aug = seq_len is not None
    if aug:
        assert seq_len % tm == 0 and seq_len // SLC_BLOCK <= LANES
        shapes += [jax.ShapeDtypeStruct((t, wd), MXU) for wd in (256, 256, LANES, 256)]
    return pl.pallas_call(
        functools.partial(_proj_kernel, tiles_per_seq=(seq_len // tm if aug else 1)),
        grid=(t // tm,),
        in_specs=[pl.BlockSpec((tm, d), lambda i: (i, 0)), _const_spec((1, d)), _const_spec(w.shape)],
        out_specs=[pl.BlockSpec((tm, s.shape[1]), lambda i: (i, 0)) for s in shapes],
        out_shape=shapes,
        compiler_params=_cparams("parallel"),
    )(x, g.reshape(1, d), w)


HALO = 8


def _split3(x):
    hi = x.astype(jnp.bfloat16).astype(F32)
    r = x - hi
    mid = r.astype(jnp.bfloat16).astype(F32)
    return hi, mid, r - mid


def _ssd_kernel(z_ref, xbc_ref, dt_ref, conv0_ref, h0_ref, cw_ref, cb_ref, dtb_ref, alog_ref, dskip_ref, g_ref,
                y_ref, hout_ref, convout_ref, xwin_ref, h_ref, *, lv):
    q = SSD_CHUNK
    c = pl.program_id(1)

    @pl.when(c == 0)
    def _():
        xwin_ref[0:HALO, :] = conv0_ref[0]
        h_ref[...] = h0_ref[0]

    def padrows(v):
        if lv == q:
            return v
        return jnp.concatenate([v, jnp.zeros((q - lv, v.shape[1]), v.dtype)], axis=0)

    xwin_ref[HALO:HALO + q, :] = padrows(xbc_ref[...])
    acc = cb_ref[...] + cw_ref[SSD_CONV - 1:SSD_CONV, :] * xwin_ref[HALO:HALO + q, :]
    for k in range(SSD_CONV - 1):
        acc = acc + cw_ref[k:k + 1, :] * xwin_ref[pl.ds(HALO - (SSD_CONV - 1) + k, q), :]
    xc = _silu(acc)
    tail = xwin_ref[lv:lv + HALO, :]
    xwin_ref[0:HALO, :] = tail
    convout_ref[0] = tail

    xs = xc[:, 0:SSD_INNER]
    rowi = lax.broadcasted_iota(jnp.int32, (q, q), 0)
    coli = lax.broadcasted_iota(jnp.int32, (q, q), 1)
    causal = rowi >= coli
    tri = jnp.where(causal, 1.0, 0.0).astype(jnp.bfloat16)

    xdt = padrows(dt_ref[...]) + dtb_ref[...]
    dt = jnp.maximum(xdt, 0.0) + jnp.log1p(jnp.exp(-jnp.abs(xdt)))
    if lv != q:
        dt = jnp.where(rowi < lv, dt, 0.0)
    da = dt * (-jnp.exp(alog_ref[...]))
    acs = sum(jnp.dot(tri, p.astype(jnp.bfloat16), preferred_element_type=F32) for p in _split3(da))
    acs_t = acs.T
    dt_t = dt.T
    last = acs[q - 1:q, :]
    last_t = acs_t[:, q - 1:q]
    xs_t = [xs[:, p * LANES:(p + 1) * LANES].T for p in range(SSD_HEADS // 2)]
    lane_hi = lax.broadcasted_iota(jnp.int32, (q, LANES), 1) >= HEAD_DIM
    row_hi = lax.broadcasted_iota(jnp.int32, (LANES, q), 0) >= HEAD_DIM
    hpg = SSD_HEADS // SSD_GROUPS
    ys = []
    for g in range(SSD_GROUPS):
        bm = xc[:, SSD_INNER + g * SSD_STATE:SSD_INNER + (g + 1) * SSD_STATE]
        cm = xc[:, SSD_INNER + (SSD_GROUPS + g) * SSD_STATE:SSD_INNER + (SSD_GROUPS + g + 1) * SSD_STATE]
        cb = _mm_nt(cm, bm)
        for pp in range(hpg // 2):
            p = g * (hpg // 2) + pp
            xpair = xs[:, p * LANES:(p + 1) * LANES]
            yi = []
            for hh in range(2):
                h = 2 * p + hh
                seg = acs[:, h:h + 1] - acs_t[h:h + 1, :]
                decay = jnp.where(causal, jnp.exp(jnp.where(causal, seg, 0.0)), 0.0)
                yi.append(_mm(cb * decay * dt_t[h:h + 1, :], xpair))
            y_intra = jnp.where(lane_hi, yi[1], yi[0])
            h0, h1 = 2 * p, 2 * p + 1
            grow = jnp.where(lane_hi, jnp.exp(acs[:, h1:h1 + 1]), jnp.exp(acs[:, h0:h0 + 1]))
            y_inter = _mm_nt(cm, h_ref[p]) * grow
            ys.append(y_intra + y_inter)
            te = jnp.where(row_hi,
                           jnp.exp(last_t[h1:h1 + 1, :] - acs_t[h1:h1 + 1, :]) * dt_t[h1:h1 + 1, :],
                           jnp.exp(last_t[h0:h0 + 1, :] - acs_t[h0:h0 + 1, :]) * dt_t[h0:h0 + 1, :])
            states = _mm(xs_t[p] * te, bm)
            rdec = jnp.where(row_hi[:, 0:1], jnp.exp(last_t[h1:h1 + 1, :]), jnp.exp(last_t[h0:h0 + 1, :]))
            h_ref[p] = h_ref[p] * rdec + states
    y = jnp.concatenate(ys, axis=1) + dskip_ref[...] * xs
    y = _rms(y * _silu(padrows(z_ref[...])), g_ref[...])
    y_ref[...] = y[0:lv, :]
    hout_ref[0] = h_ref[...]


def _ssd(z, xbc, dt, conv0, h0, conv_w, conv_b, dt_bias, a_log, d_skip, norm_g, nseq, seq_len):
    q = SSD_CHUNK
    lv = min(q, seq_len)
    nch = seq_len // lv
    assert nch * lv == seq_len
    t = nseq * seq_len
    cdim = SSD_CONV_DIM
    keep = SSD_CONV - 1
    conv0p = jnp.pad(conv0, ((0, 0), (HALO - keep, 0), (0, 0)))
    padl = lambda v: jnp.pad(v.reshape(1, -1), ((0, 0), (0, LANES - v.size)))
    hp = h0.reshape(nseq, SSD_HEADS // 2, 2 * HEAD_DIM, SSD_STATE)
    row = lambda b, c: (b * nch + c, 0)
    y, hout, convout = pl.pallas_call(
        functools.partial(_ssd_kernel, lv=lv),
        grid=(nseq, nch),
        in_specs=[pl.BlockSpec((lv, SSD_INNER), row), pl.BlockSpec((lv, cdim), row), pl.BlockSpec((lv, LANES), row),
                  pl.BlockSpec((1, HALO, cdim), lambda b, c: (b, 0, 0)),
                  pl.BlockSpec((1,) + hp.shape[1:], lambda b, c: (b, 0, 0, 0)),
                  _const_spec((SSD_CONV, cdim)), _const_spec((1, cdim)), _const_spec((1, LANES)),
                  _const_spec((1, LANES)), _const_spec((1, SSD_INNER)), _const_spec((1, SSD_INNER))],
        out_specs=[pl.BlockSpec((lv, SSD_INNER), row),
                   pl.BlockSpec((1,) + hp.shape[1:], lambda b, c: (b, 0, 0, 0)),
                   pl.BlockSpec((1, HALO, cdim), lambda b, c: (b, 0, 0))],
        out_shape=[jax.ShapeDtypeStruct((t, SSD_INNER), F32), jax.ShapeDtypeStruct(hp.shape, F32),
                   jax.ShapeDtypeStruct((nseq, HALO, cdim), F32)],
        scratch_shapes=[pltpu.VMEM((HALO + q, cdim), F32), pltpu.VMEM(hp.shape[1:], F32)],
        compiler_params=_cparams("parallel", "arbitrary"),
    )(z, xbc, dt, conv0p, hp, conv_w, conv_b.reshape(1, cdim), padl(dt_bias), padl(a_log),
      jnp.repeat(d_skip, HEAD_DIM).reshape(1, SSD_INNER), norm_g.reshape(1, SSD_INNER))
    return y, hout.reshape(nseq, SSD_HEADS, HEAD_DIM, SSD_STATE), convout[:, HALO - keep:, :]


def _compress_weights(pe_k, w1_k, w2_k, pe_v, w1_v, w2_v):
    def w1_pair(w1):
        w = w1.reshape(CMP_BLOCK, HEAD_DIM, CMP_HIDDEN)
        zz = jnp.zeros_like(w)
        return jnp.concatenate([jnp.concatenate([w, zz], axis=2), jnp.concatenate([zz, w], axis=2)], axis=1).astype(MXU)

    def w2_pair(w2):
        zz = jnp.zeros_like(w2)
        return jnp.concatenate([jnp.concatenate([w2, zz], axis=1), jnp.concatenate([zz, w2], axis=1)], axis=0).astype(MXU)

    pe = jnp.concatenate([pe_k, pe_k, pe_v, pe_v], axis=1)
    return pe, w1_pair(w1_k), w1_pair(w1_v), w2_pair(w2_k), w2_pair(w2_v)


def _compress_compute(rows_ref, pe_ref, wk_ref, wv_ref, w2k_ref, w2v_ref, sh_ref, kc_ref, vc_ref, nc):
    half = CMP_STRIDE
    rw = 2 * KV_WIDTH
    a0k = a1k = a0v = a1v = None
    for s in range(half):
        x = rows_ref[:, s * rw:(s + 1) * rw]
        x0 = (x + pe_ref[s:s + 1, :]).astype(MXU)
        x1 = (x + pe_ref[half + s:half + s + 1, :]).astype(MXU)
        t0k = jnp.dot(x0[:, 0:LANES], wk_ref[s], preferred_element_type=F32)
        t0v = jnp.dot(x0[:, LANES:], wv_ref[s], preferred_element_type=F32)
        t1k = jnp.dot(x1[:, 0:LANES], wk_ref[half + s], preferred_element_type=F32)
        t1v = jnp.dot(x1[:, LANES:], wv_ref[half + s], preferred_element_type=F32)
        a0k, a1k = (t0k, t1k) if s == 0 else (a0k + t0k, a1k + t1k)
        a0v, a1v = (t0v, t1v) if s == 0 else (a0v + t0v, a1v + t1v)
    sh_ref[nc:nc + 8, :] = jnp.zeros((8, sh_ref.shape[1]), F32)
    for a0, a1, w2_ref, o_ref in ((a0k, a1k, w2k_ref, kc_ref), (a0v, a1v, w2v_ref, vc_ref)):
        sh_ref[0:nc, :] = a1
        hid = a0 + sh_ref[pl.ds(1, nc), :]
        o_ref[...] = jnp.dot(_silu(hid).astype(MXU), w2_ref[...], preferred_element_type=F32)


def _compress_prompt_kernel(rows_ref, pe_ref, wk_ref, wv_ref, w2k_ref, w2v_ref, kc_ref, vc_ref, sh_ref, *, nc):
    _compress_compute(rows_ref, pe_ref, wk_ref, wv_ref, w2k_ref, w2v_ref, sh_ref, kc_ref.at[0], vc_ref.at[0], nc)


def _compress_specs(cw):
    return [_const_spec(w.shape) for w in cw]


def _compress_prompt(kv_cmp, cw, nseq, seq_len):
    nc = seq_len // CMP_STRIDE
    out = jax.ShapeDtypeStruct((nseq, nc, LANES), F32)
    return pl.pallas_call(
        functools.partial(_compress_prompt_kernel, nc=nc),
        grid=(nseq,),
        in_specs=[pl.BlockSpec((nc, CMP_STRIDE * 2 * KV_WIDTH), lambda b: (b, 0))] + _compress_specs(cw),
        out_specs=[pl.BlockSpec((1, nc, LANES), lambda b: (b, 0, 0))] * 2,
        out_shape=[out, out],
        scratch_shapes=[pltpu.VMEM((nc + 8, NSA_KV_HEADS * CMP_HIDDEN), F32)],
        compiler_params=_cparams("parallel"),
    )(kv_cmp.reshape(nseq * nc, CMP_STRIDE * 2 * KV_WIDTH), *cw)


def _page_copy(pool_ref, pt_ref, buf_ref, sem_ref, seq, slot, p, npages, page_size):
    return pltpu.make_async_copy(pool_ref.at[pt_ref[seq * npages + p]],
                                 buf_ref.at[slot, pl.ds(p * page_size, page_size), :], sem_ref.at[slot])


def _gather_step(pool_ref, pt_ref, buf_ref, sem_ref, npages, page_size):
    b = pl.program_id(0)
    nb = pl.num_programs(0)
    slot = b % 2

    def start_all(seq, sl):
        def body(p, c):
            _page_copy(pool_ref, pt_ref, buf_ref, sem_ref, seq, sl, p, npages, page_size).start()
            return c
        lax.fori_loop(0, npages, body, 0)

    @pl.when(b == 0)
    def _():
        start_all(0, 0)

    @pl.when(b + 1 < nb)
    def _():
        start_all(b + 1, 1 - slot)

    def wbody(p, c):
        _page_copy(pool_ref, pt_ref, buf_ref, sem_ref, b, slot, p, npages, page_size).wait()
        return c
    lax.fori_loop(0, npages, wbody, 0)
    return slot


def _compress_sample_kernel(pt_ref, pool_ref, pe_ref, wk_ref, wv_ref, w2k_ref, w2v_ref, kc_ref, vc_ref,
                            buf_ref, sem_ref, sh_ref, *, nc, npages, page_size):
    slot = _gather_step(pool_ref, pt_ref, buf_ref, sem_ref, npages, page_size)
    _compress_compute(buf_ref.at[slot], pe_ref, wk_ref, wv_ref, w2k_ref, w2v_ref, sh_ref,
                      kc_ref.at[0], vc_ref.at[0], nc)


def _compress_sample(pool, page_table, cw):
    nseq, npages = page_table.shape
    page_size = pool.shape[1]
    past = npages * page_size
    nc = past // CMP_STRIDE
    chunk_w = CMP_STRIDE * 2 * KV_WIDTH
    page_chunks = page_size // CMP_STRIDE
    assert page_chunks * CMP_STRIDE == page_size
    out = jax.ShapeDtypeStruct((nseq, nc, LANES), F32)
    cspecs = [pl.BlockSpec(w.shape, lambda b, pt, nd=w.ndim: (0,) * nd, pipeline_mode=pl.Buffered(1)) for w in cw]
    return pl.pallas_call(
        functools.partial(_compress_sample_kernel, nc=nc, npages=npages, page_size=page_chunks),
        grid_spec=pltpu.PrefetchScalarGridSpec(
            num_scalar_prefetch=1,
            grid=(nseq,),
            in_specs=[pl.BlockSpec(memory_space=pl.ANY)] + cspecs,
            out_specs=[pl.BlockSpec((1, nc, LANES), lambda b, pt: (b, 0, 0))] * 2,
            scratch_shapes=[pltpu.VMEM((2, nc, chunk_w), F32), pltpu.SemaphoreType.DMA((2,)),
                            pltpu.VMEM((nc + 8, NSA_KV_HEADS * CMP_HIDDEN), F32)],
        ),
        out_shape=[out, out],
        compiler_params=_cparams("arbitrary"),
    )(page_table.reshape(-1), pool.reshape(pool.shape[0], page_chunks, chunk_w), *cw)


def _bucket_thresholds():
    n = np.arange(0, 2 * REL_MAX_DIST)
    exact = REL_BUCKETS // 2
    out = []
    for ft in (np.float32, np.float64):
        nf = np.maximum(n, exact).astype(ft)
        large = exact + (np.log(nf / ft(exact)) / ft(math.log(REL_MAX_DIST / exact)) * ft(REL_BUCKETS - exact)).astype(np.int64)
        out.append(np.where(n < exact, n, np.minimum(large, REL_BUCKETS - 1)))
    assert (out[0] == out[1]).all() and out[1][REL_MAX_DIST] == REL_BUCKETS - 1
    return [int(np.argmax(out[1] >= b)) for b in range(1, REL_BUCKETS)]


_THR = _bucket_thresholds()


def _bias_tables(dist, rb_ref, heads):
    steps = [jnp.where(dist >= _THR[b - 1], 1.0, 0.0) for b in range(1, REL_BUCKETS)]
    far = dist >= REL_MAX_DIST
    out = []
    for h in heads:
        acc = jnp.full(dist.shape, rb_ref[0, h] - rb_ref[REL_BUCKETS - 1, h], F32)
        for b in range(1, REL_BUCKETS):
            acc = acc + steps[b - 1] * (rb_ref[b, h] - rb_ref[b - 1, h])
        out.append(jnp.where(dist < 0, NEG, jnp.where(far, 0.0, acc)))
    return out


BIG = 1e30
KT = 256
N_BAND = 20
BAND_LO = 12
F_STEP = 2 * N_BAND
N_WK = 6
N_SK = 4
WIN_TILES = WINDOW // Q_BLOCK + 1


def _nsa_tables(rb_ref, st_ref, wt_ref, qft_ref, ov_ref, ncp):
    tq = Q_BLOCK
    heads = range(NSA_HEADS)
    t2 = lax.broadcasted_iota(jnp.int32, (tq, KT), 0)
    c2 = lax.broadcasted_iota(jnp.int32, (tq, KT), 1)
    for dd in range(N_SK - 1):
        tabs = _bias_tables(tq * dd + t2 - c2, rb_ref, heads)
        for h in heads:
            st_ref[h * N_SK + dd] = tabs[h]
    t1 = lax.broadcasted_iota(jnp.int32, (tq, LANES), 0)
    c1 = lax.broadcasted_iota(jnp.int32, (tq, LANES), 1)
    zeros = jnp.zeros((tq, LANES), F32)
    for d in range(2):
        tabs = _bias_tables(tq * d + t1 - c1, rb_ref, heads)
        for h in heads:
            wt_ref[h * N_WK + d] = tabs[h]
    for h in heads:
        st_ref[h * N_SK + N_SK - 1] = jnp.zeros((tq, KT), F32)
        wt_ref[h * N_WK + 2] = zeros
        wt_ref[h * N_WK + 3] = zeros
        wt_ref[h * N_WK + 4] = jnp.where(c1 > t1, 0.0, NEG)
        wt_ref[h * N_WK + 5] = zeros + NEG
    for g in range(NSA_KV_HEADS):
        f = c1 - (HEAD_DIM if g == 0 else 0)
        is_lo = f >= N_BAND
        u = jnp.where(is_lo, f - N_BAND, f) - BAND_LO
        live = (f >= 0) & (f < 2 * N_BAND)
        tabs = _bias_tables(t1 - CMP_STRIDE * u - (CMP_BLOCK - 1), rb_ref, range(g * NSA_GROUP, (g + 1) * NSA_GROUP))
        for r in range(NSA_GROUP):
            hi = tabs[r].astype(jnp.bfloat16).astype(F32)
            lo = jnp.where(tabs[r] <= NEG, 0.0, tabs[r] - hi)
            val = jnp.where(live, jnp.where(is_lo, lo, hi), 0.0)
            qft_ref[g * NSA_GROUP + r] = jnp.where(f == F_STEP, NEG, val)
    kk = lax.broadcasted_iota(jnp.int32, (ncp, LANES), 0)
    ss = lax.broadcasted_iota(jnp.int32, (ncp, LANES), 1)
    ratio = SLC_BLOCK // CMP_STRIDE
    nover = (CMP_BLOCK - 1) // CMP_STRIDE
    ov = (kk >= ratio * ss - nover) & (kk < ratio * ss + ratio) & (kk < ncp - 1)
    ov_ref[...] = jnp.where(ov, 1.0, 0.0).astype(ov_ref.dtype)


def _lane_half(rows):
    return lax.broadcasted_iota(jnp.int32, (rows, LANES), 1) // HEAD_DIM


def _place_heads(qg, g):
    keep = _lane_half(qg.shape[0]) == g
    out = []
    for r in range(NSA_GROUP):
        v = qg[:, (r // 2) * LANES:(r // 2 + 1) * LANES]
        v = jnp.where(g == r % 2, v, pltpu.roll(v, HEAD_DIM, 1))
        out.append(jnp.where(keep, v, 0.0))
    return out


def _unplace_heads(ohs, g):
    keep = _lane_half(ohs[0].shape[0]) == g
    cols = []
    for c in range(NSA_GROUP // 2):
        pair = []
        for r in (2 * c, 2 * c + 1):
            oh = jnp.where(keep, ohs[r], 0.0)
            pair.append(jnp.where(g == r % 2, oh, pltpu.roll(oh, HEAD_DIM, 1)))
        cols.append(pair[0] + pair[1])
    return jnp.concatenate(cols, axis=1)


def _gate_mix(gates, r, o_cmp, o_slc, o_win):
    return (gates[:, 3 * r:3 * r + 1] * o_cmp + gates[:, 3 * r + 1:3 * r + 2] * o_slc
            + gates[:, 3 * r + 2:3 * r + 3] * o_win)


def _top_blocks(imp, qpos_lane):
    vals = imp.T
    srow = lax.broadcasted_iota(jnp.int32, vals.shape, 0)
    qblk = qpos_lane // SLC_BLOCK
    forced = (srow == 0) | (srow == qblk) | (srow == qblk - 1)
    vals = jnp.where(forced, 3e38, vals)
    vals = jnp.where(srow <= qblk, vals, -1.0)
    srow_f = srow.astype(F32)
    sel = jnp.zeros(vals.shape, F32)
    for _ in range(SLC_TOPN):
        mx = jnp.max(vals, axis=0, keepdims=True)
        idx = jnp.min(jnp.where(vals == mx, srow_f, float(LANES)), axis=0, keepdims=True)
        pick = srow_f == idx
        sel = jnp.where(pick, 1.0, sel)
        vals = jnp.where(pick, -2.0, vals)
    return sel.T


def _nsa_prompt_kernel(rb_ref, q_ref, gate_ref, kc_ref, vc_ref, ksa_ref, vsa_ref, kw_ref, vwa_ref, o_ref,
                       st_ref, wt_ref, qft_ref, ov_ref, lbuf_ref, m_ref, acc_ref, *, ncp):
    tq = Q_BLOCK
    i = pl.program_id(1)
    g = pl.program_id(2)

    @pl.when((pl.program_id(0) == 0) & (i == 0) & (g == 0))
    def _():
        _nsa_tables(rb_ref, st_ref, wt_ref, qft_ref, ov_ref, ncp)

    gates = gate_ref[...]
    qpos_lane = i * tq + lax.broadcasted_iota(jnp.int32, (tq, LANES), 1)
    trow = lax.broadcasted_iota(jnp.int32, (NSA_GROUP * tq, 1), 0) % tq
    row_live = (i * tq + trow) >= CMP_BLOCK - 1
    kidx = lax.broadcasted_iota(jnp.int32, (ncp, LANES), 0)
    klane = lax.broadcasted_iota(jnp.int32, (ncp, LANES), 1)
    nfar = jnp.maximum(i // 2 - 1, 0)
    ntile = i // 2 + 1
    j0 = jnp.maximum(i - (WIN_TILES - 1), 0)
    qp = _place_heads(q_ref[...], g)
    heads = [g * NSA_GROUP + r for r in range(NSA_GROUP)]

    f = klane - HEAD_DIM * (1 - g)
    u12 = kidx - (tq // CMP_STRIDE) * i + BAND_LO
    fa = jnp.where(f >= N_BAND, f - N_BAND, f)
    feat = ((fa == u12) & (f >= 0) & (f < 2 * N_BAND)) | ((f == F_STEP) & (u12 >= N_BAND - 1))
    ka = jnp.where(klane // HEAD_DIM == g, kc_ref[0], jnp.where(feat, 1.0, 0.0)).astype(MXU)
    qa = jnp.concatenate([qp[r] + qft_ref[heads[r]] for r in range(NSA_GROUP)], axis=0).astype(MXU)
    lc = _mm_nt(qa, ka)
    mc = jnp.max(lc, axis=1, keepdims=True)
    ec = jnp.exp(lc - mc)
    sc = jnp.sum(ec, axis=1, keepdims=True)
    pc = (ec * jnp.where(row_live, 1.0 / sc, 0.0)).astype(MXU)
    o_cmp = jnp.dot(pc, vc_ref[0].astype(MXU), preferred_element_type=F32)
    imp = sum(jnp.dot(pc[r * tq:(r + 1) * tq], ov_ref[...], preferred_element_type=F32) for r in range(NSA_GROUP))

    selneg = (_top_blocks(imp, qpos_lane) - 1.0) * BIG

    qs = jnp.concatenate([jnp.concatenate([qp[r], selneg], axis=1) for r in range(NSA_GROUP)], axis=0).astype(MXU)
    m_ref[...] = jnp.full(m_ref.shape, NEG, F32)

    def far_tile(j, c):
        lt = _mm_nt(qs, ksa_ref[pl.ds(pl.multiple_of(j * KT, KT), KT), :])
        lbuf_ref[j] = lt
        m_ref[...] = jnp.maximum(m_ref[...], lt)
        return c

    def near_tile(j, c):
        kind = jnp.minimum(i - 2 * j, N_SK - 1)
        tab = jnp.concatenate([st_ref[h * N_SK + kind] for h in heads], axis=0)
        lt = _mm_nt(qs, ksa_ref[pl.ds(pl.multiple_of(j * KT, KT), KT), :]) + tab
        lbuf_ref[j] = lt
        m_ref[...] = jnp.maximum(m_ref[...], lt)
        return c

    lax.fori_loop(0, nfar, far_tile, 0)
    lax.fori_loop(nfar, ntile, near_tile, 0)
    m_ref[...] = jnp.broadcast_to(jnp.max(m_ref[...], axis=1, keepdims=True), m_ref.shape)
    acc_ref[...] = jnp.zeros(acc_ref.shape, F32)

    def pv_tile(j, c):
        e = jnp.exp(lbuf_ref[j] - m_ref[...]).astype(MXU)
        acc_ref[...] += jnp.dot(e, vsa_ref[pl.ds(pl.multiple_of(j * KT, KT), KT), :], preferred_element_type=F32)
        return c

    lax.fori_loop(0, ntile, pv_tile, 0)
    acc = acc_ref[...]
    o_slc = acc[:, 0:LANES] / acc[:, LANES:LANES + 1]

    ws = pl.multiple_of(j0 * tq, tq)
    lw = _mm_nt(jnp.concatenate(qp, axis=0).astype(MXU), kw_ref[pl.ds(ws, WIN_TILES * tq), :])
    kinds = []
    for jj in range(WIN_TILES):
        d = i - j0 - jj
        kinds.append(jnp.where(d < 0, N_WK - 1, d))
    lw = lw + jnp.concatenate(
        [jnp.concatenate([wt_ref[h * N_WK + kinds[jj]] for jj in range(WIN_TILES)], axis=1) for h in heads], axis=0)
    ew = jnp.exp(lw - jnp.max(lw, axis=1, keepdims=True)).astype(MXU)
    accw = jnp.dot(ew, vwa_ref[pl.ds(ws, WIN_TILES * tq), :], preferred_element_type=F32)
    o_win = accw[:, 0:LANES] / accw[:, LANES:LANES + 1]

    ohs = []
    for r in range(NSA_GROUP):
        rows = slice(r * tq, (r + 1) * tq)
        ohs.append(_gate_mix(gates, r, o_cmp[rows], o_slc[rows], o_win[rows]))
    o_ref[...] = _unplace_heads(ohs, g)


def _nsa_prompt(rel_bias, q, gates, kc, vc, ksa, vsa, kw, vwa, nseq, seq_len):
    tq = Q_BLOCK
    nq = seq_len // tq
    ncp = kc.shape[1]
    gw = NSA_GROUP * HEAD_DIM
    assert seq_len % KT == 0 and seq_len >= WIN_TILES * tq and ncp % LANES == 0
    per_seq = lambda width: pl.BlockSpec((seq_len, width), lambda b, i, g: (b, 0), pipeline_mode=pl.Buffered(1))
    blk = lambda width: pl.BlockSpec((tq, width), lambda b, i, g: (b * nq + i, g))
    cmp_spec = pl.BlockSpec((1, ncp, LANES), lambda b, i, g: (b, 0, 0))
    return pl.pallas_call(
        functools.partial(_nsa_prompt_kernel, ncp=ncp),
        grid=(nseq, nq, NSA_KV_HEADS),
        in_specs=[pl.BlockSpec(memory_space=pltpu.SMEM), blk(gw), blk(LANES), cmp_spec, cmp_spec,
                  per_seq(2 * LANES), per_seq(2 * LANES), per_seq(LANES), per_seq(2 * LANES)],
        out_specs=blk(gw),
        out_shape=jax.ShapeDtypeStruct((nseq * seq_len, NSA_INNER), F32),
        scratch_shapes=[pltpu.VMEM((NSA_HEADS * N_SK, tq, KT), F32), pltpu.VMEM((NSA_HEADS * N_WK, tq, LANES), F32),
                        pltpu.VMEM((NSA_HEADS, tq, LANES), F32), pltpu.VMEM((ncp, LANES), MXU),
                        pltpu.VMEM((seq_len // KT, NSA_GROUP * tq, KT), F32),
                        pltpu.VMEM((NSA_GROUP * tq, KT), F32), pltpu.VMEM((NSA_GROUP * tq, KT), F32)],
        compiler_params=_cparams("arbitrary", "arbitrary", "arbitrary"),
    )(rel_bias, q, gates, kc, vc, ksa, vsa, kw, vwa)


TAIL = 2 * LANES


def _nsa_sample_tables(rb_ref, cbt_ref, tail_ref, wtab_ref, ov_ref, eall_ref, *, ls, past, ncp, wk, wkp, nk):
    heads = range(NSA_HEADS)

    def per_group(tabs):
        return [jnp.concatenate(tabs[g * NSA_GROUP:(g + 1) * NSA_GROUP], axis=0) for g in range(NSA_KV_HEADS)]

    t = lax.broadcasted_iota(jnp.int32, (ls, ncp), 0)
    k = lax.broadcasted_iota(jnp.int32, (ls, ncp), 1)
    dist = past + t - CMP_STRIDE * k - (CMP_BLOCK - 1)
    dist = jnp.where(k < ncp - 1, dist, -1)
    for g, tab in enumerate(per_group(_bias_tables(dist, rb_ref, heads))):
        cbt_ref[g] = tab
    t = lax.broadcasted_iota(jnp.int32, (ls, TAIL), 0)
    c = lax.broadcasted_iota(jnp.int32, (ls, TAIL), 1) + (nk - TAIL)
    dist = jnp.where(c < past + ls, past + t - c, -1)
    for g, tab in enumerate(per_group(_bias_tables(dist, rb_ref, heads))):
        tail_ref[g] = tab
    t = lax.broadcasted_iota(jnp.int32, (ls, wkp), 0)
    c = lax.broadcasted_iota(jnp.int32, (ls, wkp), 1)
    dist = past + t - (past - wk + c)
    dist = jnp.where((c < wk + ls) & (dist < WINDOW), dist, -1)
    for g, tab in enumerate(per_group(_bias_tables(dist, rb_ref, heads))):
        wtab_ref[g] = tab
    kk = lax.broadcasted_iota(jnp.int32, (ncp, 2 * LANES), 0)
    ss = lax.broadcasted_iota(jnp.int32, (ncp, 2 * LANES), 1)
    ratio = SLC_BLOCK // CMP_STRIDE
    nover = (CMP_BLOCK - 1) // CMP_STRIDE
    ov = (kk >= ratio * ss - nover) & (kk < ratio * ss + ratio) & (kk < ncp - 1)
    ov_ref[...] = jnp.where(ov, 1.0, 0.0).astype(ov_ref.dtype)
    s2 = lax.broadcasted_iota(jnp.int32, (2 * LANES, nk), 0)
    c2 = lax.broadcasted_iota(jnp.int32, (2 * LANES, nk), 1)
    eall_ref[...] = jnp.where(c2 // SLC_BLOCK == s2, 1.0, 0.0).astype(eall_ref.dtype)


def _softmax_rows(l):
    m = jnp.max(l, axis=1, keepdims=True)
    e = jnp.exp(l - m)
    return e / jnp.sum(e, axis=1, keepdims=True)


def _nsa_sample_kernel(pt_ref, rb_ref, q_ref, gate_ref, kvs_ref, kvw_ref, kc_ref, vc_ref, win_ref, pool_ref,
                       o_ref, winout_ref, buf_ref, sem_ref, kw_ref, cbt_ref, tail_ref, wtab_ref, ov_ref, eall_ref,
                       *, ls, past, npages, page_size, ncp, wk, wkp, nk):
    b = pl.program_id(0)

    @pl.when(b == 0)
    def _():
        _nsa_sample_tables(rb_ref, cbt_ref, tail_ref, wtab_ref, ov_ref, eall_ref,
                           ls=ls, past=past, ncp=ncp, wk=wk, wkp=wkp, nk=nk)
        zeros = jnp.zeros((nk - past, 2 * KV_WIDTH), F32)
        buf_ref[0, past:nk, :] = zeros
        buf_ref[1, past:nk, :] = zeros

    slot = _gather_step(pool_ref, pt_ref, buf_ref, sem_ref, npages, page_size)
    kbuf = buf_ref.at[slot]
    kbuf[past:past + ls, :] = kvs_ref[...]
    kw_ref[0:wk, :] = win_ref[0]
    kw_ref[wk:wkp, :] = jnp.concatenate([kvw_ref[...], jnp.zeros((wkp - wk - ls, 2 * KV_WIDTH), F32)], axis=0)
    winout_ref[0] = kw_ref[ls:ls + wk, :]

    gw = NSA_GROUP * HEAD_DIM
    lane_s = lax.broadcasted_iota(jnp.int32, (ls, 2 * LANES), 1)
    qblk = (past + lax.broadcasted_iota(jnp.int32, (ls, 2 * LANES), 0)) // SLC_BLOCK
    forced = (lane_s == 0) | (lane_s == qblk) | (lane_s == qblk - 1)
    lane_sf = lane_s.astype(F32)
    for g in range(NSA_KV_HEADS):
        gates = gate_ref[:, g * LANES:(g + 1) * LANES]
        qg = jnp.concatenate(_place_heads(q_ref[:, g * gw:(g + 1) * gw], g), axis=0).astype(MXU)
        pc = _softmax_rows(_mm_nt(qg, kc_ref[0]) + cbt_ref[g]).astype(MXU)
        o_cmp = jnp.dot(pc, vc_ref[0].astype(MXU), preferred_element_type=F32)
        impm = jnp.dot(pc, ov_ref[...], preferred_element_type=F32)
        imp = sum(impm[r * ls:(r + 1) * ls] for r in range(NSA_GROUP))
        vals = jnp.where(forced, 3e38, imp)
        vals = jnp.where(lane_s <= qblk, vals, -1.0)
        sel = jnp.zeros(vals.shape, F32)
        for _ in range(SLC_TOPN):
            mx = jnp.max(vals, axis=1, keepdims=True)
            idx = jnp.min(jnp.where(vals == mx, lane_sf, float(2 * LANES)), axis=1, keepdims=True)
            pick = lane_sf == idx
            sel = jnp.where(pick, 1.0, sel)
            vals = jnp.where(pick, -2.0, vals)
        selneg = ((sel - 1.0) * BIG).astype(MXU)
        mask = jnp.dot(jnp.concatenate([selneg] * NSA_GROUP, axis=0), eall_ref[...], preferred_element_type=F32)
        ls_ = _mm_nt(qg, kbuf[:, 0:LANES]) + mask
        ls_ = jnp.concatenate([ls_[:, 0:nk - TAIL], ls_[:, nk - TAIL:] + tail_ref[g]], axis=1)
        ps = _softmax_rows(ls_).astype(MXU)
        o_slc = jnp.dot(ps, kbuf[:, LANES:].astype(MXU), preferred_element_type=F32)
        pw = _softmax_rows(_mm_nt(qg, kw_ref[:, 0:LANES]) + wtab_ref[g]).astype(MXU)
        o_win = jnp.dot(pw, kw_ref[:, LANES:].astype(MXU), preferred_element_type=F32)
        ohs = []
        for r in range(NSA_GROUP):
            rows = slice(r * ls, (r + 1) * ls)
            ohs.append(_gate_mix(gates, r, o_cmp[rows], o_slc[rows], o_win[rows]))
        o_ref[:, g * gw:(g + 1) * gw] = _unplace_heads(ohs, g)


def _nsa_sample(rel_bias, q, gates, kvs, kvw, kc, vc, win_buf, pool, page_table, ls):
    nseq, npages = page_table.shape
    page_size = pool.shape[1]
    past = npages * page_size
    ncp = kc.shape[1]
    wk = win_buf.shape[1]
    wkp = wk + LANES
    nk = past + LANES
    assert past % SLC_BLOCK == 0 and ls % 8 == 0 and ls <= SLC_BLOCK and past >= wk and past // SLC_BLOCK + 1 <= 2 * LANES
    assert ncp % LANES == 0 and past >= CMP_BLOCK
    row = lambda width: pl.BlockSpec((ls, width), lambda b, pt: (b, 0))
    seq3 = lambda n, width: pl.BlockSpec((1, n, width), lambda b, pt: (b, 0, 0))
    kw = 2 * KV_WIDTH
    return pl.pallas_call(
        functools.partial(_nsa_sample_kernel, ls=ls, past=past, npages=npages, page_size=page_size, ncp=ncp,
                          wk=wk, wkp=wkp, nk=nk),
        grid_spec=pltpu.PrefetchScalarGridSpec(
            num_scalar_prefetch=1,
            grid=(nseq,),
            in_specs=[pl.BlockSpec(memory_space=pltpu.SMEM), row(NSA_INNER), row(NSA_KV_HEADS * LANES), row(kw), row(kw),
                      seq3(ncp, LANES), seq3(ncp, LANES), seq3(wk, kw), pl.BlockSpec(memory_space=pl.ANY)],
            out_specs=[row(NSA_INNER), seq3(wk, kw)],
            scratch_shapes=[pltpu.VMEM((2, nk, kw), F32), pltpu.SemaphoreType.DMA((2,)), pltpu.VMEM((wkp, kw), F32),
                            pltpu.VMEM((NSA_KV_HEADS, NSA_GROUP * ls, ncp), F32),
                            pltpu.VMEM((NSA_KV_HEADS, NSA_GROUP * ls, TAIL), F32),
                            pltpu.VMEM((NSA_KV_HEADS, NSA_GROUP * ls, wkp), F32),
                            pltpu.VMEM((ncp, 2 * LANES), MXU), pltpu.VMEM((2 * LANES, nk), MXU)],
        ),
        out_shape=[jax.ShapeDtypeStruct((nseq * ls, NSA_INNER), F32), jax.ShapeDtypeStruct((nseq, wk, kw), F32)],
        compiler_params=_cparams("arbitrary"),
    )(page_table.reshape(-1), rel_bias, q, gates, kvs, kvw, kc, vc, win_buf, pool)


def _mix_out_kernel(x_ref, yssd_ref, onsa_ref, gn_ref, wo_ref, gq_ref, wq_ref, mkv_ref, wmo_ref, o_ref):
    y = jnp.concatenate([yssd_ref[...], _rms(onsa_ref[...], gn_ref[...])], axis=1)
    x = x_ref[...] + jnp.dot(y.astype(MXU), wo_ref[...], preferred_element_type=F32)
    qm = jnp.dot(_rms(x, gq_ref[...]).astype(MXU), wq_ref[...], preferred_element_type=F32)
    kv = mkv_ref[0]
    km = kv[:, 0:MEM_INNER].astype(MXU)
    vm = kv[:, MEM_INNER:].astype(MXU)
    head = lax.broadcasted_iota(jnp.int32, qm.shape, 1) // HEAD_DIM
    o = jnp.zeros(qm.shape, F32)
    for h in range(MEM_HEADS):
        p = _softmax_rows(_mm_nt(jnp.where(head == h, qm, 0.0), km) * (HEAD_DIM ** -0.5))
        o = o + jnp.where(head == h, jnp.dot(p.astype(MXU), vm, preferred_element_type=F32), 0.0)
    o_ref[...] = x + jnp.dot(o.astype(MXU), wmo_ref[...], preferred_element_type=F32)


def _mix_out(x, yssd, onsa, g_nsa, w_out, g_memq, w_mem_q, mem_kv, w_mem_out, nseq, seq_len, tm=512):
    t, d = x.shape
    tm = min(tm, seq_len)
    nt = seq_len // tm
    assert nt * tm == seq_len
    m = mem_kv.shape[1]
    row = lambda width: pl.BlockSpec((tm, width), lambda b, i: (b * nt + i, 0))
    return pl.pallas_call(
        _mix_out_kernel,
        grid=(nseq, nt),
        in_specs=[row(d), row(SSD_INNER), row(NSA_INNER), _const_spec((1, NSA_INNER)), _const_spec(w_out.shape),
                  _const_spec((1, d)), _const_spec(w_mem_q.shape),
                  pl.BlockSpec((1, m, 2 * MEM_INNER), lambda b, i: (b, 0, 0)), _const_spec(w_mem_out.shape)],
        out_specs=row(d),
        out_shape=jax.ShapeDtypeStruct((t, d), F32),
        compiler_params=_cparams("parallel", "parallel"),
    )(x, yssd, onsa, g_nsa.reshape(1, -1), w_out.astype(MXU), g_memq.reshape(1, d), w_mem_q.astype(MXU),
      mem_kv, w_mem_out.astype(MXU))


def _norm_matmul_kernel(x_ref, g_ref, w_ref, o_ref):
    o_ref[...] = jnp.dot(_rms(x_ref[...], g_ref[...]).astype(MXU), w_ref[...], preferred_element_type=F32)


def _norm_matmul(x, g, w, tm=256):
    t, d = x.shape
    n = w.shape[1]
    tm = min(tm, t)
    assert t % tm == 0
    return pl.pallas_call(
        _norm_matmul_kernel,
        grid=(t // tm,),
        in_specs=[pl.BlockSpec((tm, d), lambda i: (i, 0)), _const_spec((1, d)), _const_spec(w.shape)],
        out_specs=pl.BlockSpec((tm, n), lambda i: (i, 0)),
        out_shape=jax.ShapeDtypeStruct((t, n), F32),
        compiler_params=_cparams("parallel"),
    )(x, g.reshape(1, d), w.astype(MXU))


def _trunk_layer(x, lw, cw, rel_bias, mem_kv, conv0, h0, nseq, seq_len, sample=None):
    x1 = _ffn(x, lw['norm_ffn1'], lw['w_ffn1_in'], lw['w_ffn1_out'])
    outs = _proj(x1, lw['norm_mix'], lw['w_mix'], seq_len=None if sample else seq_len)
    z, xbc, q, kvc, kvs, kvw, gates, dt = outs[:8]
    y_ssd, h_new, conv_new = _ssd(z, xbc, dt, conv0, h0, lw['conv_w'], lw['conv_b'], lw['dt_bias'], lw['a_log'],
                                  lw['d_skip'], lw['norm_ssd_out'], nseq, seq_len)
    if sample is None:
        kc, vc = _compress_prompt(kvc, cw, nseq, seq_len)
        o_nsa = _nsa_prompt(rel_bias, q, gates, kc, vc, *outs[8:12], nseq, seq_len)
        wk = min(WINDOW, seq_len)
        win_new = kvw.reshape(nseq, seq_len, -1)[:, seq_len - wk:]
    else:
        kc, vc = _compress_sample(sample['pool_c'], sample['page_table'], cw)
        o_nsa, win_new = _nsa_sample(rel_bias, q, gates, kvs, kvw, kc, vc, sample['win_buf'], sample['pool_s'],
                                     sample['page_table'], seq_len)
    x2 = _mix_out(x1, y_ssd, o_nsa, lw['norm_nsa_out'], lw['w_out'], lw['norm_mem_q'], lw['w_mem_q'], mem_kv,
                  lw['w_mem_out'], nseq, seq_len)
    return x2, kvc, kvs, win_new, h_new, conv_new


def kernel(x_prompt, x_sample, cache_kv_cmp, cache_kv_slc, state_win_kv, state_ssd, state_conv, cache_mem_kv, page_table, mem_prompt, norm_ffn1, w_ffn1_in, w_ffn1_out, norm_mix, w_in, conv_w, conv_b, dt_bias, a_log, d_skip, norm_ssd_out, cmp_pos_k, cmp_w1_k, cmp_w2_k, cmp_pos_v, cmp_w1_v, cmp_w2_v, norm_nsa_out, w_out, norm_mem_q, norm_mem_kv, w_mem_q, w_mem_kv, w_mem_out, norm_ffn2, w_ffn2_in, w_ffn2_out, rel_bias, norm_final):
    bp, lp, d = x_prompt.shape
    bs, lsq, _ = x_sample.shape
    depth = w_in.shape[0]
    n_pool, page_size = cache_kv_cmp.shape[1:3]
    kvw = 2 * KV_WIDTH
    xp = x_prompt.reshape(bp * lp, d)
    xs = x_sample.reshape(bs * lsq, d)
    outs_p = [[] for _ in range(6)]
    outs_s = [[] for _ in range(5)]
    for layer in range(depth):
        last = layer == depth - 1
        lw = {
            'norm_ffn1': norm_ffn1[layer], 'w_ffn1_in': w_ffn1_in[layer], 'w_ffn1_out': w_ffn1_out[layer],
            'norm_mix': norm_mix[layer], 'w_mix': _regroup_w_in(w_in[layer]), 'conv_w': conv_w[layer],
            'conv_b': conv_b[layer], 'dt_bias': dt_bias[layer], 'a_log': a_log[layer], 'd_skip': d_skip[layer],
            'norm_ssd_out': norm_ssd_out[layer], 'norm_nsa_out': norm_nsa_out[layer], 'w_out': w_out[layer],
            'norm_mem_q': norm_mem_q[layer], 'w_mem_q': w_mem_q[layer], 'w_mem_out': w_mem_out[layer],
        }
        cw = _compress_weights(cmp_pos_k[layer], cmp_w1_k[layer], cmp_w2_k[layer],
                               cmp_pos_v[layer], cmp_w1_v[layer], cmp_w2_v[layer])
        g_final = norm_final if last else None
        mem_kv_p = _norm_matmul(mem_prompt.reshape(-1, d), norm_mem_kv[layer], w_mem_kv[layer])
        mem_kv_p = mem_kv_p.reshape(bp, -1, 2 * MEM_INNER)
        conv0 = jnp.zeros((bp, SSD_CONV - 1, SSD_CONV_DIM), F32)
        h0 = jnp.zeros((bp, SSD_HEADS, HEAD_DIM, SSD_STATE), F32)
        xp, c_p, s_p, w_p, h_p, conv_p = _trunk_layer(xp, lw, cw, rel_bias, mem_kv_p, conv0, h0, bp, lp)
        xp = _ffn(xp, norm_ffn2[layer], w_ffn2_in[layer], w_ffn2_out[layer], g_final)
        sample = dict(pool_c=cache_kv_cmp[layer].reshape(n_pool, page_size, kvw),
                      pool_s=cache_kv_slc[layer].reshape(n_pool, page_size, kvw),
                      win_buf=state_win_kv[layer].reshape(bs, -1, kvw), page_table=page_table)
        mem_kv_s = cache_mem_kv[layer].reshape(bs, -1, 2 * MEM_INNER)
        xs, c_s, s_s, w_s, h_s, conv_s = _trunk_layer(xs, lw, cw, rel_bias, mem_kv_s, state_conv[layer],
                                                      state_ssd[layer], bs, lsq, sample=sample)
        xs = _ffn(xs, norm_ffn2[layer], w_ffn2_in[layer], w_ffn2_out[layer], g_final)
        kv6 = lambda a, n, l: a.reshape(n, l, 2, NSA_KV_HEADS, HEAD_DIM)
        for lst, v in zip(outs_p, (kv6(c_p, bp, lp), kv6(s_p, bp, lp), kv6(w_p, bp, -1), h_p, conv_p,
                                   mem_kv_p.reshape(bp, -1, 2, MEM_HEADS, HEAD_DIM))):
            lst.append(v)
        for lst, v in zip(outs_s, (kv6(c_s, bs, lsq), kv6(s_s, bs, lsq), kv6(w_s, bs, -1), h_s, conv_s)):
            lst.append(v)
    return (xp.reshape(bp, lp, d), xs.reshape(bs, lsq, d),
            *(jnp.stack(o) for o in outs_p), *(jnp.stack(o) for o in outs_s))
```

```python
import functools
import math

import numpy as np
import jax
import jax.numpy as jnp
from jax import lax
from jax.experimental import pallas as pl
from jax.experimental.pallas import tpu as pltpu

F32 = jnp.float32
MXU = jnp.bfloat16

HEAD_DIM = 64
SSD_HEADS = 8
SSD_GROUPS = 2
SSD_STATE = 128
SSD_CONV = 4
SSD_CHUNK = 128
NSA_HEADS = 8
NSA_KV_HEADS = 2
NSA_GROUP = NSA_HEADS // NSA_KV_HEADS
CMP_BLOCK = 32
CMP_STRIDE = 16
CMP_HIDDEN = 256
SLC_BLOCK = 64
SLC_TOPN = 16
WINDOW = 512
REL_BUCKETS = 32
REL_MAX_DIST = 128
MEM_HEADS = 4
Q_BLOCK = 128
EPS = 1e-6
NEG = -1e30

SSD_INNER = SSD_HEADS * HEAD_DIM
NSA_INNER = NSA_HEADS * HEAD_DIM
KV_WIDTH = NSA_KV_HEADS * HEAD_DIM
SSD_CONV_DIM = SSD_INNER + 2 * SSD_GROUPS * SSD_STATE
MEM_INNER = MEM_HEADS * HEAD_DIM
LANES = 128
VMEM_LIMIT = 56 * 1024 * 1024


def _cparams(*sem):
    return pltpu.CompilerParams(dimension_semantics=sem, vmem_limit_bytes=VMEM_LIMIT)


def _const_spec(shape):
    nd = len(shape)
    return pl.BlockSpec(shape, lambda *a: (0,) * nd, pipeline_mode=pl.Buffered(1))


def _mm(a, b):
    return jnp.dot(a.astype(MXU), b.astype(MXU), preferred_element_type=F32)


def _mm_nt(a, b):
    return lax.dot_general(a.astype(MXU), b.astype(MXU), (((1,), (1,)), ((), ())), preferred_element_type=F32)


def _rms(x, g):
    return x * lax.rsqrt(jnp.mean(x * x, axis=-1, keepdims=True) + EPS) * g


def _silu(x):
    return x * jax.nn.sigmoid(x)


FFN_CHUNK = 256


def _ffn_kernel(x_ref, g_ref, win_ref, wout_ref, *rest, nchunks, final_norm):
    if final_norm:
        gf_ref, o_ref, h_ref = rest
    else:
        o_ref, h_ref = rest
    x = x_ref[...]
    h_ref[...] = _rms(x, g_ref[...]).astype(h_ref.dtype)
    f = nchunks * FFN_CHUNK
    for j in range(nchunks):
        cols = slice(j * FFN_CHUNK, (j + 1) * FFN_CHUNK)
        h = h_ref[...]
        gate = jnp.dot(h, win_ref[:, cols], preferred_element_type=F32)
        up = jnp.dot(h, win_ref[:, f + j * FFN_CHUNK:f + (j + 1) * FFN_CHUNK], preferred_element_type=F32)
        a = (_silu(gate) * up).astype(h.dtype)
        t = jnp.dot(a, wout_ref[cols, :], preferred_element_type=F32)
        if j == 0:
            o_ref[...] = t
        else:
            o_ref[...] += t
    y = x_ref[...] + 0.5 * o_ref[...]
    if final_norm:
        y = _rms(y, gf_ref[...])
    o_ref[...] = y


def _ffn(x, g, w_in, w_out, g_final=None, tm=512):
    t, d = x.shape
    f = w_out.shape[0]
    nchunks = f // FFN_CHUNK
    tm = min(tm, t)
    assert nchunks * FFN_CHUNK == f and t % tm == 0
    win = w_in.astype(MXU)
    wout = w_out.astype(MXU)
    final_norm = g_final is not None
    args = [x, g.reshape(1, d), win, wout]
    specs = [pl.BlockSpec((tm, d), lambda i: (i, 0)), _const_spec((1, d)),
             _const_spec(win.shape), _const_spec(wout.shape)]
    if final_norm:
        args.append(g_final.reshape(1, d))
        specs.append(_const_spec((1, d)))
    return pl.pallas_call(
        functools.partial(_ffn_kernel, nchunks=nchunks, final_norm=final_norm),
        grid=(t // tm,),
        in_specs=specs,
        out_specs=pl.BlockSpec((tm, d), lambda i: (i, 0)),
        out_shape=jax.ShapeDtypeStruct((t, d), F32),
        scratch_shapes=[pltpu.VMEM((tm, d), MXU)],
        compiler_params=_cparams("parallel"),
    )(*args)


PROJ_COLS = 3200
_O_Z, _O_XBC, _O_Q, _O_KVC, _O_KVS, _O_KVW, _O_G, _O_DT = 0, 512, 1536, 2048, 2304, 2560, 2816, 3072


def _regroup_w_in(w_in):
    o = np.cumsum([0, SSD_INNER, SSD_CONV_DIM, SSD_HEADS, NSA_INNER] + [KV_WIDTH] * 6 + [3 * NSA_HEADS])
    z, xbc, dt, q, kvs, gl = (w_in[:, o[0]:o[1]], w_in[:, o[1]:o[2]], w_in[:, o[2]:o[3]], w_in[:, o[3]:o[4]],
                              w_in[:, o[4]:o[10]], w_in[:, o[10]:o[11]])
    pad = lambda w: jnp.pad(w, ((0, 0), (0, LANES - w.shape[1])))
    ng = 3 * NSA_GROUP
    gls = [pad(gl[:, g * ng:(g + 1) * ng]) for g in range(NSA_KV_HEADS)]
    return jnp.concatenate([z, xbc, q, kvs] + gls + [pad(dt)], axis=1).astype(MXU)


def _proj_kernel(x_ref, g_ref, w_ref, z_ref, xbc_ref, q_ref, kvc_ref, kvs_ref, kvw_ref, gate_ref, dt_ref, *aug,
                 tiles_per_seq):
    h = _rms(x_ref[...], g_ref[...]).astype(w_ref.dtype)
    res = jnp.dot(h, w_ref[...], preferred_element_type=F32)
    tm = res.shape[0]
    z_ref[...] = res[:, _O_Z:_O_XBC]
    xbc_ref[...] = res[:, _O_XBC:_O_Q]
    q_ref[...] = res[:, _O_Q:_O_KVC] * (HEAD_DIM ** -0.5)
    kvc_ref[...] = res[:, _O_KVC:_O_KVS]
    kvs = res[:, _O_KVS:_O_KVW]
    kvw = res[:, _O_KVW:_O_G]
    kvs_ref[...] = kvs
    kvw_ref[...] = kvw
    gate_ref[...] = jax.nn.sigmoid(res[:, _O_G:_O_DT])
    dt_ref[...] = res[:, _O_DT:PROJ_COLS]
    if aug:
        ksa_ref, vsa_ref, kw_ref, vwa_ref = aug
        lane = lax.broadcasted_iota(jnp.int32, (tm, LANES), 1)
        row = lax.broadcasted_iota(jnp.int32, (tm, LANES), 0)
        pos = (pl.program_id(0) % tiles_per_seq) * tm + row
        onehot = jnp.where(lane == pos // SLC_BLOCK, 1.0, 0.0)
        ones_col = jnp.where(lane == 0, 1.0, 0.0)
        dt_ = ksa_ref.dtype
        ksa_ref[:, 0:LANES] = kvs[:, 0:LANES].astype(dt_)
        ksa_ref[:, LANES:] = onehot.astype(dt_)
        vsa_ref[:, 0:LANES] = kvs[:, LANES:].astype(dt_)
        vsa_ref[:, LANES:] = ones_col.astype(dt_)
        kw_ref[...] = kvw[:, 0:LANES].astype(dt_)
        vwa_ref[:, 0:LANES] = kvw[:, LANES:].astype(dt_)
        vwa_ref[:, LANES:] = ones_col.astype(dt_)


def _proj(x, g, w, seq_len=None, tm=512):
    t, d = x.shape
    tm = min(tm, t)
    assert t % tm == 0
    widths = [512, 1024, 512, 256, 256, 256, NSA_KV_HEADS * LANES, LANES]
    shapes = [jax.ShapeDtypeStruct((t, wd), F32) for wd in widths]
    aug = seq_len is not None
    if aug:
        assert seq_len % tm == 0 and seq_len // SLC_BLOCK <= LANES
        shapes += [jax.ShapeDtypeStruct((t, wd), MXU) for wd in (256, 256, LANES, 256)]
    return pl.pallas_call(
        functools.partial(_proj_kernel, tiles_per_seq=(seq_len // tm if aug else 1)),
        grid=(t // tm,),
        in_specs=[pl.BlockSpec((tm, d), lambda i: (i, 0)), _const_spec((1, d)), _const_spec(w.shape)],
        out_specs=[pl.BlockSpec((tm, s.shape[1]), lambda i: (i, 0)) for s in shapes],
        out_shape=shapes,
        compiler_params=_cparams("parallel"),
    )(x, g.reshape(1, d), w)


HALO = 8


def _split3(x):
    hi = x.astype(jnp.bfloat16).astype(F32)
    r = x - hi
    mid = r.astype(jnp.bfloat16).astype(F32)
    return hi, mid, r - mid


def _ssd_kernel(z_ref, xbc_ref, dt_ref, conv0_ref, h0_ref, cw_ref, cb_ref, dtb_ref, alog_ref, dskip_ref, g_ref,
                y_ref, hout_ref, convout_ref, xwin_ref, h_ref, *, lv):
    q = SSD_CHUNK
    c = pl.program_id(1)

    @pl.when(c == 0)
    def _():
        xwin_ref[0:HALO, :] = conv0_ref[0]
        h_ref[...] = h0_ref[0]

    def padrows(v):
        if lv == q:
            return v
        return jnp.concatenate([v, jnp.zeros((q - lv, v.shape[1]), v.dtype)], axis=0)

    xwin_ref[HALO:HALO + q, :] = padrows(xbc_ref[...])
    acc = cb_ref[...] + cw_ref[SSD_CONV - 1:SSD_CONV, :] * xwin_ref[HALO:HALO + q, :]
    for k in range(SSD_CONV - 1):
        acc = acc + cw_ref[k:k + 1, :] * xwin_ref[pl.ds(HALO - (SSD_CONV - 1) + k, q), :]
    xc = _silu(acc)
    tail = xwin_ref[lv:lv + HALO, :]
    xwin_ref[0:HALO, :] = tail
    convout_ref[0] = tail

    xs = xc[:, 0:SSD_INNER]
    rowi = lax.broadcasted_iota(jnp.int32, (q, q), 0)
    coli = lax.broadcasted_iota(jnp.int32, (q, q), 1)
    causal = rowi >= coli
    tri = jnp.where(causal, 1.0, 0.0).astype(jnp.bfloat16)

    xdt = padrows(dt_ref[...]) + dtb_ref[...]
    dt = jnp.maximum(xdt, 0.0) + jnp.log1p(jnp.exp(-jnp.abs(xdt)))
    if lv != q:
        dt = jnp.where(rowi < lv, dt, 0.0)
    da = dt * (-jnp.exp(alog_ref[...]))
    acs = sum(jnp.dot(tri, p.astype(jnp.bfloat16), preferred_element_type=F32) for p in _split3(da))
    acs_t = acs.T
    dt_t = dt.T
    last = acs[q - 1:q, :]
    last_t = acs_t[:, q - 1:q]
    xs_t = [xs[:, p * LANES:(p + 1) * LANES].T for p in range(SSD_HEADS // 2)]
    lane_hi = lax.broadcasted_iota(jnp.int32, (q, LANES), 1) >= HEAD_DIM
    row_hi = lax.broadcasted_iota(jnp.int32, (LANES, q), 0) >= HEAD_DIM
    hpg = SSD_HEADS // SSD_GROUPS
    ys = []
    for g in range(SSD_GROUPS):
        bm = xc[:, SSD_INNER + g * SSD_STATE:SSD_INNER + (g + 1) * SSD_STATE]
        cm = xc[:, SSD_INNER + (SSD_GROUPS + g) * SSD_STATE:SSD_INNER + (SSD_GROUPS + g + 1) * SSD_STATE]
        cb = _mm_nt(cm, bm)
        for pp in range(hpg // 2):
            p = g * (hpg // 2) + pp
            xpair = xs[:, p * LANES:(p + 1) * LANES]
            yi = []
            for hh in range(2):
                h = 2 * p + hh
                seg = acs[:, h:h + 1] - acs_t[h:h + 1, :]
                decay = jnp.where(causal, jnp.exp(jnp.where(causal, seg, 0.0)), 0.0)
                yi.append(_mm(cb * decay * dt_t[h:h + 1, :], xpair))
            y_intra = jnp.where(lane_hi, yi[1], yi[0])
            h0, h1 = 2 * p, 2 * p + 1
            grow = jnp.where(lane_hi, jnp.exp(acs[:, h1:h1 + 1]), jnp.exp(acs[:, h0:h0 + 1]))
            y_inter = _mm_nt(cm, h_ref[p]) * grow
            ys.append(y_intra + y_inter)
            te = jnp.where(row_hi,
                           jnp.exp(last_t[h1:h1 + 1, :] - acs_t[h1:h1 + 1, :]) * dt_t[h1:h1 + 1, :],
                           jnp.exp(last_t[h0:h0 + 1, :] - acs_t[h0:h0 + 1, :]) * dt_t[h0:h0 + 1, :])
            states = _mm(xs_t[p] * te, bm)
            rdec = jnp.where(row_hi[:, 0:1], jnp.exp(last_t[h1:h1 + 1, :]), jnp.exp(last_t[h0:h0 + 1, :]))
            h_ref[p] = h_ref[p] * rdec + states
    y = jnp.concatenate(ys, axis=1) + dskip_ref[...] * xs
    y = _rms(y * _silu(padrows(z_ref[...])), g_ref[...])
    y_ref[...] = y[0:lv, :]
    hout_ref[0] = h_ref[...]


def _ssd(z, xbc, dt, conv0, h0, conv_w, conv_b, dt_bias, a_log, d_skip, norm_g, nseq, seq_len):
    q = SSD_CHUNK
    lv = min(q, seq_len)
    nch = seq_len // lv
    assert nch * lv == seq_len
    t = nseq * seq_len
    cdim = SSD_CONV_DIM
    keep = SSD_CONV - 1
    conv0p = jnp.pad(conv0, ((0, 0), (HALO - keep, 0), (0, 0)))
    padl = lambda v: jnp.pad(v.reshape(1, -1), ((0, 0), (0, LANES - v.size)))
    hp = h0.reshape(nseq, SSD_HEADS // 2, 2 * HEAD_DIM, SSD_STATE)
    row = lambda b, c: (b * nch + c, 0)
    y, hout, convout = pl.pallas_call(
        functools.partial(_ssd_kernel, lv=lv),
        grid=(nseq, nch),
        in_specs=[pl.BlockSpec((lv, SSD_INNER), row), pl.BlockSpec((lv, cdim), row), pl.BlockSpec((lv, LANES), row),
                  pl.BlockSpec((1, HALO, cdim), lambda b, c: (b, 0, 0)),
                  pl.BlockSpec((1,) + hp.shape[1:], lambda b, c: (b, 0, 0, 0)),
                  _const_spec((SSD_CONV, cdim)), _const_spec((1, cdim)), _const_spec((1, LANES)),
                  _const_spec((1, LANES)), _const_spec((1, SSD_INNER)), _const_spec((1, SSD_INNER))],
        out_specs=[pl.BlockSpec((lv, SSD_INNER), row),
                   pl.BlockSpec((1,) + hp.shape[1:], lambda b, c: (b, 0, 0, 0)),
                   pl.BlockSpec((1, HALO, cdim), lambda b, c: (b, 0, 0))],
        out_shape=[jax.ShapeDtypeStruct((t, SSD_INNER), F32), jax.ShapeDtypeStruct(hp.shape, F32),
                   jax.ShapeDtypeStruct((nseq, HALO, cdim), F32)],
        scratch_shapes=[pltpu.VMEM((HALO + q, cdim), F32), pltpu.VMEM(hp.shape[1:], F32)],
        compiler_params=_cparams("parallel", "arbitrary"),
    )(z, xbc, dt, conv0p, hp, conv_w, conv_b.reshape(1, cdim), padl(dt_bias), padl(a_log),
      jnp.repeat(d_skip, HEAD_DIM).reshape(1, SSD_INNER), norm_g.reshape(1, SSD_INNER))
    return y, hout.reshape(nseq, SSD_HEADS, HEAD_DIM, SSD_STATE), convout[:, HALO - keep:, :]


def _compress_weights(pe_k, w1_k, w2_k, pe_v, w1_v, w2_v):
    def w1_pair(w1):
        w = w1.reshape(CMP_BLOCK, HEAD_DIM, CMP_HIDDEN)
        zz = jnp.zeros_like(w)
        return jnp.concatenate([jnp.concatenate([w, zz], axis=2), jnp.concatenate([zz, w], axis=2)], axis=1).astype(MXU)

    def w2_pair(w2):
        zz = jnp.zeros_like(w2)
        return jnp.concatenate([jnp.concatenate([w2, zz], axis=1), jnp.concatenate([zz, w2], axis=1)], axis=0).astype(MXU)

    pe = jnp.concatenate([pe_k, pe_k, pe_v, pe_v], axis=1)
    return pe, w1_pair(w1_k), w1_pair(w1_v), w2_pair(w2_k), w2_pair(w2_v)


def _compress_compute(row_of_chunks, pe_ref, wk_ref, wv_ref, w2k_ref, w2v_ref, sh_ref, kc_ref, vc_ref, nc):
    half = CMP_STRIDE
    a0k = a1k = a0v = a1v = None
    for s in range(half):
        xk, xv = row_of_chunks(s)
        pe0, pe1 = pe_ref[s:s + 1, :], pe_ref[half + s:half + s + 1, :]
        t0k = jnp.dot((xk + pe0[:, 0:LANES]).astype(MXU), wk_ref[s], preferred_element_type=F32)
        t0v = jnp.dot((xv + pe0[:, LANES:]).astype(MXU), wv_ref[s], preferred_element_type=F32)
        t1k = jnp.dot((xk + pe1[:, 0:LANES]).astype(MXU), wk_ref[half + s], preferred_element_type=F32)
        t1v = jnp.dot((xv + pe1[:, LANES:]).astype(MXU), wv_ref[half + s], preferred_element_type=F32)
        a0k, a1k = (t0k, t1k) if s == 0 else (a0k + t0k, a1k + t1k)
        a0v, a1v = (t0v, t1v) if s == 0 else (a0v + t0v, a1v + t1v)
    sh_ref[nc:nc + 8, :] = jnp.zeros((8, sh_ref.shape[1]), F32)
    for a0, a1, w2_ref, o_ref in ((a0k, a1k, w2k_ref, kc_ref), (a0v, a1v, w2v_ref, vc_ref)):
        sh_ref[0:nc, :] = a1
        hid = a0 + sh_ref[pl.ds(1, nc), :]
        o_ref[...] = jnp.dot(_silu(hid).astype(MXU), w2_ref[...], preferred_element_type=F32)


def _compress_prompt_kernel(rows_ref, pe_ref, wk_ref, wv_ref, w2k_ref, w2v_ref, kc_ref, vc_ref, sh_ref, *, nc):
    rw = 2 * KV_WIDTH

    def row_of_chunks(s):
        return rows_ref[:, s * rw:s * rw + LANES], rows_ref[:, s * rw + LANES:(s + 1) * rw]

    _compress_compute(row_of_chunks, pe_ref, wk_ref, wv_ref, w2k_ref, w2v_ref, sh_ref, kc_ref.at[0], vc_ref.at[0], nc)


def _compress_specs(cw):
    return [_const_spec(w.shape) for w in cw]


def _compress_prompt(kv_cmp, cw, nseq, seq_len):
    nc = seq_len // CMP_STRIDE
    out = jax.ShapeDtypeStruct((nseq, nc, LANES), F32)
    return pl.pallas_call(
        functools.partial(_compress_prompt_kernel, nc=nc),
        grid=(nseq,),
        in_specs=[pl.BlockSpec((nc, CMP_STRIDE * 2 * KV_WIDTH), lambda b: (b, 0))] + _compress_specs(cw),
        out_specs=[pl.BlockSpec((1, nc, LANES), lambda b: (b, 0, 0))] * 2,
        out_shape=[out, out],
        scratch_shapes=[pltpu.VMEM((nc + 8, NSA_KV_HEADS * CMP_HIDDEN), F32)],
        compiler_params=_cparams("parallel"),
    )(kv_cmp.reshape(nseq * nc, CMP_STRIDE * 2 * KV_WIDTH), *cw)


def _gather_step(page_copies, npages):
    b = pl.program_id(0)
    nb = pl.num_programs(0)
    slot = b % 2

    def start_all(seq, sl):
        def body(p, c):
            for cp in page_copies(seq, sl, p):
                cp.start()
            return c
        lax.fori_loop(0, npages, body, 0)

    @pl.when(b == 0)
    def _():
        start_all(0, 0)

    @pl.when(b + 1 < nb)
    def _():
        start_all(b + 1, 1 - slot)

    def wbody(p, c):
        for cp in page_copies(b, slot, p):
            cp.wait()
        return c
    lax.fori_loop(0, npages, wbody, 0)
    return slot


def _paged_view(cache):
    n_pool, page = cache.shape[:2]
    return jnp.transpose(cache, (0, 2, 3, 4, 1)).reshape(n_pool, 2, KV_WIDTH, page)


def _compress_sample_kernel(pt_ref, pool_ref, pe_ref, wk_ref, wv_ref, w2k_ref, w2v_ref, kc_ref, vc_ref,
                            buf_ref, sem_ref, rk_ref, rv_ref, sh_ref, *, nc, npages, page_size):
    def page_copies(seq, slot, p):
        return [pltpu.make_async_copy(pool_ref.at[pt_ref[seq * npages + p]], buf_ref.at[slot, p], sem_ref.at[slot])]

    slot = _gather_step(page_copies, npages)

    def to_rows(p, c):
        rows = pl.ds(pl.multiple_of(p * page_size, page_size), page_size)
        rk_ref[rows, :] = buf_ref[slot, p, 0].T
        rv_ref[rows, :] = buf_ref[slot, p, 1].T
        return c

    lax.fori_loop(0, npages, to_rows, 0)

    def row_of_chunks(s):
        return rk_ref[pl.ds(s, nc, stride=CMP_STRIDE), :], rv_ref[pl.ds(s, nc, stride=CMP_STRIDE), :]

    _compress_compute(row_of_chunks, pe_ref, wk_ref, wv_ref, w2k_ref, w2v_ref, sh_ref, kc_ref.at[0], vc_ref.at[0], nc)


def _compress_sample(pool_t, page_table, cw):
    nseq, npages = page_table.shape
    page_size = pool_t.shape[3]
    past = npages * page_size
    nc = past // CMP_STRIDE
    assert page_size == LANES and KV_WIDTH == LANES
    out = jax.ShapeDtypeStruct((nseq, nc, LANES), F32)
    cspecs = [pl.BlockSpec(w.shape, lambda b, pt, nd=w.ndim: (0,) * nd, pipeline_mode=pl.Buffered(1)) for w in cw]
    return pl.pallas_call(
        functools.partial(_compress_sample_kernel, nc=nc, npages=npages, page_size=page_size),
        grid_spec=pltpu.PrefetchScalarGridSpec(
            num_scalar_prefetch=1,
            grid=(nseq,),
            in_specs=[pl.BlockSpec(memory_space=pl.ANY)] + cspecs,
            out_specs=[pl.BlockSpec((1, nc, LANES), lambda b, pt: (b, 0, 0))] * 2,
            scratch_shapes=[pltpu.VMEM((2, npages, 2, KV_WIDTH, page_size), F32), pltpu.SemaphoreType.DMA((2,)),
                            pltpu.VMEM((past, KV_WIDTH), F32), pltpu.VMEM((past, KV_WIDTH), F32),
                            pltpu.VMEM((nc + 8, NSA_KV_HEADS * CMP_HIDDEN), F32)],
        ),
        out_shape=[out, out],
        compiler_params=_cparams("arbitrary"),
    )(page_table.reshape(-1), pool_t, *cw)


def _bucket_thresholds():
    n = np.arange(0, 2 * REL_MAX_DIST)
    exact = REL_BUCKETS // 2
    out = []
    for ft in (np.float32, np.float64):
        nf = np.maximum(n, exact).astype(ft)
        large = exact + (np.log(nf / ft(exact)) / ft(math.log(REL_MAX_DIST / exact)) * ft(REL_BUCKETS - exact)).astype(np.int64)
        out.append(np.where(n < exact, n, np.minimum(large, REL_BUCKETS - 1)))
    assert (out[0] == out[1]).all() and out[1][REL_MAX_DIST] == REL_BUCKETS - 1
    return [int(np.argmax(out[1] >= b)) for b in range(1, REL_BUCKETS)]


_THR = _bucket_thresholds()


def _bias_tables(dist, rb_ref, heads):
    steps = [jnp.where(dist >= _THR[b - 1], 1.0, 0.0) for b in range(1, REL_BUCKETS)]
    far = dist >= REL_MAX_DIST
    out = []
    for h in heads:
        acc = jnp.full(dist.shape, rb_ref[0, h] - rb_ref[REL_BUCKETS - 1, h], F32)
        for b in range(1, REL_BUCKETS):
            acc = acc + steps[b - 1] * (rb_ref[b, h] - rb_ref[b - 1, h])
        out.append(jnp.where(dist < 0, NEG, jnp.where(far, 0.0, acc)))
    return out


BIG = 1e30
KT = 256
SLC_UNROLL = 4
N_BAND = 20
BAND_LO = 12
F_STEP = 2 * N_BAND
N_WK = 6
N_SK = 4
WIN_TILES = WINDOW // Q_BLOCK + 1


def _nsa_tables(rb_ref, st_ref, wt_ref, qft_ref, ov_ref, ncp):
    tq = Q_BLOCK
    heads = range(NSA_HEADS)
    t2 = lax.broadcasted_iota(jnp.int32, (tq, KT), 0)
    c2 = lax.broadcasted_iota(jnp.int32, (tq, KT), 1)
    for dd in range(N_SK - 1):
        tabs = _bias_tables(tq * dd + t2 - c2, rb_ref, heads)
        for h in heads:
            st_ref[h * N_SK + dd] = tabs[h]
    t1 = lax.broadcasted_iota(jnp.int32, (tq, LANES), 0)
    c1 = lax.broadcasted_iota(jnp.int32, (tq, LANES), 1)
    zeros = jnp.zeros((tq, LANES), F32)
    for d in range(2):
        tabs = _bias_tables(tq * d + t1 - c1, rb_ref, heads)
        for h in heads:
            wt_ref[h * N_WK + d] = tabs[h]
    for h in heads:
        st_ref[h * N_SK + N_SK - 1] = jnp.zeros((tq, KT), F32)
        wt_ref[h * N_WK + 2] = zeros
        wt_ref[h * N_WK + 3] = zeros
        wt_ref[h * N_WK + 4] = jnp.where(c1 > t1, 0.0, NEG)
        wt_ref[h * N_WK + 5] = zeros + NEG
    for g in range(NSA_KV_HEADS):
        f = c1 - (HEAD_DIM if g == 0 else 0)
        is_lo = f >= N_BAND
        u = jnp.where(is_lo, f - N_BAND, f) - BAND_LO
        live = (f >= 0) & (f < 2 * N_BAND)
        tabs = _bias_tables(t1 - CMP_STRIDE * u - (CMP_BLOCK - 1), rb_ref, range(g * NSA_GROUP, (g + 1) * NSA_GROUP))
        for r in range(NSA_GROUP):
            hi = tabs[r].astype(jnp.bfloat16).astype(F32)
            lo = jnp.where(tabs[r] <= NEG, 0.0, tabs[r] - hi)
            val = jnp.where(live, jnp.where(is_lo, lo, hi), 0.0)
            qft_ref[g * NSA_GROUP + r] = jnp.where(f == F_STEP, NEG, val)
    kk = lax.broadcasted_iota(jnp.int32, (ncp, LANES), 0)
    ss = lax.broadcasted_iota(jnp.int32, (ncp, LANES), 1)
    ratio = SLC_BLOCK // CMP_STRIDE
    nover = (CMP_BLOCK - 1) // CMP_STRIDE
    ov = (kk >= ratio * ss - nover) & (kk < ratio * ss + ratio) & (kk < ncp - 1)
    ov_ref[...] = jnp.where(ov, 1.0, 0.0).astype(ov_ref.dtype)


def _lane_half(rows):
    return lax.broadcasted_iota(jnp.int32, (rows, LANES), 1) // HEAD_DIM


def _place_heads(qg, g):
    keep = _lane_half(qg.shape[0]) == g
    out = []
    for r in range(NSA_GROUP):
        v = qg[:, (r // 2) * LANES:(r // 2 + 1) * LANES]
        v = jnp.where(g == r % 2, v, pltpu.roll(v, HEAD_DIM, 1))
        out.append(jnp.where(keep, v, 0.0))
    return out


def _unplace_heads(ohs, g):
    keep = _lane_half(ohs[0].shape[0]) == g
    cols = []
    for c in range(NSA_GROUP // 2):
        pair = []
        for r in (2 * c, 2 * c + 1):
            oh = jnp.where(keep, ohs[r], 0.0)
            pair.append(jnp.where(g == r % 2, oh, pltpu.roll(oh, HEAD_DIM, 1)))
        cols.append(pair[0] + pair[1])
    return jnp.concatenate(cols, axis=1)


def _gate_mix(gates, r, o_cmp, o_slc, o_win):
    return (gates[:, 3 * r:3 * r + 1] * o_cmp + gates[:, 3 * r + 1:3 * r + 2] * o_slc
            + gates[:, 3 * r + 2:3 * r + 3] * o_win)


def _top_blocks(imp, qpos_lane):
    vals = imp.T
    srow = lax.broadcasted_iota(jnp.int32, vals.shape, 0)
    qblk = qpos_lane // SLC_BLOCK
    forced = (srow == 0) | (srow == qblk) | (srow == qblk - 1)
    vals = jnp.where(forced, 3e38, vals)
    vals = jnp.where(srow <= qblk, vals, -1.0)
    srow_f = srow.astype(F32)
    sel = jnp.zeros(vals.shape, F32)
    for _ in range(SLC_TOPN):
        mx = jnp.max(vals, axis=0, keepdims=True)
        idx = jnp.min(jnp.where(vals == mx, srow_f, float(LANES)), axis=0, keepdims=True)
        pick = srow_f == idx
        sel = jnp.where(pick, 1.0, sel)
        vals = jnp.where(pick, -2.0, vals)
    return jnp.where(srow <= qblk, sel, 0.0).T


def _nsa_prompt_kernel(rb_ref, q_ref, gate_ref, kc_ref, vc_ref, ksa_ref, vsa_ref, kw_ref, vwa_ref, o_ref,
                       st_ref, wt_ref, qft_ref, ov_ref, lbuf_ref, m_ref, acc_ref, *, ncp):
    tq = Q_BLOCK
    i = pl.program_id(1)
    g = pl.program_id(2)

    @pl.when((pl.program_id(0) == 0) & (i == 0) & (g == 0))
    def _():
        _nsa_tables(rb_ref, st_ref, wt_ref, qft_ref, ov_ref, ncp)

    gates = gate_ref[...]
    qpos_lane = i * tq + lax.broadcasted_iota(jnp.int32, (tq, LANES), 1)
    trow = lax.broadcasted_iota(jnp.int32, (NSA_GROUP * tq, 1), 0) % tq
    row_live = (i * tq + trow) >= CMP_BLOCK - 1
    kidx = lax.broadcasted_iota(jnp.int32, (ncp, LANES), 0)
    klane = lax.broadcasted_iota(jnp.int32, (ncp, LANES), 1)
    ngrp = i // (2 * SLC_UNROLL) + 1
    nplain = jnp.maximum(ngrp - 2, 0)
    j0 = jnp.maximum(i - (WIN_TILES - 1), 0)
    qp = _place_heads(q_ref[...], g)
    heads = [g * NSA_GROUP + r for r in range(NSA_GROUP)]

    ws = pl.multiple_of(j0 * tq, tq)
    lw = _mm_nt(jnp.concatenate(qp, axis=0).astype(MXU), kw_ref[pl.ds(ws, WIN_TILES * tq), :])
    kinds = []
    for jj in range(WIN_TILES):
        d = i - j0 - jj
        kinds.append(jnp.where(d < 0, N_WK - 1, d))
    lw = lw + jnp.concatenate(
        [jnp.concatenate([wt_ref[h * N_WK + kinds[jj]] for jj in range(WIN_TILES)], axis=1) for h in heads], axis=0)
    ew = jnp.exp(lw - jnp.max(lw, axis=1, keepdims=True)).astype(MXU)
    accw = jnp.dot(ew, vwa_ref[pl.ds(ws, WIN_TILES * tq), :], preferred_element_type=F32)
    o_win = accw[:, 0:LANES] * (1.0 / accw[:, LANES:LANES + 1])

    f = klane - HEAD_DIM * (1 - g)
    u12 = kidx - (tq // CMP_STRIDE) * i + BAND_LO
    fa = jnp.where(f >= N_BAND, f - N_BAND, f)
    feat = ((fa == u12) & (f >= 0) & (f < 2 * N_BAND)) | ((f == F_STEP) & (u12 >= N_BAND - 1))
    ka = jnp.where(klane // HEAD_DIM == g, kc_ref[0], jnp.where(feat, 1.0, 0.0)).astype(MXU)
    qa = jnp.concatenate([qp[r] + qft_ref[heads[r]] for r in range(NSA_GROUP)], axis=0).astype(MXU)
    lc = _mm_nt(qa, ka)
    mc = jnp.max(lc, axis=1, keepdims=True)
    ec = jnp.exp(lc - mc)
    sc = jnp.sum(ec, axis=1, keepdims=True)
    pc = (ec * jnp.where(row_live, 1.0 / sc, 0.0)).astype(MXU)
    o_cmp = jnp.dot(pc, vc_ref[0].astype(MXU), preferred_element_type=F32)
    imp = sum(jnp.dot(pc[r * tq:(r + 1) * tq], ov_ref[...], preferred_element_type=F32) for r in range(NSA_GROUP))

    selneg = (_top_blocks(imp, qpos_lane) - 1.0) * BIG

    qs = jnp.concatenate([jnp.concatenate([qp[r], selneg], axis=1) for r in range(NSA_GROUP)], axis=0).astype(MXU)
    m_ref[...] = jnp.full(m_ref.shape, NEG, F32)
    gk = SLC_UNROLL * KT

    def stage(grp, lt):
        lbuf_ref[grp] = lt
        mg = lt[:, 0:LANES]
        for c in range(1, gk // LANES):
            mg = jnp.maximum(mg, lt[:, c * LANES:(c + 1) * LANES])
        m_ref[...] = jnp.maximum(m_ref[...], mg)

    def plain_group(grp, c):
        stage(grp, _mm_nt(qs, ksa_ref[pl.ds(pl.multiple_of(grp * gk, gk), gk), :]))
        return c

    def near_group(grp, c):
        tabs = []
        for h in heads:
            row = []
            for u in range(SLC_UNROLL):
                dd = i - 2 * (grp * SLC_UNROLL + u)
                row.append(st_ref[h * N_SK + jnp.where((dd < 0) | (dd >= N_SK), N_SK - 1, dd)])
            tabs.append(jnp.concatenate(row, axis=1))
        stage(grp, _mm_nt(qs, ksa_ref[pl.ds(pl.multiple_of(grp * gk, gk), gk), :]) + jnp.concatenate(tabs, axis=0))
        return c

    lax.fori_loop(0, nplain, plain_group, 0)
    lax.fori_loop(nplain, ngrp, near_group, 0)
    m_ref[...] = jnp.broadcast_to(jnp.max(m_ref[...], axis=1, keepdims=True), m_ref.shape)
    acc_ref[...] = jnp.zeros(acc_ref.shape, F32)

    def pv_group(grp, c):
        e = jnp.exp(lbuf_ref[grp] - jnp.concatenate([m_ref[...]] * (gk // LANES), axis=1)).astype(MXU)
        acc_ref[...] += jnp.dot(e, vsa_ref[pl.ds(pl.multiple_of(grp * gk, gk), gk), :], preferred_element_type=F32)
        return c

    lax.fori_loop(0, ngrp, pv_group, 0)
    acc = acc_ref[...]
    o_slc = acc[:, 0:LANES] * (1.0 / acc[:, LANES:LANES + 1])

    ohs = []
    for r in range(NSA_GROUP):
        rows = slice(r * tq, (r + 1) * tq)
        ohs.append(_gate_mix(gates, r, o_cmp[rows], o_slc[rows], o_win[rows]))
    o_ref[...] = _unplace_heads(ohs, g)


def _nsa_prompt(rel_bias, q, gates, kc, vc, ksa, vsa, kw, vwa, nseq, seq_len):
    tq = Q_BLOCK
    nq = seq_len // tq
    ncp = kc.shape[1]
    gw = NSA_GROUP * HEAD_DIM
    gk = SLC_UNROLL * KT
    assert seq_len % gk == 0 and seq_len >= WIN_TILES * tq and ncp % LANES == 0
    per_seq = lambda width: pl.BlockSpec((seq_len, width), lambda b, i, g: (b, 0), pipeline_mode=pl.Buffered(1))
    blk = lambda width: pl.BlockSpec((tq, width), lambda b, i, g: (b * nq + i, g))
    cmp_spec = pl.BlockSpec((1, ncp, LANES), lambda b, i, g: (b, 0, 0))
    return pl.pallas_call(
        functools.partial(_nsa_prompt_kernel, ncp=ncp),
        grid=(nseq, nq, NSA_KV_HEADS),
        in_specs=[pl.BlockSpec(memory_space=pltpu.SMEM), blk(gw), blk(LANES), cmp_spec, cmp_spec,
                  per_seq(2 * LANES), per_seq(2 * LANES), per_seq(LANES), per_seq(2 * LANES)],
        out_specs=blk(gw),
        out_shape=jax.ShapeDtypeStruct((nseq * seq_len, NSA_INNER), F32),
        scratch_shapes=[pltpu.VMEM((NSA_HEADS * N_SK, tq, KT), F32), pltpu.VMEM((NSA_HEADS * N_WK, tq, LANES), F32),
                        pltpu.VMEM((NSA_HEADS, tq, LANES), F32), pltpu.VMEM((ncp, LANES), MXU),
                        pltpu.VMEM((seq_len // gk, NSA_GROUP * tq, gk), F32),
                        pltpu.VMEM((NSA_GROUP * tq, LANES), F32), pltpu.VMEM((NSA_GROUP * tq, 2 * LANES), F32)],
        compiler_params=_cparams("arbitrary", "arbitrary", "arbitrary"),
    )(rel_bias, q, gates, kc, vc, ksa, vsa, kw, vwa)


TAIL = 2 * LANES


def _nsa_sample_tables(rb_ref, cbt_ref, tail_ref, wtab_ref, ov_ref, eall_ref, *, ls, past, ncp, wk, wkp, nk):
    heads = range(NSA_HEADS)

    def per_group(tabs):
        return [jnp.concatenate(tabs[g * NSA_GROUP:(g + 1) * NSA_GROUP], axis=0) for g in range(NSA_KV_HEADS)]

    t = lax.broadcasted_iota(jnp.int32, (ls, ncp), 0)
    k = lax.broadcasted_iota(jnp.int32, (ls, ncp), 1)
    dist = past + t - CMP_STRIDE * k - (CMP_BLOCK - 1)
    dist = jnp.where(k < ncp - 1, dist, -1)
    for g, tab in enumerate(per_group(_bias_tables(dist, rb_ref, heads))):
        cbt_ref[g] = tab
    t = lax.broadcasted_iota(jnp.int32, (ls, TAIL), 0)
    c = lax.broadcasted_iota(jnp.int32, (ls, TAIL), 1) + (nk - TAIL)
    dist = jnp.where(c < past + ls, past + t - c, -1)
    for g, tab in enumerate(per_group(_bias_tables(dist, rb_ref, heads))):
        tail_ref[g] = tab
    t = lax.broadcasted_iota(jnp.int32, (ls, wkp), 0)
    c = lax.broadcasted_iota(jnp.int32, (ls, wkp), 1)
    dist = past + t - (past - wk + c)
    dist = jnp.where((c < wk + ls) & (dist < WINDOW), dist, -1)
    for g, tab in enumerate(per_group(_bias_tables(dist, rb_ref, heads))):
        wtab_ref[g] = tab
    kk = lax.broadcasted_iota(jnp.int32, (ncp, 2 * LANES), 0)
    ss = lax.broadcasted_iota(jnp.int32, (ncp, 2 * LANES), 1)
    ratio = SLC_BLOCK // CMP_STRIDE
    nover = (CMP_BLOCK - 1) // CMP_STRIDE
    ov = (kk >= ratio * ss - nover) & (kk < ratio * ss + ratio) & (kk < ncp - 1)
    ov_ref[...] = jnp.where(ov, 1.0, 0.0).astype(ov_ref.dtype)
    s2 = lax.broadcasted_iota(jnp.int32, (2 * LANES, nk), 0)
    c2 = lax.broadcasted_iota(jnp.int32, (2 * LANES, nk), 1)
    eall_ref[...] = jnp.where(c2 // SLC_BLOCK == s2, 1.0, 0.0).astype(eall_ref.dtype)


def _softmax_rows(l):
    m = jnp.max(l, axis=1, keepdims=True)
    e = jnp.exp(l - m)
    return e / jnp.sum(e, axis=1, keepdims=True)


def _t128(x):
    rows = x.shape[0]
    if rows < LANES:
        x = jnp.concatenate([x, jnp.zeros((LANES - rows, LANES), x.dtype)], axis=0)
    return x.T


def _nsa_sample_kernel(pt_ref, rb_ref, q_ref, gate_ref, kvs_ref, kvw_ref, kc_ref, vc_ref, win_ref, pool_ref,
                       o_ref, kt_ref, vt_ref, sem_ref, cbt_ref, tail_ref, wtab_ref, ov_ref, eall_ref,
                       *, ls, past, npages, page_size, ncp, wk, wkp, nk):
    b = pl.program_id(0)

    @pl.when(b == 0)
    def _():
        _nsa_sample_tables(rb_ref, cbt_ref, tail_ref, wtab_ref, ov_ref, eall_ref,
                           ls=ls, past=past, ncp=ncp, wk=wk, wkp=wkp, nk=nk)

    def page_copies(seq, slot, p):
        pg = pt_ref[seq * npages + p]
        cols = pl.ds(pl.multiple_of(p * page_size, page_size), page_size)
        return [pltpu.make_async_copy(pool_ref.at[pg, 0], kt_ref.at[slot, :, cols], sem_ref.at[0, slot]),
                pltpu.make_async_copy(pool_ref.at[pg, 1], vt_ref.at[slot, :, cols], sem_ref.at[1, slot])]

    slot = _gather_step(page_copies, npages)
    kvs = kvs_ref[...]
    kt_ref[slot, :, past:nk] = _t128(kvs[:, 0:KV_WIDTH])
    vt_ref[slot, :, past:nk] = _t128(kvs[:, KV_WIDTH:])
    kvw = kvw_ref[...]
    kwt = jnp.concatenate([win_ref[0, 0], _t128(kvw[:, 0:KV_WIDTH])], axis=1).astype(MXU)
    vwt = jnp.concatenate([win_ref[0, 1], _t128(kvw[:, KV_WIDTH:])], axis=1).astype(MXU)

    gw = NSA_GROUP * HEAD_DIM
    ng = NSA_KV_HEADS
    qall = jnp.concatenate([t for g in range(ng) for t in _place_heads(q_ref[:, g * gw:(g + 1) * gw], g)],
                           axis=0).astype(MXU)
    per_group = lambda ref: jnp.concatenate([ref[g] for g in range(ng)], axis=0)
    pc = _softmax_rows(_mm_nt(qall, kc_ref[0]) + per_group(cbt_ref)).astype(MXU)
    o_cmp = jnp.dot(pc, vc_ref[0].astype(MXU), preferred_element_type=F32)
    impm = jnp.dot(pc, ov_ref[...], preferred_element_type=F32)
    imp = jnp.concatenate([sum(impm[(g * NSA_GROUP + r) * ls:(g * NSA_GROUP + r + 1) * ls] for r in range(NSA_GROUP))
                           for g in range(ng)], axis=0)
    lane_s = lax.broadcasted_iota(jnp.int32, imp.shape, 1)
    qblk = (past + lax.broadcasted_iota(jnp.int32, imp.shape, 0) % ls) // SLC_BLOCK
    vals = jnp.where((lane_s == 0) | (lane_s == qblk) | (lane_s == qblk - 1), 3e38, imp)
    vals = jnp.where(lane_s <= qblk, vals, -1.0)
    lane_sf = lane_s.astype(F32)
    sel = jnp.zeros(vals.shape, F32)
    for _ in range(SLC_TOPN):
        mx = jnp.max(vals, axis=1, keepdims=True)
        idx = jnp.min(jnp.where(vals == mx, lane_sf, float(2 * LANES)), axis=1, keepdims=True)
        pick = lane_sf == idx
        sel = jnp.where(pick, 1.0, sel)
        vals = jnp.where(pick, -2.0, vals)
    selneg = ((sel - 1.0) * BIG).astype(MXU)
    selneg = jnp.concatenate([selneg[g * ls:(g + 1) * ls] for g in range(ng) for _ in range(NSA_GROUP)], axis=0)
    lsl = (jnp.dot(qall, kt_ref[slot].astype(MXU), preferred_element_type=F32)
           + jnp.dot(selneg, eall_ref[...], preferred_element_type=F32))
    lsl = jnp.concatenate([lsl[:, 0:nk - TAIL], lsl[:, nk - TAIL:] + per_group(tail_ref)], axis=1)
    o_slc = _mm_nt(_softmax_rows(lsl), vt_ref[slot])
    pw = _softmax_rows(jnp.dot(qall, kwt, preferred_element_type=F32) + per_group(wtab_ref))
    o_win = _mm_nt(pw, vwt)
    for g in range(ng):
        gates = gate_ref[:, g * LANES:(g + 1) * LANES]
        ohs = []
        for r in range(NSA_GROUP):
            rows = slice((g * NSA_GROUP + r) * ls, (g * NSA_GROUP + r + 1) * ls)
            ohs.append(_gate_mix(gates, r, o_cmp[rows], o_slc[rows], o_win[rows]))
        o_ref[:, g * gw:(g + 1) * gw] = _unplace_heads(ohs, g)


def _nsa_sample(rel_bias, q, gates, kvs, kvw, kc, vc, win_t, pool_t, page_table, ls):
    nseq, npages = page_table.shape
    page_size = pool_t.shape[3]
    past = npages * page_size
    ncp = kc.shape[1]
    wk = win_t.shape[3]
    wkp = wk + LANES
    nk = past + LANES
    assert past % SLC_BLOCK == 0 and ls % 8 == 0 and ls <= SLC_BLOCK and past >= wk and past // SLC_BLOCK + 1 <= 2 * LANES
    assert ncp % LANES == 0 and past >= CMP_BLOCK and page_size % LANES == 0 and wk % LANES == 0 and KV_WIDTH == LANES
    row = lambda width: pl.BlockSpec((ls, width), lambda b, pt: (b, 0))
    seq3 = lambda n, width: pl.BlockSpec((1, n, width), lambda b, pt: (b, 0, 0))
    kw = 2 * KV_WIDTH
    nrow = NSA_GROUP * ls
    return pl.pallas_call(
        functools.partial(_nsa_sample_kernel, ls=ls, past=past, npages=npages, page_size=page_size, ncp=ncp,
                          wk=wk, wkp=wkp, nk=nk),
        grid_spec=pltpu.PrefetchScalarGridSpec(
            num_scalar_prefetch=1,
            grid=(nseq,),
            in_specs=[pl.BlockSpec(memory_space=pltpu.SMEM), row(NSA_INNER), row(NSA_KV_HEADS * LANES), row(kw), row(kw),
                      seq3(ncp, LANES), seq3(ncp, LANES),
                      pl.BlockSpec((1, 2, KV_WIDTH, wk), lambda b, pt: (b, 0, 0, 0)), pl.BlockSpec(memory_space=pl.ANY)],
            out_specs=row(NSA_INNER),
            scratch_shapes=[pltpu.VMEM((2, KV_WIDTH, nk), F32), pltpu.VMEM((2, KV_WIDTH, nk), F32),
                            pltpu.SemaphoreType.DMA((2, 2)),
                            pltpu.VMEM((NSA_KV_HEADS, nrow, ncp), F32), pltpu.VMEM((NSA_KV_HEADS, nrow, TAIL), F32),
                            pltpu.VMEM((NSA_KV_HEADS, nrow, wkp), F32),
                            pltpu.VMEM((ncp, 2 * LANES), MXU), pltpu.VMEM((2 * LANES, nk), MXU)],
        ),
        out_shape=jax.ShapeDtypeStruct((nseq * ls, NSA_INNER), F32),
        compiler_params=_cparams("arbitrary"),
    )(page_table.reshape(-1), rel_bias, q, gates, kvs, kvw, kc, vc, win_t, pool_t)


def _mix_out_kernel(x_ref, yssd_ref, onsa_ref, gn_ref, wo_ref, gq_ref, wq_ref, mkv_ref, wmo_ref, o_ref):
    y = jnp.concatenate([yssd_ref[...], _rms(onsa_ref[...], gn_ref[...])], axis=1)
    x = x_ref[...] + jnp.dot(y.astype(MXU), wo_ref[...], preferred_element_type=F32)
    qm = jnp.dot(_rms(x, gq_ref[...]).astype(MXU), wq_ref[...], preferred_element_type=F32)
    kv = mkv_ref[0]
    km = kv[:, 0:MEM_INNER].astype(MXU)
    vm = kv[:, MEM_INNER:].astype(MXU)
    head = lax.broadcasted_iota(jnp.int32, qm.shape, 1) // HEAD_DIM
    o = jnp.zeros(qm.shape, F32)
    for h in range(MEM_HEADS):
        p = _softmax_rows(_mm_nt(jnp.where(head == h, qm, 0.0), km) * (HEAD_DIM ** -0.5))
        o = o + jnp.where(head == h, jnp.dot(p.astype(MXU), vm, preferred_element_type=F32), 0.0)
    o_ref[...] = x + jnp.dot(o.astype(MXU), wmo_ref[...], preferred_element_type=F32)


def _mix_out(x, yssd, onsa, g_nsa, w_out, g_memq, w_mem_q, mem_kv, w_mem_out, nseq, seq_len, tm=512):
    t, d = x.shape
    tm = min(tm, seq_len)
    nt = seq_len // tm
    assert nt * tm == seq_len
    m = mem_kv.shape[1]
    row = lambda width: pl.BlockSpec((tm, width), lambda b, i: (b * nt + i, 0))
    return pl.pallas_call(
        _mix_out_kernel,
        grid=(nseq, nt),
        in_specs=[row(d), row(SSD_INNER), row(NSA_INNER), _const_spec((1, NSA_INNER)), _const_spec(w_out.shape),
                  _const_spec((1, d)), _const_spec(w_mem_q.shape),
                  pl.BlockSpec((1, m, 2 * MEM_INNER), lambda b, i: (b, 0, 0)), _const_spec(w_mem_out.shape)],
        out_specs=row(d),
        out_shape=jax.ShapeDtypeStruct((t, d), F32),
        compiler_params=_cparams("parallel", "parallel"),
    )(x, yssd, onsa, g_nsa.reshape(1, -1), w_out.astype(MXU), g_memq.reshape(1, d), w_mem_q.astype(MXU),
      mem_kv, w_mem_out.astype(MXU))


def _norm_matmul_kernel(x_ref, g_ref, w_ref, o_ref):
    o_ref[...] = jnp.dot(_rms(x_ref[...], g_ref[...]).astype(MXU), w_ref[...], preferred_element_type=F32)


def _norm_matmul(x, g, w, tm=256):
    t, d = x.shape
    n = w.shape[1]
    tm = min(tm, t)
    assert t % tm == 0
    return pl.pallas_call(
        _norm_matmul_kernel,
        grid=(t // tm,),
        in_specs=[pl.BlockSpec((tm, d), lambda i: (i, 0)), _const_spec((1, d)), _const_spec(w.shape)],
        out_specs=pl.BlockSpec((tm, n), lambda i: (i, 0)),
        out_shape=jax.ShapeDtypeStruct((t, n), F32),
        compiler_params=_cparams("parallel"),
    )(x, g.reshape(1, d), w.astype(MXU))


def _trunk_layer(x, lw, cw, rel_bias, mem_kv, conv0, h0, nseq, seq_len, sample=None):
    x1 = _ffn(x, lw['norm_ffn1'], lw['w_ffn1_in'], lw['w_ffn1_out'])
    outs = _proj(x1, lw['norm_mix'], lw['w_mix'], seq_len=None if sample else seq_len)
    z, xbc, q, kvc, kvs, kvw, gates, dt = outs[:8]
    y_ssd, h_new, conv_new = _ssd(z, xbc, dt, conv0, h0, lw['conv_w'], lw['conv_b'], lw['dt_bias'], lw['a_log'],
                                  lw['d_skip'], lw['norm_ssd_out'], nseq, seq_len)
    if sample is None:
        kc, vc = _compress_prompt(kvc, cw, nseq, seq_len)
        o_nsa = _nsa_prompt(rel_bias, q, gates, kc, vc, *outs[8:12], nseq, seq_len)
    else:
        kc, vc = _compress_sample(sample['pool_c'], sample['page_table'], cw)
        o_nsa = _nsa_sample(rel_bias, q, gates, kvs, kvw, kc, vc, sample['win_t'], sample['pool_s'],
                            sample['page_table'], seq_len)
    x2 = _mix_out(x1, y_ssd, o_nsa, lw['norm_nsa_out'], lw['w_out'], lw['norm_mem_q'], lw['w_mem_q'], mem_kv,
                  lw['w_mem_out'], nseq, seq_len)
    return x2, kvc, kvs, kvw, h_new, conv_new


def kernel(x_prompt, x_sample, cache_kv_cmp, cache_kv_slc, state_win_kv, state_ssd, state_conv, cache_mem_kv, page_table, mem_prompt, norm_ffn1, w_ffn1_in, w_ffn1_out, norm_mix, w_in, conv_w, conv_b, dt_bias, a_log, d_skip, norm_ssd_out, cmp_pos_k, cmp_w1_k, cmp_w2_k, cmp_pos_v, cmp_w1_v, cmp_w2_v, norm_nsa_out, w_out, norm_mem_q, norm_mem_kv, w_mem_q, w_mem_kv, w_mem_out, norm_ffn2, w_ffn2_in, w_ffn2_out, rel_bias, norm_final):
    bp, lp, d = x_prompt.shape
    bs, lsq, _ = x_sample.shape
    depth = w_in.shape[0]
    xp = x_prompt.reshape(bp * lp, d)
    xs = x_sample.reshape(bs * lsq, d)
    outs_p = [[] for _ in range(6)]
    outs_s = [[] for _ in range(5)]
    for layer in range(depth):
        last = layer == depth - 1
        lw = {
            'norm_ffn1': norm_ffn1[layer], 'w_ffn1_in': w_ffn1_in[layer], 'w_ffn1_out': w_ffn1_out[layer],
            'norm_mix': norm_mix[layer], 'w_mix': _regroup_w_in(w_in[layer]), 'conv_w': conv_w[layer],
            'conv_b': conv_b[layer], 'dt_bias': dt_bias[layer], 'a_log': a_log[layer], 'd_skip': d_skip[layer],
            'norm_ssd_out': norm_ssd_out[layer], 'norm_nsa_out': norm_nsa_out[layer], 'w_out': w_out[layer],
            'norm_mem_q': norm_mem_q[layer], 'w_mem_q': w_mem_q[layer], 'w_mem_out': w_mem_out[layer],
        }
        cw = _compress_weights(cmp_pos_k[layer], cmp_w1_k[layer], cmp_w2_k[layer],
                               cmp_pos_v[layer], cmp_w1_v[layer], cmp_w2_v[layer])
        g_final = norm_final if last else None
        mem_kv_p = _norm_matmul(mem_prompt.reshape(-1, d), norm_mem_kv[layer], w_mem_kv[layer])
        mem_kv_p = mem_kv_p.reshape(bp, -1, 2 * MEM_INNER)
        conv0 = jnp.zeros((bp, SSD_CONV - 1, SSD_CONV_DIM), F32)
        h0 = jnp.zeros((bp, SSD_HEADS, HEAD_DIM, SSD_STATE), F32)
        xp, c_p, s_p, w_p, h_p, conv_p = _trunk_layer(xp, lw, cw, rel_bias, mem_kv_p, conv0, h0, bp, lp)
        xp = _ffn(xp, norm_ffn2[layer], w_ffn2_in[layer], w_ffn2_out[layer], g_final)
        sample = dict(pool_c=_paged_view(cache_kv_cmp[layer]), pool_s=_paged_view(cache_kv_slc[layer]),
                      win_t=_paged_view(state_win_kv[layer]), page_table=page_table)
        mem_kv_s = cache_mem_kv[layer].reshape(bs, -1, 2 * MEM_INNER)
        xs, c_s, s_s, w_s, h_s, conv_s = _trunk_layer(xs, lw, cw, rel_bias, mem_kv_s, state_conv[layer],
                                                      state_ssd[layer], bs, lsq, sample=sample)
        xs = _ffn(xs, norm_ffn2[layer], w_ffn2_in[layer], w_ffn2_out[layer], g_final)
        kv6 = lambda a, n, l: a.reshape(n, l, 2, NSA_KV_HEADS, HEAD_DIM)
        w_p = kv6(w_p, bp, lp)[:, lp - min(WINDOW, lp):]
        w_s = jnp.concatenate([state_win_kv[layer], kv6(w_s, bs, lsq)], axis=1)[:, lsq:]
        for lst, v in zip(outs_p, (kv6(c_p, bp, lp), kv6(s_p, bp, lp), w_p, h_p, conv_p,
                                   mem_kv_p.reshape(bp, -1, 2, MEM_HEADS, HEAD_DIM))):
            lst.append(v)
        for lst, v in zip(outs_s, (kv6(c_s, bs, lsq), kv6(s_s, bs, lsq), w_s, h_s, conv_s)):
            lst.append(v)
    return (xp.reshape(bp, lp, d), xs.reshape(bs, lsq, d),
            *(jnp.stack(o) for o in outs_p), *(jnp.stack(o) for o in outs_s))
```

```python
import functools
import math

import numpy as np
import jax
import jax.numpy as jnp
from jax import lax
from jax.experimental import pallas as pl
from jax.experimental.pallas import tpu as pltpu

F32 = jnp.float32
MXU = jnp.bfloat16

HEAD_DIM = 64
SSD_HEADS = 8
SSD_GROUPS = 2
SSD_STATE = 128
SSD_CONV = 4
SSD_CHUNK = 128
NSA_HEADS = 8
NSA_KV_HEADS = 2
NSA_GROUP = NSA_HEADS // NSA_KV_HEADS
CMP_BLOCK = 32
CMP_STRIDE = 16
CMP_HIDDEN = 256
SLC_BLOCK = 64
SLC_TOPN = 16
WINDOW = 512
REL_BUCKETS = 32
REL_MAX_DIST = 128
MEM_HEADS = 4
Q_BLOCK = 128
EPS = 1e-6
NEG = -1e30

SSD_INNER = SSD_HEADS * HEAD_DIM
NSA_INNER = NSA_HEADS * HEAD_DIM
KV_WIDTH = NSA_KV_HEADS * HEAD_DIM
SSD_CONV_DIM = SSD_INNER + 2 * SSD_GROUPS * SSD_STATE
MEM_INNER = MEM_HEADS * HEAD_DIM
LANES = 128
VMEM_LIMIT = 56 * 1024 * 1024


def _cparams(*sem):
    return pltpu.CompilerParams(dimension_semantics=sem, vmem_limit_bytes=VMEM_LIMIT)


def _const_spec(shape):
    nd = len(shape)
    return pl.BlockSpec(shape, lambda *a: (0,) * nd, pipeline_mode=pl.Buffered(1))


def _mm(a, b):
    return jnp.dot(a.astype(MXU), b.astype(MXU), preferred_element_type=F32)


def _mm_nt(a, b):
    return lax.dot_general(a.astype(MXU), b.astype(MXU), (((1,), (1,)), ((), ())), preferred_element_type=F32)


def _rms(x, g):
    return x * lax.rsqrt(jnp.mean(x * x, axis=-1, keepdims=True) + EPS) * g


def _silu(x):
    return x * jax.nn.sigmoid(x)


FFN_CHUNK = 256


def _ffn_kernel(x_ref, g_ref, win_ref, wout_ref, *rest, nchunks, final_norm):
    if final_norm:
        gf_ref, o_ref, h_ref = rest
    else:
        o_ref, h_ref = rest
    x = x_ref[...]
    h_ref[...] = _rms(x, g_ref[...]).astype(h_ref.dtype)
    f = nchunks * FFN_CHUNK
    for j in range(nchunks):
        cols = slice(j * FFN_CHUNK, (j + 1) * FFN_CHUNK)
        h = h_ref[...]
        gate = jnp.dot(h, win_ref[:, cols], preferred_element_type=F32)
        up = jnp.dot(h, win_ref[:, f + j * FFN_CHUNK:f + (j + 1) * FFN_CHUNK], preferred_element_type=F32)
        a = (_silu(gate) * up).astype(h.dtype)
        t = jnp.dot(a, wout_ref[cols, :], preferred_element_type=F32)
        if j == 0:
            o_ref[...] = t
        else:
            o_ref[...] += t
    y = x_ref[...] + 0.5 * o_ref[...]
    if final_norm:
        y = _rms(y, gf_ref[...])
    o_ref[...] = y


def _ffn(x, g, w_in, w_out, g_final=None, tm=512):
    t, d = x.shape
    f = w_out.shape[0]
    nchunks = f // FFN_CHUNK
    tm = min(tm, t)
    assert nchunks * FFN_CHUNK == f and t % tm == 0
    win = w_in.astype(MXU)
    wout = w_out.astype(MXU)
    final_norm = g_final is not None
    args = [x, g.reshape(1, d), win, wout]
    specs = [pl.BlockSpec((tm, d), lambda i: (i, 0)), _const_spec((1, d)),
             _const_spec(win.shape), _const_spec(wout.shape)]
    if final_norm:
        args.append(g_final.reshape(1, d))
        specs.append(_const_spec((1, d)))
    return pl.pallas_call(
        functools.partial(_ffn_kernel, nchunks=nchunks, final_norm=final_norm),
        grid=(t // tm,),
        in_specs=specs,
        out_specs=pl.BlockSpec((tm, d), lambda i: (i, 0)),
        out_shape=jax.ShapeDtypeStruct((t, d), F32),
        scratch_shapes=[pltpu.VMEM((tm, d), MXU)],
        compiler_params=_cparams("parallel"),
    )(*args)


PROJ_COLS = 3200
_O_Z, _O_XBC, _O_Q, _O_KVC, _O_KVS, _O_KVW, _O_G, _O_DT = 0, 512, 1536, 2048, 2304, 2560, 2816, 3072


def _regroup_w_in(w_in):
    o = np.cumsum([0, SSD_INNER, SSD_CONV_DIM, SSD_HEADS, NSA_INNER] + [KV_WIDTH] * 6 + [3 * NSA_HEADS])
    z, xbc, dt, q, kvs, gl = (w_in[:, o[0]:o[1]], w_in[:, o[1]:o[2]], w_in[:, o[2]:o[3]], w_in[:, o[3]:o[4]],
                              w_in[:, o[4]:o[10]], w_in[:, o[10]:o[11]])
    pad = lambda w: jnp.pad(w, ((0, 0), (0, LANES - w.shape[1])))
    ng = 3 * NSA_GROUP
    gls = [pad(gl[:, g * ng:(g + 1) * ng]) for g in range(NSA_KV_HEADS)]
    return jnp.concatenate([z, xbc, q, kvs] + gls + [pad(dt)], axis=1).astype(MXU)


def _proj_kernel(x_ref, g_ref, w_ref, z_ref, xbc_ref, q_ref, kvc_ref, kvs_ref, kvw_ref, gate_ref, dt_ref, *aug,
                 tiles_per_seq):
    h = _rms(x_ref[...], g_ref[...]).astype(w_ref.dtype)
    res = jnp.dot(h, w_ref[...], preferred_element_type=F32)
    tm = res.shape[0]
    z_ref[...] = res[:, _O_Z:_O_XBC]
    xbc_ref[...] = res[:, _O_XBC:_O_Q]
    q_ref[...] = res[:, _O_Q:_O_KVC] * (HEAD_DIM ** -0.5)
    kvc_ref[...] = res[:, _O_KVC:_O_KVS]
    kvs = res[:, _O_KVS:_O_KVW]
    kvw = res[:, _O_KVW:_O_G]
    kvs_ref[...] = kvs
    kvw_ref[...] = kvw
    gate_ref[...] = jax.nn.sigmoid(res[:, _O_G:_O_DT])
    dt_ref[...] = res[:, _O_DT:PROJ_COLS]
    if aug:
        ksa_ref, vsa_ref, kw_ref, vwa_ref, kvct_ref, kvst_ref = aug
        kvct_ref[0] = res[:, _O_KVC:_O_KVS].T
        kvst_ref[0] = kvs.T
        lane = lax.broadcasted_iota(jnp.int32, (tm, LANES), 1)
        row = lax.broadcasted_iota(jnp.int32, (tm, LANES), 0)
        pos = (pl.program_id(0) % tiles_per_seq) * tm + row
        onehot = jnp.where(lane == pos // SLC_BLOCK, 1.0, 0.0)
        ones_col = jnp.where(lane == 0, 1.0, 0.0)
        dt_ = ksa_ref.dtype
        ksa_ref[:, 0:LANES] = kvs[:, 0:LANES].astype(dt_)
        ksa_ref[:, LANES:] = onehot.astype(dt_)
        vsa_ref[:, 0:LANES] = kvs[:, LANES:].astype(dt_)
        vsa_ref[:, LANES:] = ones_col.astype(dt_)
        kw_ref[...] = kvw[:, 0:LANES].astype(dt_)
        vwa_ref[:, 0:LANES] = kvw[:, LANES:].astype(dt_)
        vwa_ref[:, LANES:] = ones_col.astype(dt_)


def _proj(x, g, w, seq_len=None, tm=512):
    t, d = x.shape
    tm = min(tm, t)
    assert t % tm == 0
    widths = [512, 1024, 512, 256, 256, 256, NSA_KV_HEADS * LANES, LANES]
    shapes = [jax.ShapeDtypeStruct((t, wd), F32) for wd in widths]
    specs = [pl.BlockSpec((tm, wd), lambda i: (i, 0)) for wd in widths]
    aug = seq_len is not None
    tps = seq_len // tm if aug else 1
    if aug:
        assert seq_len % tm == 0 and seq_len // SLC_BLOCK <= LANES
        for wd in (256, 256, LANES, 256):
            shapes.append(jax.ShapeDtypeStruct((t, wd), MXU))
            specs.append(pl.BlockSpec((tm, wd), lambda i: (i, 0)))
        for _ in range(2):
            shapes.append(jax.ShapeDtypeStruct((t // seq_len, 2 * KV_WIDTH, seq_len), F32))
            specs.append(pl.BlockSpec((1, 2 * KV_WIDTH, tm), lambda i: (i // tps, 0, i % tps)))
    return pl.pallas_call(
        functools.partial(_proj_kernel, tiles_per_seq=tps),
        grid=(t // tm,),
        in_specs=[pl.BlockSpec((tm, d), lambda i: (i, 0)), _const_spec((1, d)), _const_spec(w.shape)],
        out_specs=specs,
        out_shape=shapes,
        compiler_params=_cparams("parallel"),
    )(x, g.reshape(1, d), w)


HALO = 8


def _split3(x):
    hi = x.astype(jnp.bfloat16).astype(F32)
    r = x - hi
    mid = r.astype(jnp.bfloat16).astype(F32)
    return hi, mid, r - mid


def _ssd_kernel(z_ref, xbc_ref, dt_ref, conv0_ref, h0_ref, cw_ref, cb_ref, dtb_ref, alog_ref, dskip_ref, g_ref,
                y_ref, hout_ref, convout_ref, xwin_ref, h_ref, *, lv):
    q = SSD_CHUNK
    c = pl.program_id(1)

    @pl.when(c == 0)
    def _():
        xwin_ref[0:HALO, :] = conv0_ref[0]
        h_ref[...] = h0_ref[0]

    def padrows(v):
        if lv == q:
            return v
        return jnp.concatenate([v, jnp.zeros((q - lv, v.shape[1]), v.dtype)], axis=0)

    xwin_ref[HALO:HALO + q, :] = padrows(xbc_ref[...])
    acc = cb_ref[...] + cw_ref[SSD_CONV - 1:SSD_CONV, :] * xwin_ref[HALO:HALO + q, :]
    for k in range(SSD_CONV - 1):
        acc = acc + cw_ref[k:k + 1, :] * xwin_ref[pl.ds(HALO - (SSD_CONV - 1) + k, q), :]
    xc = _silu(acc)
    tail = xwin_ref[lv:lv + HALO, :]
    xwin_ref[0:HALO, :] = tail
    convout_ref[0] = tail

    xs = xc[:, 0:SSD_INNER]
    rowi = lax.broadcasted_iota(jnp.int32, (q, q), 0)
    coli = lax.broadcasted_iota(jnp.int32, (q, q), 1)
    causal = rowi >= coli
    tri = jnp.where(causal, 1.0, 0.0).astype(jnp.bfloat16)

    xdt = padrows(dt_ref[...]) + dtb_ref[...]
    dt = jnp.maximum(xdt, 0.0) + jnp.log1p(jnp.exp(-jnp.abs(xdt)))
    if lv != q:
        dt = jnp.where(rowi < lv, dt, 0.0)
    da = dt * (-jnp.exp(alog_ref[...]))
    acs = sum(jnp.dot(tri, p.astype(jnp.bfloat16), preferred_element_type=F32) for p in _split3(da))
    acs_t = acs.T
    dt_t = dt.T
    last = acs[q - 1:q, :]
    last_t = acs_t[:, q - 1:q]
    xs_t = [xs[:, p * LANES:(p + 1) * LANES].T for p in range(SSD_HEADS // 2)]
    lane_hi = lax.broadcasted_iota(jnp.int32, (q, LANES), 1) >= HEAD_DIM
    row_hi = lax.broadcasted_iota(jnp.int32, (LANES, q), 0) >= HEAD_DIM
    hpg = SSD_HEADS // SSD_GROUPS
    ys = []
    for g in range(SSD_GROUPS):
        bm = xc[:, SSD_INNER + g * SSD_STATE:SSD_INNER + (g + 1) * SSD_STATE]
        cm = xc[:, SSD_INNER + (SSD_GROUPS + g) * SSD_STATE:SSD_INNER + (SSD_GROUPS + g + 1) * SSD_STATE]
        cb = _mm_nt(cm, bm)
        for pp in range(hpg // 2):
            p = g * (hpg // 2) + pp
            xpair = xs[:, p * LANES:(p + 1) * LANES]
            yi = []
            for hh in range(2):
                h = 2 * p + hh
                seg = acs[:, h:h + 1] - acs_t[h:h + 1, :]
                decay = jnp.where(causal, jnp.exp(jnp.where(causal, seg, 0.0)), 0.0)
                yi.append(_mm(cb * decay * dt_t[h:h + 1, :], xpair))
            y_intra = jnp.where(lane_hi, yi[1], yi[0])
            h0, h1 = 2 * p, 2 * p + 1
            grow = jnp.where(lane_hi, jnp.exp(acs[:, h1:h1 + 1]), jnp.exp(acs[:, h0:h0 + 1]))
            y_inter = _mm_nt(cm, h_ref[p]) * grow
            ys.append(y_intra + y_inter)
            te = jnp.where(row_hi,
                           jnp.exp(last_t[h1:h1 + 1, :] - acs_t[h1:h1 + 1, :]) * dt_t[h1:h1 + 1, :],
                           jnp.exp(last_t[h0:h0 + 1, :] - acs_t[h0:h0 + 1, :]) * dt_t[h0:h0 + 1, :])
            states = _mm(xs_t[p] * te, bm)
            rdec = jnp.where(row_hi[:, 0:1], jnp.exp(last_t[h1:h1 + 1, :]), jnp.exp(last_t[h0:h0 + 1, :]))
            h_ref[p] = h_ref[p] * rdec + states
    y = jnp.concatenate(ys, axis=1) + dskip_ref[...] * xs
    y = _rms(y * _silu(padrows(z_ref[...])), g_ref[...])
    y_ref[...] = y[0:lv, :]
    hout_ref[0] = h_ref[...]


def _ssd(z, xbc, dt, conv0, h0, conv_w, conv_b, dt_bias, a_log, d_skip, norm_g, nseq, seq_len):
    q = SSD_CHUNK
    lv = min(q, seq_len)
    nch = seq_len // lv
    assert nch * lv == seq_len
    t = nseq * seq_len
    cdim = SSD_CONV_DIM
    keep = SSD_CONV - 1
    conv0p = jnp.pad(conv0, ((0, 0), (HALO - keep, 0), (0, 0)))
    padl = lambda v: jnp.pad(v.reshape(1, -1), ((0, 0), (0, LANES - v.size)))
    hp = h0.reshape(nseq, SSD_HEADS // 2, 2 * HEAD_DIM, SSD_STATE)
    row = lambda b, c: (b * nch + c, 0)
    y, hout, convout = pl.pallas_call(
        functools.partial(_ssd_kernel, lv=lv),
        grid=(nseq, nch),
        in_specs=[pl.BlockSpec((lv, SSD_INNER), row), pl.BlockSpec((lv, cdim), row), pl.BlockSpec((lv, LANES), row),
                  pl.BlockSpec((1, HALO, cdim), lambda b, c: (b, 0, 0)),
                  pl.BlockSpec((1,) + hp.shape[1:], lambda b, c: (b, 0, 0, 0)),
                  _const_spec((SSD_CONV, cdim)), _const_spec((1, cdim)), _const_spec((1, LANES)),
                  _const_spec((1, LANES)), _const_spec((1, SSD_INNER)), _const_spec((1, SSD_INNER))],
        out_specs=[pl.BlockSpec((lv, SSD_INNER), row),
                   pl.BlockSpec((1,) + hp.shape[1:], lambda b, c: (b, 0, 0, 0)),
                   pl.BlockSpec((1, HALO, cdim), lambda b, c: (b, 0, 0))],
        out_shape=[jax.ShapeDtypeStruct((t, SSD_INNER), F32), jax.ShapeDtypeStruct(hp.shape, F32),
                   jax.ShapeDtypeStruct((nseq, HALO, cdim), F32)],
        scratch_shapes=[pltpu.VMEM((HALO + q, cdim), F32), pltpu.VMEM(hp.shape[1:], F32)],
        compiler_params=_cparams("parallel", "arbitrary"),
    )(z, xbc, dt, conv0p, hp, conv_w, conv_b.reshape(1, cdim), padl(dt_bias), padl(a_log),
      jnp.repeat(d_skip, HEAD_DIM).reshape(1, SSD_INNER), norm_g.reshape(1, SSD_INNER))
    return y, hout.reshape(nseq, SSD_HEADS, HEAD_DIM, SSD_STATE), convout[:, HALO - keep:, :]


def _compress_weights(pe_k, w1_k, w2_k, pe_v, w1_v, w2_v):
    def w1_pair(w1):
        w = w1.reshape(CMP_BLOCK, HEAD_DIM, CMP_HIDDEN)
        zz = jnp.zeros_like(w)
        w = jnp.concatenate([jnp.concatenate([w, zz], axis=2), jnp.concatenate([zz, w], axis=2)], axis=1)
        return w.reshape(CMP_BLOCK // 2, 2 * KV_WIDTH, NSA_KV_HEADS * CMP_HIDDEN).astype(MXU)

    def w2_pair(w2):
        zz = jnp.zeros_like(w2)
        return jnp.concatenate([jnp.concatenate([w2, zz], axis=1), jnp.concatenate([zz, w2], axis=1)], axis=0).astype(MXU)

    pe = jnp.concatenate([pe_k, pe_k, pe_v, pe_v], axis=1)
    return pe, w1_pair(w1_k), w1_pair(w1_v), w2_pair(w2_k), w2_pair(w2_v)


def _compress_compute(row_of_chunks, pe_ref, wk_ref, wv_ref, w2k_ref, w2v_ref, sh_ref, kc_ref, vc_ref, nc):
    half = CMP_STRIDE
    a0k = a1k = a0v = a1v = None
    for s2 in range(half // 2):
        rows = [row_of_chunks(2 * s2 + u) for u in range(2)]

        def lhs(j, kv):
            lanes = slice(kv * LANES, (kv + 1) * LANES)
            return jnp.concatenate([(rows[u][kv] + pe_ref[j * half + 2 * s2 + u:j * half + 2 * s2 + u + 1, lanes])
                                    for u in range(2)], axis=1).astype(MXU)

        t0k = jnp.dot(lhs(0, 0), wk_ref[s2], preferred_element_type=F32)
        t0v = jnp.dot(lhs(0, 1), wv_ref[s2], preferred_element_type=F32)
        t1k = jnp.dot(lhs(1, 0), wk_ref[half // 2 + s2], preferred_element_type=F32)
        t1v = jnp.dot(lhs(1, 1), wv_ref[half // 2 + s2], preferred_element_type=F32)
        a0k, a1k = (t0k, t1k) if s2 == 0 else (a0k + t0k, a1k + t1k)
        a0v, a1v = (t0v, t1v) if s2 == 0 else (a0v + t0v, a1v + t1v)
    sh_ref[nc:nc + 8, :] = jnp.zeros((8, sh_ref.shape[1]), F32)
    for a0, a1, w2_ref, o_ref in ((a0k, a1k, w2k_ref, kc_ref), (a0v, a1v, w2v_ref, vc_ref)):
        sh_ref[0:nc, :] = a1
        hid = a0 + sh_ref[pl.ds(1, nc), :]
        o_ref[...] = jnp.dot(_silu(hid).astype(MXU), w2_ref[...], preferred_element_type=F32)


def _compress_prompt_kernel(rows_ref, pe_ref, wk_ref, wv_ref, w2k_ref, w2v_ref, kc_ref, vc_ref, sh_ref, *, nc):
    rw = 2 * KV_WIDTH

    def row_of_chunks(s):
        return rows_ref[:, s * rw:s * rw + LANES], rows_ref[:, s * rw + LANES:(s + 1) * rw]

    _compress_compute(row_of_chunks, pe_ref, wk_ref, wv_ref, w2k_ref, w2v_ref, sh_ref, kc_ref.at[0], vc_ref.at[0], nc)


def _compress_specs(cw):
    return [_const_spec(w.shape) for w in cw]


def _compress_prompt(kv_cmp, cw, nseq, seq_len):
    nc = seq_len // CMP_STRIDE
    out = jax.ShapeDtypeStruct((nseq, nc, LANES), F32)
    return pl.pallas_call(
        functools.partial(_compress_prompt_kernel, nc=nc),
        grid=(nseq,),
        in_specs=[pl.BlockSpec((nc, CMP_STRIDE * 2 * KV_WIDTH), lambda b: (b, 0))] + _compress_specs(cw),
        out_specs=[pl.BlockSpec((1, nc, LANES), lambda b: (b, 0, 0))] * 2,
        out_shape=[out, out],
        scratch_shapes=[pltpu.VMEM((nc + 8, NSA_KV_HEADS * CMP_HIDDEN), F32)],
        compiler_params=_cparams("parallel"),
    )(kv_cmp.reshape(nseq * nc, CMP_STRIDE * 2 * KV_WIDTH), *cw)


PAGE_UNROLL = 8


def _gather_step(page_copies, npages):
    b = pl.program_id(0)
    nb = pl.num_programs(0)
    slot = b % 2
    assert npages % PAGE_UNROLL == 0

    def for_pages(fn):
        def body(q, c):
            for u in range(PAGE_UNROLL):
                fn(q * PAGE_UNROLL + u)
            return c
        lax.fori_loop(0, npages // PAGE_UNROLL, body, 0)

    def start_all(seq, sl):
        for_pages(lambda p: [cp.start() for cp in page_copies(seq, sl, p)])

    @pl.when(b == 0)
    def _():
        start_all(0, 0)

    @pl.when(b + 1 < nb)
    def _():
        start_all(b + 1, 1 - slot)

    for_pages(lambda p: [cp.wait() for cp in page_copies(b, slot, p)])
    return slot


def _paged_view(cache):
    n_pool, page = cache.shape[:2]
    return jnp.transpose(cache, (0, 2, 3, 4, 1)).reshape(n_pool, 2, KV_WIDTH, page)


def _compress_sample_kernel(pt_ref, pool_ref, pe_ref, wk_ref, wv_ref, w2k_ref, w2v_ref, kc_ref, vc_ref,
                            buf_ref, sem_ref, rk_ref, rv_ref, sh_ref, *, nc, npages, page_size):
    def page_copies(seq, slot, p):
        return [pltpu.make_async_copy(pool_ref.at[pt_ref[seq * npages + p]], buf_ref.at[slot, p], sem_ref.at[slot])]

    slot = _gather_step(page_copies, npages)

    def to_rows(q, c):
        for u in range(PAGE_UNROLL):
            p = q * PAGE_UNROLL + u
            rows = pl.ds(pl.multiple_of(p * page_size, page_size), page_size)
            rk_ref[rows, :] = buf_ref[slot, p, 0].T
            rv_ref[rows, :] = buf_ref[slot, p, 1].T
        return c

    lax.fori_loop(0, npages // PAGE_UNROLL, to_rows, 0)

    def row_of_chunks(s):
        return rk_ref[pl.ds(s, nc, stride=CMP_STRIDE), :], rv_ref[pl.ds(s, nc, stride=CMP_STRIDE), :]

    _compress_compute(row_of_chunks, pe_ref, wk_ref, wv_ref, w2k_ref, w2v_ref, sh_ref, kc_ref.at[0], vc_ref.at[0], nc)


def _compress_sample(pool_t, page_table, cw):
    nseq, npages = page_table.shape
    page_size = pool_t.shape[3]
    past = npages * page_size
    nc = past // CMP_STRIDE
    assert page_size == LANES and KV_WIDTH == LANES
    out = jax.ShapeDtypeStruct((nseq, nc, LANES), F32)
    cspecs = [pl.BlockSpec(w.shape, lambda b, pt, nd=w.ndim: (0,) * nd, pipeline_mode=pl.Buffered(1)) for w in cw]
    return pl.pallas_call(
        functools.partial(_compress_sample_kernel, nc=nc, npages=npages, page_size=page_size),
        grid_spec=pltpu.PrefetchScalarGridSpec(
            num_scalar_prefetch=1,
            grid=(nseq,),
            in_specs=[pl.BlockSpec(memory_space=pl.ANY)] + cspecs,
            out_specs=[pl.BlockSpec((1, nc, LANES), lambda b, pt: (b, 0, 0))] * 2,
            scratch_shapes=[pltpu.VMEM((2, npages, 2, KV_WIDTH, page_size), F32), pltpu.SemaphoreType.DMA((2,)),
                            pltpu.VMEM((past, KV_WIDTH), F32), pltpu.VMEM((past, KV_WIDTH), F32),
                            pltpu.VMEM((nc + 8, NSA_KV_HEADS * CMP_HIDDEN), F32)],
        ),
        out_shape=[out, out],
        compiler_params=_cparams("arbitrary"),
    )(page_table.reshape(-1), pool_t, *cw)


def _bucket_thresholds():
    n = np.arange(0, 2 * REL_MAX_DIST)
    exact = REL_BUCKETS // 2
    out = []
    for ft in (np.float32, np.float64):
        nf = np.maximum(n, exact).astype(ft)
        large = exact + (np.log(nf / ft(exact)) / ft(math.log(REL_MAX_DIST / exact)) * ft(REL_BUCKETS - exact)).astype(np.int64)
        out.append(np.where(n < exact, n, np.minimum(large, REL_BUCKETS - 1)))
    assert (out[0] == out[1]).all() and out[1][REL_MAX_DIST] == REL_BUCKETS - 1
    return [int(np.argmax(out[1] >= b)) for b in range(1, REL_BUCKETS)]


_THR = _bucket_thresholds()


def _bias_tables(dist, rb_ref, heads):
    steps = [jnp.where(dist >= _THR[b - 1], 1.0, 0.0) for b in range(1, REL_BUCKETS)]
    far = dist >= REL_MAX_DIST
    out = []
    for h in heads:
        acc = jnp.full(dist.shape, rb_ref[0, h] - rb_ref[REL_BUCKETS - 1, h], F32)
        for b in range(1, REL_BUCKETS):
            acc = acc + steps[b - 1] * (rb_ref[b, h] - rb_ref[b - 1, h])
        out.append(jnp.where(dist < 0, NEG, jnp.where(far, 0.0, acc)))
    return out


BIG = 1e30
KT = 256
SLC_UNROLL = 8
N_BAND = 20
BAND_LO = 12
F_STEP = 2 * N_BAND
N_WK = 6
N_SK = 4
WIN_TILES = WINDOW // Q_BLOCK + 1


def _nsa_tables(rb_ref, st_ref, wt_ref, qft_ref, ov_ref, ncp):
    tq = Q_BLOCK
    heads = range(NSA_HEADS)
    t2 = lax.broadcasted_iota(jnp.int32, (tq, KT), 0)
    c2 = lax.broadcasted_iota(jnp.int32, (tq, KT), 1)
    for dd in range(N_SK - 1):
        tabs = _bias_tables(tq * dd + t2 - c2, rb_ref, heads)
        for h in heads:
            st_ref[h * N_SK + dd] = tabs[h]
    t1 = lax.broadcasted_iota(jnp.int32, (tq, LANES), 0)
    c1 = lax.broadcasted_iota(jnp.int32, (tq, LANES), 1)
    zeros = jnp.zeros((tq, LANES), F32)
    for d in range(2):
        tabs = _bias_tables(tq * d + t1 - c1, rb_ref, heads)
        for h in heads:
            wt_ref[h * N_WK + d] = tabs[h]
    for h in heads:
        st_ref[h * N_SK + N_SK - 1] = jnp.zeros((tq, KT), F32)
        wt_ref[h * N_WK + 2] = zeros
        wt_ref[h * N_WK + 3] = zeros
        wt_ref[h * N_WK + 4] = jnp.where(c1 > t1, 0.0, NEG)
        wt_ref[h * N_WK + 5] = zeros + NEG
    for g in range(NSA_KV_HEADS):
        f = c1 - (HEAD_DIM if g == 0 else 0)
        is_lo = f >= N_BAND
        u = jnp.where(is_lo, f - N_BAND, f) - BAND_LO
        live = (f >= 0) & (f < 2 * N_BAND)
        tabs = _bias_tables(t1 - CMP_STRIDE * u - (CMP_BLOCK - 1), rb_ref, range(g * NSA_GROUP, (g + 1) * NSA_GROUP))
        for r in range(NSA_GROUP):
            hi = tabs[r].astype(jnp.bfloat16).astype(F32)
            lo = jnp.where(tabs[r] <= NEG, 0.0, tabs[r] - hi)
            val = jnp.where(live, jnp.where(is_lo, lo, hi), 0.0)
            qft_ref[g * NSA_GROUP + r] = jnp.where(f == F_STEP, NEG, val)
    kk = lax.broadcasted_iota(jnp.int32, (ncp, LANES), 0)
    ss = lax.broadcasted_iota(jnp.int32, (ncp, LANES), 1)
    ratio = SLC_BLOCK // CMP_STRIDE
    nover = (CMP_BLOCK - 1) // CMP_STRIDE
    ov = (kk >= ratio * ss - nover) & (kk < ratio * ss + ratio) & (kk < ncp - 1)
    ov_ref[...] = jnp.where(ov, 1.0, 0.0).astype(ov_ref.dtype)


def _lane_half(rows):
    return lax.broadcasted_iota(jnp.int32, (rows, LANES), 1) // HEAD_DIM


def _place_heads(qg, g):
    keep = _lane_half(qg.shape[0]) == g
    out = []
    for r in range(NSA_GROUP):
        v = qg[:, (r // 2) * LANES:(r // 2 + 1) * LANES]
        v = jnp.where(g == r % 2, v, pltpu.roll(v, HEAD_DIM, 1))
        out.append(jnp.where(keep, v, 0.0))
    return out


def _unplace_heads(ohs, g):
    keep = _lane_half(ohs[0].shape[0]) == g
    cols = []
    for c in range(NSA_GROUP // 2):
        pair = []
        for r in (2 * c, 2 * c + 1):
            oh = jnp.where(keep, ohs[r], 0.0)
            pair.append(jnp.where(g == r % 2, oh, pltpu.roll(oh, HEAD_DIM, 1)))
        cols.append(pair[0] + pair[1])
    return jnp.concatenate(cols, axis=1)


def _gate_mix(gates, r, o_cmp, o_slc, o_win):
    return (gates[:, 3 * r:3 * r + 1] * o_cmp + gates[:, 3 * r + 1:3 * r + 2] * o_slc
            + gates[:, 3 * r + 2:3 * r + 3] * o_win)


def _top_blocks(imp, qpos_lane):
    vals = imp.T
    srow = lax.broadcasted_iota(jnp.int32, vals.shape, 0)
    qblk = qpos_lane // SLC_BLOCK
    forced = (srow == 0) | (srow == qblk) | (srow == qblk - 1)
    vals = jnp.where(forced, 3e38, vals)
    vals = jnp.where(srow <= qblk, vals, -1.0)
    srow_f = srow.astype(F32)
    sel = jnp.zeros(vals.shape, F32)
    for _ in range(SLC_TOPN):
        mx = jnp.max(vals, axis=0, keepdims=True)
        idx = jnp.min(jnp.where(vals == mx, srow_f, float(LANES)), axis=0, keepdims=True)
        pick = srow_f == idx
        sel = jnp.where(pick, 1.0, sel)
        vals = jnp.where(pick, -2.0, vals)
    return jnp.where(srow <= qblk, sel, 0.0).T


def _nsa_prompt_kernel(rb_ref, q_ref, gate_ref, kc_ref, vc_ref, ksa_ref, vsa_ref, kw_ref, vwa_ref, o_ref,
                       st_ref, wt_ref, qft_ref, ov_ref, lbuf_ref, m_ref, acc_ref, *, ncp):
    tq = Q_BLOCK
    i = pl.program_id(1)
    g = pl.program_id(2)

    @pl.when((pl.program_id(0) == 0) & (i == 0) & (g == 0))
    def _():
        _nsa_tables(rb_ref, st_ref, wt_ref, qft_ref, ov_ref, ncp)

    gates = gate_ref[...]
    qpos_lane = i * tq + lax.broadcasted_iota(jnp.int32, (tq, LANES), 1)
    trow = lax.broadcasted_iota(jnp.int32, (NSA_GROUP * tq, 1), 0) % tq
    row_live = (i * tq + trow) >= CMP_BLOCK - 1
    kidx = lax.broadcasted_iota(jnp.int32, (ncp, LANES), 0)
    klane = lax.broadcasted_iota(jnp.int32, (ncp, LANES), 1)
    ngrp = i // (2 * SLC_UNROLL) + 1
    nplain = jnp.maximum(ngrp - 2, 0)
    j0 = jnp.maximum(i - (WIN_TILES - 1), 0)
    qp = _place_heads(q_ref[...], g)
    heads = [g * NSA_GROUP + r for r in range(NSA_GROUP)]

    ws = pl.multiple_of(j0 * tq, tq)
    lw = _mm_nt(jnp.concatenate(qp, axis=0).astype(MXU), kw_ref[pl.ds(ws, WIN_TILES * tq), :])
    kinds = []
    for jj in range(WIN_TILES):
        d = i - j0 - jj
        kinds.append(jnp.where(d < 0, N_WK - 1, d))
    lw = lw + jnp.concatenate(
        [jnp.concatenate([wt_ref[h * N_WK + kinds[jj]] for jj in range(WIN_TILES)], axis=1) for h in heads], axis=0)
    ew = jnp.exp(lw - jnp.max(lw, axis=1, keepdims=True)).astype(MXU)
    accw = jnp.dot(ew, vwa_ref[pl.ds(ws, WIN_TILES * tq), :], preferred_element_type=F32)
    o_win = accw[:, 0:LANES] * (1.0 / accw[:, LANES:LANES + 1])

    f = klane - HEAD_DIM * (1 - g)
    u12 = kidx - (tq // CMP_STRIDE) * i + BAND_LO
    fa = jnp.where(f >= N_BAND, f - N_BAND, f)
    feat = ((fa == u12) & (f >= 0) & (f < 2 * N_BAND)) | ((f == F_STEP) & (u12 >= N_BAND - 1))
    ka = jnp.where(klane // HEAD_DIM == g, kc_ref[0], jnp.where(feat, 1.0, 0.0)).astype(MXU)
    qa = jnp.concatenate([qp[r] + qft_ref[heads[r]] for r in range(NSA_GROUP)], axis=0).astype(MXU)
    lc = _mm_nt(qa, ka)
    mc = jnp.max(lc, axis=1, keepdims=True)
    ec = jnp.exp(lc - mc)
    sc = jnp.sum(ec, axis=1, keepdims=True)
    pc = (ec * jnp.where(row_live, 1.0 / sc, 0.0)).astype(MXU)
    o_cmp = jnp.dot(pc, vc_ref[0].astype(MXU), preferred_element_type=F32)
    imp = sum(jnp.dot(pc[r * tq:(r + 1) * tq], ov_ref[...], preferred_element_type=F32) for r in range(NSA_GROUP))

    selneg = (_top_blocks(imp, qpos_lane) - 1.0) * BIG

    qs = jnp.concatenate([jnp.concatenate([qp[r], selneg], axis=1) for r in range(NSA_GROUP)], axis=0).astype(MXU)
    m_ref[...] = jnp.full(m_ref.shape, NEG, F32)
    gk = SLC_UNROLL * KT

    def stage(grp, lt):
        lbuf_ref[grp] = lt
        mg = lt[:, 0:LANES]
        for c in range(1, gk // LANES):
            mg = jnp.maximum(mg, lt[:, c * LANES:(c + 1) * LANES])
        m_ref[...] = jnp.maximum(m_ref[...], mg)

    def plain_group(grp, c):
        stage(grp, _mm_nt(qs, ksa_ref[pl.ds(pl.multiple_of(grp * gk, gk), gk), :]))
        return c

    def near_group(grp, c):
        tabs = []
        for h in heads:
            row = []
            for u in range(SLC_UNROLL):
                dd = i - 2 * (grp * SLC_UNROLL + u)
                row.append(st_ref[h * N_SK + jnp.where((dd < 0) | (dd >= N_SK), N_SK - 1, dd)])
            tabs.append(jnp.concatenate(row, axis=1))
        stage(grp, _mm_nt(qs, ksa_ref[pl.ds(pl.multiple_of(grp * gk, gk), gk), :]) + jnp.concatenate(tabs, axis=0))
        return c

    lax.fori_loop(0, nplain, plain_group, 0)
    lax.fori_loop(nplain, ngrp, near_group, 0)
    m_ref[...] = jnp.broadcast_to(jnp.max(m_ref[...], axis=1, keepdims=True), m_ref.shape)
    acc_ref[...] = jnp.zeros(acc_ref.shape, F32)

    def pv_group(grp, c):
        e = jnp.exp(lbuf_ref[grp] - jnp.concatenate([m_ref[...]] * (gk // LANES), axis=1)).astype(MXU)
        acc_ref[...] += jnp.dot(e, vsa_ref[pl.ds(pl.multiple_of(grp * gk, gk), gk), :], preferred_element_type=F32)
        return c

    lax.fori_loop(0, ngrp, pv_group, 0)
    acc = acc_ref[...]
    o_slc = acc[:, 0:LANES] * (1.0 / acc[:, LANES:LANES + 1])

    ohs = []
    for r in range(NSA_GROUP):
        rows = slice(r * tq, (r + 1) * tq)
        ohs.append(_gate_mix(gates, r, o_cmp[rows], o_slc[rows], o_win[rows]))
    o_ref[...] = _unplace_heads(ohs, g)


def _nsa_prompt(rel_bias, q, gates, kc, vc, ksa, vsa, kw, vwa, nseq, seq_len):
    tq = Q_BLOCK
    nq = seq_len // tq
    ncp = kc.shape[1]
    gw = NSA_GROUP * HEAD_DIM
    gk = SLC_UNROLL * KT
    assert seq_len % gk == 0 and seq_len >= WIN_TILES * tq and ncp % LANES == 0
    per_seq = lambda width: pl.BlockSpec((seq_len, width), lambda b, i, g: (b, 0), pipeline_mode=pl.Buffered(1))
    blk = lambda width: pl.BlockSpec((tq, width), lambda b, i, g: (b * nq + i, g))
    cmp_spec = pl.BlockSpec((1, ncp, LANES), lambda b, i, g: (b, 0, 0))
    return pl.pallas_call(
        functools.partial(_nsa_prompt_kernel, ncp=ncp),
        grid=(nseq, nq, NSA_KV_HEADS),
        in_specs=[pl.BlockSpec(memory_space=pltpu.SMEM), blk(gw), blk(LANES), cmp_spec, cmp_spec,
                  per_seq(2 * LANES), per_seq(2 * LANES), per_seq(LANES), per_seq(2 * LANES)],
        out_specs=blk(gw),
        out_shape=jax.ShapeDtypeStruct((nseq * seq_len, NSA_INNER), F32),
        scratch_shapes=[pltpu.VMEM((NSA_HEADS * N_SK, tq, KT), F32), pltpu.VMEM((NSA_HEADS * N_WK, tq, LANES), F32),
                        pltpu.VMEM((NSA_HEADS, tq, LANES), F32), pltpu.VMEM((ncp, LANES), MXU),
                        pltpu.VMEM((seq_len // gk, NSA_GROUP * tq, gk), F32),
                        pltpu.VMEM((NSA_GROUP * tq, LANES), F32), pltpu.VMEM((NSA_GROUP * tq, 2 * LANES), F32)],
        compiler_params=_cparams("arbitrary", "arbitrary", "arbitrary"),
    )(rel_bias, q, gates, kc, vc, ksa, vsa, kw, vwa)


TAIL = 2 * LANES


def _nsa_sample_tables(rb_ref, cbt_ref, tail_ref, wtab_ref, ov_ref, eall_ref, *, ls, past, ncp, wk, wkp, nk):
    heads = range(NSA_HEADS)

    def per_group(tabs):
        return [jnp.concatenate(tabs[g * NSA_GROUP:(g + 1) * NSA_GROUP], axis=0) for g in range(NSA_KV_HEADS)]

    t = lax.broadcasted_iota(jnp.int32, (ls, ncp), 0)
    k = lax.broadcasted_iota(jnp.int32, (ls, ncp), 1)
    dist = past + t - CMP_STRIDE * k - (CMP_BLOCK - 1)
    dist = jnp.where(k < ncp - 1, dist, -1)
    for g, tab in enumerate(per_group(_bias_tables(dist, rb_ref, heads))):
        cbt_ref[g] = tab
    t = lax.broadcasted_iota(jnp.int32, (ls, TAIL), 0)
    c = lax.broadcasted_iota(jnp.int32, (ls, TAIL), 1) + (nk - TAIL)
    dist = jnp.where(c < past + ls, past + t - c, -1)
    for g, tab in enumerate(per_group(_bias_tables(dist, rb_ref, heads))):
        tail_ref[g] = tab
    t = lax.broadcasted_iota(jnp.int32, (ls, wkp), 0)
    c = lax.broadcasted_iota(jnp.int32, (ls, wkp), 1)
    dist = past + t - (past - wk + c)
    dist = jnp.where((c < wk + ls) & (dist < WINDOW), dist, -1)
    for g, tab in enumerate(per_group(_bias_tables(dist, rb_ref, heads))):
        wtab_ref[g] = tab
    kk = lax.broadcasted_iota(jnp.int32, (ncp, 2 * LANES), 0)
    ss = lax.broadcasted_iota(jnp.int32, (ncp, 2 * LANES), 1)
    ratio = SLC_BLOCK // CMP_STRIDE
    nover = (CMP_BLOCK - 1) // CMP_STRIDE
    ov = (kk >= ratio * ss - nover) & (kk < ratio * ss + ratio) & (kk < ncp - 1)
    ov_ref[...] = jnp.where(ov, 1.0, 0.0).astype(ov_ref.dtype)
    s2 = lax.broadcasted_iota(jnp.int32, (2 * LANES, nk), 0)
    c2 = lax.broadcasted_iota(jnp.int32, (2 * LANES, nk), 1)
    eall_ref[...] = jnp.where(c2 // SLC_BLOCK == s2, 1.0, 0.0).astype(eall_ref.dtype)


def _softmax_rows(l):
    m = jnp.max(l, axis=1, keepdims=True)
    e = jnp.exp(l - m)
    return e / jnp.sum(e, axis=1, keepdims=True)


def _t128(x):
    rows = x.shape[0]
    if rows < LANES:
        x = jnp.concatenate([x, jnp.zeros((LANES - rows, LANES), x.dtype)], axis=0)
    return x.T


def _nsa_sample_kernel(pt_ref, rb_ref, q_ref, gate_ref, kvs_ref, kvw_ref, kc_ref, vc_ref, win_ref, pool_ref,
                       o_ref, kt_ref, vt_ref, sem_ref, cbt_ref, tail_ref, wtab_ref, ov_ref, eall_ref,
                       *, ls, past, npages, page_size, ncp, wk, wkp, nk):
    b = pl.program_id(0)

    @pl.when(b == 0)
    def _():
        _nsa_sample_tables(rb_ref, cbt_ref, tail_ref, wtab_ref, ov_ref, eall_ref,
                           ls=ls, past=past, ncp=ncp, wk=wk, wkp=wkp, nk=nk)

    def page_copies(seq, slot, p):
        pg = pt_ref[seq * npages + p]
        cols = pl.ds(pl.multiple_of(p * page_size, page_size), page_size)
        return [pltpu.make_async_copy(pool_ref.at[pg, 0], kt_ref.at[slot, :, cols], sem_ref.at[0, slot]),
                pltpu.make_async_copy(pool_ref.at[pg, 1], vt_ref.at[slot, :, cols], sem_ref.at[1, slot])]

    slot = _gather_step(page_copies, npages)
    kvs = kvs_ref[...]
    kt_ref[slot, :, past:nk] = _t128(kvs[:, 0:KV_WIDTH])
    vt_ref[slot, :, past:nk] = _t128(kvs[:, KV_WIDTH:])
    kvw = kvw_ref[...]
    kwt = jnp.concatenate([win_ref[0, 0], _t128(kvw[:, 0:KV_WIDTH])], axis=1).astype(MXU)
    vwt = jnp.concatenate([win_ref[0, 1], _t128(kvw[:, KV_WIDTH:])], axis=1).astype(MXU)

    gw = NSA_GROUP * HEAD_DIM
    ng = NSA_KV_HEADS
    qall = jnp.concatenate([t for g in range(ng) for t in _place_heads(q_ref[:, g * gw:(g + 1) * gw], g)],
                           axis=0).astype(MXU)
    per_group = lambda ref: jnp.concatenate([ref[g] for g in range(ng)], axis=0)
    pc = _softmax_rows(_mm_nt(qall, kc_ref[0]) + per_group(cbt_ref)).astype(MXU)
    o_cmp = jnp.dot(pc, vc_ref[0].astype(MXU), preferred_element_type=F32)
    impm = jnp.dot(pc, ov_ref[...], preferred_element_type=F32)
    imp = jnp.concatenate([sum(impm[(g * NSA_GROUP + r) * ls:(g * NSA_GROUP + r + 1) * ls] for r in range(NSA_GROUP))
                           for g in range(ng)], axis=0)
    lane_s = lax.broadcasted_iota(jnp.int32, imp.shape, 1)
    qblk = (past + lax.broadcasted_iota(jnp.int32, imp.shape, 0) % ls) // SLC_BLOCK
    vals = jnp.where((lane_s == 0) | (lane_s == qblk) | (lane_s == qblk - 1), 3e38, imp)
    vals = jnp.where(lane_s <= qblk, vals, -1.0)
    lane_sf = lane_s.astype(F32)
    sel = jnp.zeros(vals.shape, F32)
    for _ in range(SLC_TOPN):
        mx = jnp.max(vals, axis=1, keepdims=True)
        idx = jnp.min(jnp.where(vals == mx, lane_sf, float(2 * LANES)), axis=1, keepdims=True)
        pick = lane_sf == idx
        sel = jnp.where(pick, 1.0, sel)
        vals = jnp.where(pick, -2.0, vals)
    selneg = ((sel - 1.0) * BIG).astype(MXU)
    selneg = jnp.concatenate([selneg[g * ls:(g + 1) * ls] for g in range(ng) for _ in range(NSA_GROUP)], axis=0)
    lsl = (jnp.dot(qall, kt_ref[slot].astype(MXU), preferred_element_type=F32)
           + jnp.dot(selneg, eall_ref[...], preferred_element_type=F32))
    lsl = jnp.concatenate([lsl[:, 0:nk - TAIL], lsl[:, nk - TAIL:] + per_group(tail_ref)], axis=1)
    o_slc = _mm_nt(_softmax_rows(lsl), vt_ref[slot])
    pw = _softmax_rows(jnp.dot(qall, kwt, preferred_element_type=F32) + per_group(wtab_ref))
    o_win = _mm_nt(pw, vwt)
    for g in range(ng):
        gates = gate_ref[:, g * LANES:(g + 1) * LANES]
        ohs = []
        for r in range(NSA_GROUP):
            rows = slice((g * NSA_GROUP + r) * ls, (g * NSA_GROUP + r + 1) * ls)
            ohs.append(_gate_mix(gates, r, o_cmp[rows], o_slc[rows], o_win[rows]))
        o_ref[:, g * gw:(g + 1) * gw] = _unplace_heads(ohs, g)


def _nsa_sample(rel_bias, q, gates, kvs, kvw, kc, vc, win_t, pool_t, page_table, ls):
    nseq, npages = page_table.shape
    page_size = pool_t.shape[3]
    past = npages * page_size
    ncp = kc.shape[1]
    wk = win_t.shape[3]
    wkp = wk + LANES
    nk = past + LANES
    assert past % SLC_BLOCK == 0 and ls % 8 == 0 and ls <= SLC_BLOCK and past >= wk and past // SLC_BLOCK + 1 <= 2 * LANES
    assert ncp % LANES == 0 and past >= CMP_BLOCK and page_size % LANES == 0 and wk % LANES == 0 and KV_WIDTH == LANES
    row = lambda width: pl.BlockSpec((ls, width), lambda b, pt: (b, 0))
    seq3 = lambda n, width: pl.BlockSpec((1, n, width), lambda b, pt: (b, 0, 0))
    kw = 2 * KV_WIDTH
    nrow = NSA_GROUP * ls
    return pl.pallas_call(
        functools.partial(_nsa_sample_kernel, ls=ls, past=past, npages=npages, page_size=page_size, ncp=ncp,
                          wk=wk, wkp=wkp, nk=nk),
        grid_spec=pltpu.PrefetchScalarGridSpec(
            num_scalar_prefetch=1,
            grid=(nseq,),
            in_specs=[pl.BlockSpec(memory_space=pltpu.SMEM), row(NSA_INNER), row(NSA_KV_HEADS * LANES), row(kw), row(kw),
                      seq3(ncp, LANES), seq3(ncp, LANES),
                      pl.BlockSpec((1, 2, KV_WIDTH, wk), lambda b, pt: (b, 0, 0, 0)), pl.BlockSpec(memory_space=pl.ANY)],
            out_specs=row(NSA_INNER),
            scratch_shapes=[pltpu.VMEM((2, KV_WIDTH, nk), F32), pltpu.VMEM((2, KV_WIDTH, nk), F32),
                            pltpu.SemaphoreType.DMA((2, 2)),
                            pltpu.VMEM((NSA_KV_HEADS, nrow, ncp), F32), pltpu.VMEM((NSA_KV_HEADS, nrow, TAIL), F32),
                            pltpu.VMEM((NSA_KV_HEADS, nrow, wkp), F32),
                            pltpu.VMEM((ncp, 2 * LANES), MXU), pltpu.VMEM((2 * LANES, nk), MXU)],
        ),
        out_shape=jax.ShapeDtypeStruct((nseq * ls, NSA_INNER), F32),
        compiler_params=_cparams("arbitrary"),
    )(page_table.reshape(-1), rel_bias, q, gates, kvs, kvw, kc, vc, win_t, pool_t)


def _mix_out_kernel(x_ref, yssd_ref, onsa_ref, gn_ref, wo_ref, gq_ref, wq_ref, mkv_ref, wmo_ref, o_ref):
    y = jnp.concatenate([yssd_ref[...], _rms(onsa_ref[...], gn_ref[...])], axis=1)
    x = x_ref[...] + jnp.dot(y.astype(MXU), wo_ref[...], preferred_element_type=F32)
    qm = jnp.dot(_rms(x, gq_ref[...]).astype(MXU), wq_ref[...], preferred_element_type=F32)
    kv = mkv_ref[0]
    km = kv[:, 0:MEM_INNER].astype(MXU)
    vm = kv[:, MEM_INNER:].astype(MXU)
    head = lax.broadcasted_iota(jnp.int32, qm.shape, 1) // HEAD_DIM
    o = jnp.zeros(qm.shape, F32)
    for h in range(MEM_HEADS):
        p = _softmax_rows(_mm_nt(jnp.where(head == h, qm, 0.0), km) * (HEAD_DIM ** -0.5))
        o = o + jnp.where(head == h, jnp.dot(p.astype(MXU), vm, preferred_element_type=F32), 0.0)
    o_ref[...] = x + jnp.dot(o.astype(MXU), wmo_ref[...], preferred_element_type=F32)


def _mix_out(x, yssd, onsa, g_nsa, w_out, g_memq, w_mem_q, mem_kv, w_mem_out, nseq, seq_len, tm=512):
    t, d = x.shape
    tm = min(tm, seq_len)
    nt = seq_len // tm
    assert nt * tm == seq_len
    m = mem_kv.shape[1]
    row = lambda width: pl.BlockSpec((tm, width), lambda b, i: (b * nt + i, 0))
    return pl.pallas_call(
        _mix_out_kernel,
        grid=(nseq, nt),
        in_specs=[row(d), row(SSD_INNER), row(NSA_INNER), _const_spec((1, NSA_INNER)), _const_spec(w_out.shape),
                  _const_spec((1, d)), _const_spec(w_mem_q.shape),
                  pl.BlockSpec((1, m, 2 * MEM_INNER), lambda b, i: (b, 0, 0)), _const_spec(w_mem_out.shape)],
        out_specs=row(d),
        out_shape=jax.ShapeDtypeStruct((t, d), F32),
        compiler_params=_cparams("parallel", "parallel"),
    )(x, yssd, onsa, g_nsa.reshape(1, -1), w_out.astype(MXU), g_memq.reshape(1, d), w_mem_q.astype(MXU),
      mem_kv, w_mem_out.astype(MXU))


def _norm_matmul_kernel(x_ref, g_ref, w_ref, o_ref):
    o_ref[...] = jnp.dot(_rms(x_ref[...], g_ref[...]).astype(MXU), w_ref[...], preferred_element_type=F32)


def _norm_matmul(x, g, w, tm=256):
    t, d = x.shape
    n = w.shape[1]
    tm = min(tm, t)
    assert t % tm == 0
    return pl.pallas_call(
        _norm_matmul_kernel,
        grid=(t // tm,),
        in_specs=[pl.BlockSpec((tm, d), lambda i: (i, 0)), _const_spec((1, d)), _const_spec(w.shape)],
        out_specs=pl.BlockSpec((tm, n), lambda i: (i, 0)),
        out_shape=jax.ShapeDtypeStruct((t, n), F32),
        compiler_params=_cparams("parallel"),
    )(x, g.reshape(1, d), w.astype(MXU))


def _trunk_layer(x, lw, cw, rel_bias, mem_kv, conv0, h0, nseq, seq_len, sample=None):
    x1 = _ffn(x, lw['norm_ffn1'], lw['w_ffn1_in'], lw['w_ffn1_out'])
    outs = _proj(x1, lw['norm_mix'], lw['w_mix'], seq_len=None if sample else seq_len)
    z, xbc, q, kvc, kvs, kvw, gates, dt = outs[:8]
    y_ssd, h_new, conv_new = _ssd(z, xbc, dt, conv0, h0, lw['conv_w'], lw['conv_b'], lw['dt_bias'], lw['a_log'],
                                  lw['d_skip'], lw['norm_ssd_out'], nseq, seq_len)
    rows5 = lambda a: a.reshape(nseq, seq_len, 2, NSA_KV_HEADS, HEAD_DIM)
    if sample is None:
        kc, vc = _compress_prompt(kvc, cw, nseq, seq_len)
        o_nsa = _nsa_prompt(rel_bias, q, gates, kc, vc, *outs[8:12], nseq, seq_len)
        from_t = lambda a: jnp.transpose(a.reshape(nseq, 2, NSA_KV_HEADS, HEAD_DIM, seq_len), (0, 4, 1, 2, 3))
        kv_c, kv_s = from_t(outs[12]), from_t(outs[13])
    else:
        kc, vc = _compress_sample(sample['pool_c'], sample['page_table'], cw)
        o_nsa = _nsa_sample(rel_bias, q, gates, kvs, kvw, kc, vc, sample['win_t'], sample['pool_s'],
                            sample['page_table'], seq_len)
        kv_c, kv_s = rows5(kvc), rows5(kvs)
    x2 = _mix_out(x1, y_ssd, o_nsa, lw['norm_nsa_out'], lw['w_out'], lw['norm_mem_q'], lw['w_mem_q'], mem_kv,
                  lw['w_mem_out'], nseq, seq_len)
    keep = min(WINDOW, seq_len)
    kv_w = kvw.reshape(nseq, seq_len, -1)[:, seq_len - keep:].reshape(nseq, keep, 2, NSA_KV_HEADS, HEAD_DIM)
    return x2, kv_c, kv_s, kv_w, h_new, conv_new


def kernel(x_prompt, x_sample, cache_kv_cmp, cache_kv_slc, state_win_kv, state_ssd, state_conv, cache_mem_kv, page_table, mem_prompt, norm_ffn1, w_ffn1_in, w_ffn1_out, norm_mix, w_in, conv_w, conv_b, dt_bias, a_log, d_skip, norm_ssd_out, cmp_pos_k, cmp_w1_k, cmp_w2_k, cmp_pos_v, cmp_w1_v, cmp_w2_v, norm_nsa_out, w_out, norm_mem_q, norm_mem_kv, w_mem_q, w_mem_kv, w_mem_out, norm_ffn2, w_ffn2_in, w_ffn2_out, rel_bias, norm_final):
    bp, lp, d = x_prompt.shape
    bs, lsq, _ = x_sample.shape
    depth = w_in.shape[0]
    xp = x_prompt.reshape(bp * lp, d)
    xs = x_sample.reshape(bs * lsq, d)
    outs_p = [[] for _ in range(6)]
    outs_s = [[] for _ in range(5)]
    for layer in range(depth):
        last = layer == depth - 1
        lw = {
            'norm_ffn1': norm_ffn1[layer], 'w_ffn1_in': w_ffn1_in[layer], 'w_ffn1_out': w_ffn1_out[layer],
            'norm_mix': norm_mix[layer], 'w_mix': _regroup_w_in(w_in[layer]), 'conv_w': conv_w[layer],
            'conv_b': conv_b[layer], 'dt_bias': dt_bias[layer], 'a_log': a_log[layer], 'd_skip': d_skip[layer],
            'norm_ssd_out': norm_ssd_out[layer], 'norm_nsa_out': norm_nsa_out[layer], 'w_out': w_out[layer],
            'norm_mem_q': norm_mem_q[layer], 'w_mem_q': w_mem_q[layer], 'w_mem_out': w_mem_out[layer],
        }
        cw = _compress_weights(cmp_pos_k[layer], cmp_w1_k[layer], cmp_w2_k[layer],
                               cmp_pos_v[layer], cmp_w1_v[layer], cmp_w2_v[layer])
        g_final = norm_final if last else None
        mem_kv_p = _norm_matmul(mem_prompt.reshape(-1, d), norm_mem_kv[layer], w_mem_kv[layer])
        mem_kv_p = mem_kv_p.reshape(bp, -1, 2 * MEM_INNER)
        conv0 = jnp.zeros((bp, SSD_CONV - 1, SSD_CONV_DIM), F32)
        h0 = jnp.zeros((bp, SSD_HEADS, HEAD_DIM, SSD_STATE), F32)
        xp, c_p, s_p, w_p, h_p, conv_p = _trunk_layer(xp, lw, cw, rel_bias, mem_kv_p, conv0, h0, bp, lp)
        xp = _ffn(xp, norm_ffn2[layer], w_ffn2_in[layer], w_ffn2_out[layer], g_final)
        sample = dict(pool_c=_paged_view(cache_kv_cmp[layer]), pool_s=_paged_view(cache_kv_slc[layer]),
                      win_t=_paged_view(state_win_kv[layer]), page_table=page_table)
        mem_kv_s = cache_mem_kv[layer].reshape(bs, -1, 2 * MEM_INNER)
        xs, c_s, s_s, w_s, h_s, conv_s = _trunk_layer(xs, lw, cw, rel_bias, mem_kv_s, state_conv[layer],
                                                      state_ssd[layer], bs, lsq, sample=sample)
        xs = _ffn(xs, norm_ffn2[layer], w_ffn2_in[layer], w_ffn2_out[layer], g_final)
        w_s = jnp.concatenate([state_win_kv[layer], w_s], axis=1)[:, w_s.shape[1]:]
        for lst, v in zip(outs_p, (c_p, s_p, w_p, h_p, conv_p, mem_kv_p.reshape(bp, -1, 2, MEM_HEADS, HEAD_DIM))):
            lst.append(v)
        for lst, v in zip(outs_s, (c_s, s_s, w_s, h_s, conv_s)):
            lst.append(v)
    return (xp.reshape(bp, lp, d), xs.reshape(bs, lsq, d),
            *(jnp.stack(o) for o in outs_p), *(jnp.stack(o) for o in outs_s))
```

```python
import functools
import math

import numpy as np
import jax
import jax.numpy as jnp
from jax import lax
from jax.experimental import pallas as pl
from jax.experimental.pallas import tpu as pltpu

F32 = jnp.float32
MXU = jnp.bfloat16

HEAD_DIM = 64
SSD_HEADS = 8
SSD_GROUPS = 2
SSD_STATE = 128
SSD_CONV = 4
SSD_CHUNK = 128
NSA_HEADS = 8
NSA_KV_HEADS = 2
NSA_GROUP = NSA_HEADS // NSA_KV_HEADS
CMP_BLOCK = 32
CMP_STRIDE = 16
CMP_HIDDEN = 256
SLC_BLOCK = 64
SLC_TOPN = 16
WINDOW = 512
REL_BUCKETS = 32
REL_MAX_DIST = 128
MEM_HEADS = 4
Q_BLOCK = 128
EPS = 1e-6
NEG = -1e30

SSD_INNER = SSD_HEADS * HEAD_DIM
NSA_INNER = NSA_HEADS * HEAD_DIM
KV_WIDTH = NSA_KV_HEADS * HEAD_DIM
SSD_CONV_DIM = SSD_INNER + 2 * SSD_GROUPS * SSD_STATE
MEM_INNER = MEM_HEADS * HEAD_DIM
LANES = 128
VMEM_LIMIT = 56 * 1024 * 1024


def _cparams(*sem):
    return pltpu.CompilerParams(dimension_semantics=sem, vmem_limit_bytes=VMEM_LIMIT)


def _const_spec(shape):
    nd = len(shape)
    return pl.BlockSpec(shape, lambda *a: (0,) * nd, pipeline_mode=pl.Buffered(1))


def _mm(a, b):
    return jnp.dot(a.astype(MXU), b.astype(MXU), preferred_element_type=F32)


def _mm_nt(a, b):
    return lax.dot_general(a.astype(MXU), b.astype(MXU), (((1,), (1,)), ((), ())), preferred_element_type=F32)


def _rms(x, g):
    return x * lax.rsqrt(jnp.mean(x * x, axis=-1, keepdims=True) + EPS) * g


def _silu(x):
    return x * jax.nn.sigmoid(x)


FFN_CHUNK = 256


def _ffn_kernel(x_ref, g_ref, win_ref, wout_ref, *rest, nchunks, final_norm):
    if final_norm:
        gf_ref, o_ref, h_ref = rest
    else:
        o_ref, h_ref = rest
    x = x_ref[...]
    h_ref[...] = _rms(x, g_ref[...]).astype(h_ref.dtype)
    f = nchunks * FFN_CHUNK
    for j in range(nchunks):
        cols = slice(j * FFN_CHUNK, (j + 1) * FFN_CHUNK)
        h = h_ref[...]
        gate = jnp.dot(h, win_ref[:, cols], preferred_element_type=F32)
        up = jnp.dot(h, win_ref[:, f + j * FFN_CHUNK:f + (j + 1) * FFN_CHUNK], preferred_element_type=F32)
        a = (_silu(gate) * up).astype(h.dtype)
        t = jnp.dot(a, wout_ref[cols, :], preferred_element_type=F32)
        if j == 0:
            o_ref[...] = t
        else:
            o_ref[...] += t
    y = x_ref[...] + 0.5 * o_ref[...]
    if final_norm:
        y = _rms(y, gf_ref[...])
    o_ref[...] = y


def _ffn(x, g, w_in, w_out, g_final=None, tm=512):
    t, d = x.shape
    f = w_out.shape[0]
    nchunks = f // FFN_CHUNK
    tm = min(tm, t)
    assert nchunks * FFN_CHUNK == f and t % tm == 0
    win = w_in.astype(MXU)
    wout = w_out.astype(MXU)
    final_norm = g_final is not None
    args = [x, g.reshape(1, d), win, wout]
    specs = [pl.BlockSpec((tm, d), lambda i: (i, 0)), _const_spec((1, d)),
             _const_spec(win.shape), _const_spec(wout.shape)]
    if final_norm:
        args.append(g_final.reshape(1, d))
        specs.append(_const_spec((1, d)))
    return pl.pallas_call(
        functools.partial(_ffn_kernel, nchunks=nchunks, final_norm=final_norm),
        grid=(t // tm,),
        in_specs=specs,
        out_specs=pl.BlockSpec((tm, d), lambda i: (i, 0)),
        out_shape=jax.ShapeDtypeStruct((t, d), F32),
        scratch_shapes=[pltpu.VMEM((tm, d), MXU)],
        compiler_params=_cparams("parallel"),
    )(*args)


PROJ_COLS = 3200
_O_Z, _O_XBC, _O_Q, _O_KVC, _O_KVS, _O_KVW, _O_G, _O_DT = 0, 512, 1536, 2048, 2304, 2560, 2816, 3072


def _regroup_w_in(w_in):
    o = np.cumsum([0, SSD_INNER, SSD_CONV_DIM, SSD_HEADS, NSA_INNER] + [KV_WIDTH] * 6 + [3 * NSA_HEADS])
    z, xbc, dt, q, kvs, gl = (w_in[:, o[0]:o[1]], w_in[:, o[1]:o[2]], w_in[:, o[2]:o[3]], w_in[:, o[3]:o[4]],
                              w_in[:, o[4]:o[10]], w_in[:, o[10]:o[11]])
    pad = lambda w: jnp.pad(w, ((0, 0), (0, LANES - w.shape[1])))
    ng = 3 * NSA_GROUP
    gls = [pad(gl[:, g * ng:(g + 1) * ng]) for g in range(NSA_KV_HEADS)]
    return jnp.concatenate([z, xbc, q, kvs] + gls + [pad(dt)], axis=1).astype(MXU)


def _proj_kernel(x_ref, g_ref, w_ref, z_ref, xbc_ref, q_ref, kvc_ref, kvs_ref, kvw_ref, gate_ref, dt_ref, *aug,
                 tiles_per_seq):
    h = _rms(x_ref[...], g_ref[...]).astype(w_ref.dtype)
    res = jnp.dot(h, w_ref[...], preferred_element_type=F32)
    tm = res.shape[0]
    z_ref[...] = res[:, _O_Z:_O_XBC]
    xbc_ref[...] = res[:, _O_XBC:_O_Q]
    q_ref[...] = res[:, _O_Q:_O_KVC] * (HEAD_DIM ** -0.5)
    kvc_ref[...] = res[:, _O_KVC:_O_KVS]
    kvs = res[:, _O_KVS:_O_KVW]
    kvw = res[:, _O_KVW:_O_G]
    kvs_ref[...] = kvs
    kvw_ref[...] = kvw
    gate_ref[...] = jax.nn.sigmoid(res[:, _O_G:_O_DT])
    dt_ref[...] = res[:, _O_DT:PROJ_COLS]
    if aug:
        ksa_ref, vsa_ref, kw_ref, vwa_ref, kvct_ref, kvst_ref = aug
        kvct_ref[0] = res[:, _O_KVC:_O_KVS].T
        kvst_ref[0] = kvs.T
        lane = lax.broadcasted_iota(jnp.int32, (tm, LANES), 1)
        row = lax.broadcasted_iota(jnp.int32, (tm, LANES), 0)
        pos = (pl.program_id(0) % tiles_per_seq) * tm + row
        onehot = jnp.where(lane == pos // SLC_BLOCK, 1.0, 0.0)
        ones_col = jnp.where(lane == 0, 1.0, 0.0)
        dt_ = ksa_ref.dtype
        ksa_ref[:, 0:LANES] = kvs[:, 0:LANES].astype(dt_)
        ksa_ref[:, LANES:] = onehot.astype(dt_)
        vsa_ref[:, 0:LANES] = kvs[:, LANES:].astype(dt_)
        vsa_ref[:, LANES:] = ones_col.astype(dt_)
        kw_ref[...] = kvw[:, 0:LANES].astype(dt_)
        vwa_ref[:, 0:LANES] = kvw[:, LANES:].astype(dt_)
        vwa_ref[:, LANES:] = ones_col.astype(dt_)


def _proj(x, g, w, seq_len=None, tm=512):
    t, d = x.shape
    tm = min(tm, t)
    assert t % tm == 0
    widths = [512, 1024, 512, 256, 256, 256, NSA_KV_HEADS * LANES, LANES]
    shapes = [jax.ShapeDtypeStruct((t, wd), F32) for wd in widths]
    specs = [pl.BlockSpec((tm, wd), lambda i: (i, 0)) for wd in widths]
    aug = seq_len is not None
    tps = seq_len // tm if aug else 1
    if aug:
        assert seq_len % tm == 0 and seq_len // SLC_BLOCK <= LANES
        for wd in (256, 256, LANES, 256):
            shapes.append(jax.ShapeDtypeStruct((t, wd), MXU))
            specs.append(pl.BlockSpec((tm, wd), lambda i: (i, 0)))
        for _ in range(2):
            shapes.append(jax.ShapeDtypeStruct((t // seq_len, 2 * KV_WIDTH, seq_len), F32))
            specs.append(pl.BlockSpec((1, 2 * KV_WIDTH, tm), lambda i: (i // tps, 0, i % tps)))
    return pl.pallas_call(
        functools.partial(_proj_kernel, tiles_per_seq=tps),
        grid=(t // tm,),
        in_specs=[pl.BlockSpec((tm, d), lambda i: (i, 0)), _const_spec((1, d)), _const_spec(w.shape)],
        out_specs=specs,
        out_shape=shapes,
        compiler_params=_cparams("parallel"),
    )(x, g.reshape(1, d), w)


HALO = 8


def _split3(x):
    hi = x.astype(jnp.bfloat16).astype(F32)
    r = x - hi
    mid = r.astype(jnp.bfloat16).astype(F32)
    return hi, mid, r - mid


def _ssd_kernel(z_ref, xbc_ref, dt_ref, conv0_ref, h0_ref, cw_ref, cb_ref, dtb_ref, alog_ref, dskip_ref, g_ref,
                y_ref, hout_ref, convout_ref, xwin_ref, h_ref, *, lv):
    q = SSD_CHUNK
    c = pl.program_id(1)

    @pl.when(c == 0)
    def _():
        xwin_ref[0:HALO, :] = conv0_ref[0]
        h_ref[...] = h0_ref[0]

    def padrows(v):
        if lv == q:
            return v
        return jnp.concatenate([v, jnp.zeros((q - lv, v.shape[1]), v.dtype)], axis=0)

    xwin_ref[HALO:HALO + q, :] = padrows(xbc_ref[...])
    acc = cb_ref[...] + cw_ref[SSD_CONV - 1:SSD_CONV, :] * xwin_ref[HALO:HALO + q, :]
    for k in range(SSD_CONV - 1):
        acc = acc + cw_ref[k:k + 1, :] * xwin_ref[pl.ds(HALO - (SSD_CONV - 1) + k, q), :]
    xc = _silu(acc)
    tail = xwin_ref[lv:lv + HALO, :]
    xwin_ref[0:HALO, :] = tail
    convout_ref[0] = tail

    xs = xc[:, 0:SSD_INNER]
    rowi = lax.broadcasted_iota(jnp.int32, (q, q), 0)
    coli = lax.broadcasted_iota(jnp.int32, (q, q), 1)
    causal = rowi >= coli
    tri = jnp.where(causal, 1.0, 0.0).astype(jnp.bfloat16)

    xdt = padrows(dt_ref[...]) + dtb_ref[...]
    dt = jnp.maximum(xdt, 0.0) + jnp.log1p(jnp.exp(-jnp.abs(xdt)))
    if lv != q:
        dt = jnp.where(rowi < lv, dt, 0.0)
    da = dt * (-jnp.exp(alog_ref[...]))
    acs = sum(jnp.dot(tri, p.astype(jnp.bfloat16), preferred_element_type=F32) for p in _split3(da))
    acs_t = acs.T
    dt_t = dt.T
    last = acs[q - 1:q, :]
    last_t = acs_t[:, q - 1:q]
    xs_t = [xs[:, p * LANES:(p + 1) * LANES].T for p in range(SSD_HEADS // 2)]
    lane_hi = lax.broadcasted_iota(jnp.int32, (q, LANES), 1) >= HEAD_DIM
    row_hi = lax.broadcasted_iota(jnp.int32, (LANES, q), 0) >= HEAD_DIM
    hpg = SSD_HEADS // SSD_GROUPS
    ys = []
    for g in range(SSD_GROUPS):
        bm = xc[:, SSD_INNER + g * SSD_STATE:SSD_INNER + (g + 1) * SSD_STATE]
        cm = xc[:, SSD_INNER + (SSD_GROUPS + g) * SSD_STATE:SSD_INNER + (SSD_GROUPS + g + 1) * SSD_STATE]
        cb = _mm_nt(cm, bm)
        for pp in range(hpg // 2):
            p = g * (hpg // 2) + pp
            xpair = xs[:, p * LANES:(p + 1) * LANES]
            yi = []
            for hh in range(2):
                h = 2 * p + hh
                seg = acs[:, h:h + 1] - acs_t[h:h + 1, :]
                decay = jnp.where(causal, jnp.exp(jnp.where(causal, seg, 0.0)), 0.0)
                yi.append(_mm(cb * decay * dt_t[h:h + 1, :], xpair))
            y_intra = jnp.where(lane_hi, yi[1], yi[0])
            h0, h1 = 2 * p, 2 * p + 1
            grow = jnp.where(lane_hi, jnp.exp(acs[:, h1:h1 + 1]), jnp.exp(acs[:, h0:h0 + 1]))
            y_inter = _mm_nt(cm, h_ref[p]) * grow
            ys.append(y_intra + y_inter)
            te = jnp.where(row_hi,
                           jnp.exp(last_t[h1:h1 + 1, :] - acs_t[h1:h1 + 1, :]) * dt_t[h1:h1 + 1, :],
                           jnp.exp(last_t[h0:h0 + 1, :] - acs_t[h0:h0 + 1, :]) * dt_t[h0:h0 + 1, :])
            states = _mm(xs_t[p] * te, bm)
            rdec = jnp.where(row_hi[:, 0:1], jnp.exp(last_t[h1:h1 + 1, :]), jnp.exp(last_t[h0:h0 + 1, :]))
            h_ref[p] = h_ref[p] * rdec + states
    y = jnp.concatenate(ys, axis=1) + dskip_ref[...] * xs
    y = _rms(y * _silu(padrows(z_ref[...])), g_ref[...])
    y_ref[...] = y[0:lv, :]
    hout_ref[0] = h_ref[...]


def _ssd(z, xbc, dt, conv0, h0, conv_w, conv_b, dt_bias, a_log, d_skip, norm_g, nseq, seq_len):
    q = SSD_CHUNK
    lv = min(q, seq_len)
    nch = seq_len // lv
    assert nch * lv == seq_len
    t = nseq * seq_len
    cdim = SSD_CONV_DIM
    keep = SSD_CONV - 1
    conv0p = jnp.pad(conv0, ((0, 0), (HALO - keep, 0), (0, 0)))
    padl = lambda v: jnp.pad(v.reshape(1, -1), ((0, 0), (0, LANES - v.size)))
    hp = h0.reshape(nseq, SSD_HEADS // 2, 2 * HEAD_DIM, SSD_STATE)
    row = lambda b, c: (b * nch + c, 0)
    y, hout, convout = pl.pallas_call(
        functools.partial(_ssd_kernel, lv=lv),
        grid=(nseq, nch),
        in_specs=[pl.BlockSpec((lv, SSD_INNER), row), pl.BlockSpec((lv, cdim), row), pl.BlockSpec((lv, LANES), row),
                  pl.BlockSpec((1, HALO, cdim), lambda b, c: (b, 0, 0)),
                  pl.BlockSpec((1,) + hp.shape[1:], lambda b, c: (b, 0, 0, 0)),
                  _const_spec((SSD_CONV, cdim)), _const_spec((1, cdim)), _const_spec((1, LANES)),
                  _const_spec((1, LANES)), _const_spec((1, SSD_INNER)), _const_spec((1, SSD_INNER))],
        out_specs=[pl.BlockSpec((lv, SSD_INNER), row),
                   pl.BlockSpec((1,) + hp.shape[1:], lambda b, c: (b, 0, 0, 0)),
                   pl.BlockSpec((1, HALO, cdim), lambda b, c: (b, 0, 0))],
        out_shape=[jax.ShapeDtypeStruct((t, SSD_INNER), F32), jax.ShapeDtypeStruct(hp.shape, F32),
                   jax.ShapeDtypeStruct((nseq, HALO, cdim), F32)],
        scratch_shapes=[pltpu.VMEM((HALO + q, cdim), F32), pltpu.VMEM(hp.shape[1:], F32)],
        compiler_params=_cparams("parallel", "arbitrary"),
    )(z, xbc, dt, conv0p, hp, conv_w, conv_b.reshape(1, cdim), padl(dt_bias), padl(a_log),
      jnp.repeat(d_skip, HEAD_DIM).reshape(1, SSD_INNER), norm_g.reshape(1, SSD_INNER))
    return y, hout.reshape(nseq, SSD_HEADS, HEAD_DIM, SSD_STATE), convout[:, HALO - keep:, :]


def _compress_weights(pe_k, w1_k, w2_k, pe_v, w1_v, w2_v):
    def w1_pair(w1):
        w = w1.reshape(CMP_BLOCK, HEAD_DIM, CMP_HIDDEN)
        zz = jnp.zeros_like(w)
        w = jnp.concatenate([jnp.concatenate([w, zz], axis=2), jnp.concatenate([zz, w], axis=2)], axis=1)
        return w.reshape(CMP_BLOCK // 2, 2 * KV_WIDTH, NSA_KV_HEADS * CMP_HIDDEN).astype(MXU)

    def w2_pair(w2):
        zz = jnp.zeros_like(w2)
        return jnp.concatenate([jnp.concatenate([w2, zz], axis=1), jnp.concatenate([zz, w2], axis=1)], axis=0).astype(MXU)

    pe = jnp.concatenate([pe_k, pe_k, pe_v, pe_v], axis=1)
    return pe, w1_pair(w1_k), w1_pair(w1_v), w2_pair(w2_k), w2_pair(w2_v)


def _compress_compute(row_of_chunks, pe_ref, wk_ref, wv_ref, w2k_ref, w2v_ref, sh_ref, kc_ref, vc_ref, nc):
    half = CMP_STRIDE
    a0k = a1k = a0v = a1v = None
    for s2 in range(half // 2):
        rows = [row_of_chunks(2 * s2 + u) for u in range(2)]

        def lhs(j, kv):
            lanes = slice(kv * LANES, (kv + 1) * LANES)
            return jnp.concatenate([(rows[u][kv] + pe_ref[j * half + 2 * s2 + u:j * half + 2 * s2 + u + 1, lanes])
                                    for u in range(2)], axis=1).astype(MXU)

        t0k = jnp.dot(lhs(0, 0), wk_ref[s2], preferred_element_type=F32)
        t0v = jnp.dot(lhs(0, 1), wv_ref[s2], preferred_element_type=F32)
        t1k = jnp.dot(lhs(1, 0), wk_ref[half // 2 + s2], preferred_element_type=F32)
        t1v = jnp.dot(lhs(1, 1), wv_ref[half // 2 + s2], preferred_element_type=F32)
        a0k, a1k = (t0k, t1k) if s2 == 0 else (a0k + t0k, a1k + t1k)
        a0v, a1v = (t0v, t1v) if s2 == 0 else (a0v + t0v, a1v + t1v)
    sh_ref[nc:nc + 8, :] = jnp.zeros((8, sh_ref.shape[1]), F32)
    for a0, a1, w2_ref, o_ref in ((a0k, a1k, w2k_ref, kc_ref), (a0v, a1v, w2v_ref, vc_ref)):
        sh_ref[0:nc, :] = a1
        hid = a0 + sh_ref[pl.ds(1, nc), :]
        o_ref[...] = jnp.dot(_silu(hid).astype(MXU), w2_ref[...], preferred_element_type=F32)


def _compress_prompt_kernel(rows_ref, pe_ref, wk_ref, wv_ref, w2k_ref, w2v_ref, kc_ref, vc_ref, sh_ref, *, nc):
    rw = 2 * KV_WIDTH

    def row_of_chunks(s):
        return rows_ref[:, s * rw:s * rw + LANES], rows_ref[:, s * rw + LANES:(s + 1) * rw]

    _compress_compute(row_of_chunks, pe_ref, wk_ref, wv_ref, w2k_ref, w2v_ref, sh_ref, kc_ref.at[0], vc_ref.at[0], nc)


def _compress_specs(cw):
    return [_const_spec(w.shape) for w in cw]


def _compress_prompt(kv_cmp, cw, nseq, seq_len):
    nc = seq_len // CMP_STRIDE
    out = jax.ShapeDtypeStruct((nseq, nc, LANES), F32)
    return pl.pallas_call(
        functools.partial(_compress_prompt_kernel, nc=nc),
        grid=(nseq,),
        in_specs=[pl.BlockSpec((nc, CMP_STRIDE * 2 * KV_WIDTH), lambda b: (b, 0))] + _compress_specs(cw),
        out_specs=[pl.BlockSpec((1, nc, LANES), lambda b: (b, 0, 0))] * 2,
        out_shape=[out, out],
        scratch_shapes=[pltpu.VMEM((nc + 8, NSA_KV_HEADS * CMP_HIDDEN), F32)],
        compiler_params=_cparams("parallel"),
    )(kv_cmp.reshape(nseq * nc, CMP_STRIDE * 2 * KV_WIDTH), *cw)


PAGE_UNROLL = 8


def _gather_step(page_copies, npages):
    b = pl.program_id(0)
    nb = pl.num_programs(0)
    slot = b % 2
    assert npages % PAGE_UNROLL == 0

    def for_pages(fn):
        def body(q, c):
            for u in range(PAGE_UNROLL):
                fn(q * PAGE_UNROLL + u)
            return c
        lax.fori_loop(0, npages // PAGE_UNROLL, body, 0)

    def start_all(seq, sl):
        for_pages(lambda p: [cp.start() for cp in page_copies(seq, sl, p)])

    @pl.when(b == 0)
    def _():
        start_all(0, 0)

    @pl.when(b + 1 < nb)
    def _():
        start_all(b + 1, 1 - slot)

    for_pages(lambda p: [cp.wait() for cp in page_copies(b, slot, p)])
    return slot


def _paged_view(cache):
    n_pool, page = cache.shape[:2]
    return jnp.transpose(cache, (0, 2, 3, 4, 1)).reshape(n_pool, 2, KV_WIDTH, page)


def _compress_sample_kernel(pt_ref, pool_ref, pe_ref, wk_ref, wv_ref, w2k_ref, w2v_ref, kc_ref, vc_ref,
                            buf_ref, sem_ref, rk_ref, rv_ref, sh_ref, *, nc, npages, page_size):
    def page_copies(seq, slot, p):
        return [pltpu.make_async_copy(pool_ref.at[pt_ref[seq * npages + p]], buf_ref.at[slot, p], sem_ref.at[slot])]

    slot = _gather_step(page_copies, npages)

    def to_rows(q, c):
        for u in range(PAGE_UNROLL):
            p = q * PAGE_UNROLL + u
            rows = pl.ds(pl.multiple_of(p * page_size, page_size), page_size)
            rk_ref[rows, :] = buf_ref[slot, p, 0].T
            rv_ref[rows, :] = buf_ref[slot, p, 1].T
        return c

    lax.fori_loop(0, npages // PAGE_UNROLL, to_rows, 0)

    def row_of_chunks(s):
        return rk_ref[pl.ds(s, nc, stride=CMP_STRIDE), :], rv_ref[pl.ds(s, nc, stride=CMP_STRIDE), :]

    _compress_compute(row_of_chunks, pe_ref, wk_ref, wv_ref, w2k_ref, w2v_ref, sh_ref, kc_ref.at[0], vc_ref.at[0], nc)


def _compress_sample(pool_t, page_table, cw):
    nseq, npages = page_table.shape
    page_size = pool_t.shape[3]
    past = npages * page_size
    nc = past // CMP_STRIDE
    assert page_size == LANES and KV_WIDTH == LANES
    out = jax.ShapeDtypeStruct((nseq, nc, LANES), F32)
    cspecs = [pl.BlockSpec(w.shape, lambda b, pt, nd=w.ndim: (0,) * nd, pipeline_mode=pl.Buffered(1)) for w in cw]
    return pl.pallas_call(
        functools.partial(_compress_sample_kernel, nc=nc, npages=npages, page_size=page_size),
        grid_spec=pltpu.PrefetchScalarGridSpec(
            num_scalar_prefetch=1,
            grid=(nseq,),
            in_specs=[pl.BlockSpec(memory_space=pl.ANY)] + cspecs,
            out_specs=[pl.BlockSpec((1, nc, LANES), lambda b, pt: (b, 0, 0))] * 2,
            scratch_shapes=[pltpu.VMEM((2, npages, 2, KV_WIDTH, page_size), F32), pltpu.SemaphoreType.DMA((2,)),
                            pltpu.VMEM((past, KV_WIDTH), F32), pltpu.VMEM((past, KV_WIDTH), F32),
                            pltpu.VMEM((nc + 8, NSA_KV_HEADS * CMP_HIDDEN), F32)],
        ),
        out_shape=[out, out],
        compiler_params=_cparams("arbitrary"),
    )(page_table.reshape(-1), pool_t, *cw)


def _bucket_thresholds():
    n = np.arange(0, 2 * REL_MAX_DIST)
    exact = REL_BUCKETS // 2
    out = []
    for ft in (np.float32, np.float64):
        nf = np.maximum(n, exact).astype(ft)
        large = exact + (np.log(nf / ft(exact)) / ft(math.log(REL_MAX_DIST / exact)) * ft(REL_BUCKETS - exact)).astype(np.int64)
        out.append(np.where(n < exact, n, np.minimum(large, REL_BUCKETS - 1)))
    assert (out[0] == out[1]).all() and out[1][REL_MAX_DIST] == REL_BUCKETS - 1
    return [int(np.argmax(out[1] >= b)) for b in range(1, REL_BUCKETS)]


_THR = _bucket_thresholds()


def _bias_tables(dist, rb_ref, heads):
    steps = [jnp.where(dist >= _THR[b - 1], 1.0, 0.0) for b in range(1, REL_BUCKETS)]
    far = dist >= REL_MAX_DIST
    out = []
    for h in heads:
        acc = jnp.full(dist.shape, rb_ref[0, h] - rb_ref[REL_BUCKETS - 1, h], F32)
        for b in range(1, REL_BUCKETS):
            acc = acc + steps[b - 1] * (rb_ref[b, h] - rb_ref[b - 1, h])
        out.append(jnp.where(dist < 0, NEG, jnp.where(far, 0.0, acc)))
    return out


BIG = 1e30
KT = 256
SLC_UNROLL = 8
N_BAND = 20
BAND_LO = 12
F_STEP = 2 * N_BAND
N_WK = 6
N_SK = 4
WIN_TILES = WINDOW // Q_BLOCK + 1


def _nsa_tables(rb_ref, st_ref, wt_ref, qft_ref, ov_ref, ncp):
    tq = Q_BLOCK
    heads = range(NSA_HEADS)
    t2 = lax.broadcasted_iota(jnp.int32, (tq, KT), 0)
    c2 = lax.broadcasted_iota(jnp.int32, (tq, KT), 1)
    for dd in range(N_SK - 1):
        tabs = _bias_tables(tq * dd + t2 - c2, rb_ref, heads)
        for h in heads:
            st_ref[h * N_SK + dd] = tabs[h]
    t1 = lax.broadcasted_iota(jnp.int32, (tq, LANES), 0)
    c1 = lax.broadcasted_iota(jnp.int32, (tq, LANES), 1)
    zeros = jnp.zeros((tq, LANES), F32)
    for d in range(2):
        tabs = _bias_tables(tq * d + t1 - c1, rb_ref, heads)
        for h in heads:
            wt_ref[h * N_WK + d] = tabs[h]
    for h in heads:
        st_ref[h * N_SK + N_SK - 1] = jnp.zeros((tq, KT), F32)
        wt_ref[h * N_WK + 2] = zeros
        wt_ref[h * N_WK + 3] = zeros
        wt_ref[h * N_WK + 4] = jnp.where(c1 > t1, 0.0, NEG)
        wt_ref[h * N_WK + 5] = zeros + NEG
    for g in range(NSA_KV_HEADS):
        f = c1 - (HEAD_DIM if g == 0 else 0)
        is_lo = f >= N_BAND
        u = jnp.where(is_lo, f - N_BAND, f) - BAND_LO
        live = (f >= 0) & (f < 2 * N_BAND)
        tabs = _bias_tables(t1 - CMP_STRIDE * u - (CMP_BLOCK - 1), rb_ref, range(g * NSA_GROUP, (g + 1) * NSA_GROUP))
        for r in range(NSA_GROUP):
            hi = tabs[r].astype(jnp.bfloat16).astype(F32)
            lo = jnp.where(tabs[r] <= NEG, 0.0, tabs[r] - hi)
            val = jnp.where(live, jnp.where(is_lo, lo, hi), 0.0)
            qft_ref[g * NSA_GROUP + r] = jnp.where(f == F_STEP, NEG, val)
    kk = lax.broadcasted_iota(jnp.int32, (ncp, LANES), 0)
    ss = lax.broadcasted_iota(jnp.int32, (ncp, LANES), 1)
    ratio = SLC_BLOCK // CMP_STRIDE
    nover = (CMP_BLOCK - 1) // CMP_STRIDE
    ov = (kk >= ratio * ss - nover) & (kk < ratio * ss + ratio) & (kk < ncp - 1)
    ov_ref[...] = jnp.where(ov, 1.0, 0.0).astype(ov_ref.dtype)


def _lane_half(rows):
    return lax.broadcasted_iota(jnp.int32, (rows, LANES), 1) // HEAD_DIM


def _place_heads(qg, g):
    keep = _lane_half(qg.shape[0]) == g
    out = []
    for r in range(NSA_GROUP):
        v = qg[:, (r // 2) * LANES:(r // 2 + 1) * LANES]
        v = jnp.where(g == r % 2, v, pltpu.roll(v, HEAD_DIM, 1))
        out.append(jnp.where(keep, v, 0.0))
    return out


def _unplace_heads(ohs, g):
    keep = _lane_half(ohs[0].shape[0]) == g
    cols = []
    for c in range(NSA_GROUP // 2):
        pair = []
        for r in (2 * c, 2 * c + 1):
            oh = jnp.where(keep, ohs[r], 0.0)
            pair.append(jnp.where(g == r % 2, oh, pltpu.roll(oh, HEAD_DIM, 1)))
        cols.append(pair[0] + pair[1])
    return jnp.concatenate(cols, axis=1)


def _gate_mix(gates, r, o_cmp, o_slc, o_win, l_slc=None, l_win=None):
    c_slc = gates[:, 3 * r + 1:3 * r + 2]
    c_win = gates[:, 3 * r + 2:3 * r + 3]
    if l_slc is not None:
        c_slc = c_slc / l_slc
    if l_win is not None:
        c_win = c_win / l_win
    return gates[:, 3 * r:3 * r + 1] * o_cmp + c_slc * o_slc + c_win * o_win


def _top_blocks(imp, qpos_lane):
    ns = imp.shape[1]
    vals = jnp.concatenate([imp[:, c * LANES:(c + 1) * LANES].T for c in range(ns // LANES)], axis=0)
    srow = lax.broadcasted_iota(jnp.int32, vals.shape, 0)
    qblk = qpos_lane[0:1, :] // SLC_BLOCK
    forced = (srow == 0) | (srow == qblk) | (srow == qblk - 1)
    vals = jnp.where(forced, -2.0, vals)
    vals = jnp.where(srow <= qblk, vals, -1.0)
    srow_f = srow.astype(F32)
    sel = jnp.where(forced, 1.0, 0.0)
    for _ in range(SLC_TOPN - 3):
        mx = jnp.max(vals, axis=0, keepdims=True)
        idx = jnp.min(jnp.where(vals == mx, srow_f, float(ns)), axis=0, keepdims=True)
        pick = srow_f == idx
        sel = jnp.where(pick, 1.0, sel)
        vals = jnp.where(pick, -2.0, vals)
    sel = jnp.where(srow <= qblk, sel, 0.0)
    return jnp.concatenate([sel[c * LANES:(c + 1) * LANES].T for c in range(ns // LANES)], axis=1)


def _nsa_prompt_kernel(rb_ref, q_ref, gate_ref, kc_ref, vc_ref, ksa_ref, vsa_ref, kw_ref, vwa_ref, o_ref,
                       st_ref, wt_ref, qft_ref, ov_ref, lbuf_ref, m_ref, acc_ref, qs_ref, *, ncp):
    tq = Q_BLOCK
    i = pl.program_id(1)
    gw = NSA_GROUP * HEAD_DIM

    @pl.when((pl.program_id(0) == 0) & (i == 0))
    def _():
        _nsa_tables(rb_ref, st_ref, wt_ref, qft_ref, ov_ref, ncp)

    qpos_lane = i * tq + lax.broadcasted_iota(jnp.int32, (tq, LANES), 1)
    trow = lax.broadcasted_iota(jnp.int32, (NSA_GROUP * tq, 1), 0) % tq
    row_live = (i * tq + trow) >= CMP_BLOCK - 1
    kidx = lax.broadcasted_iota(jnp.int32, (ncp, LANES), 0)
    klane = lax.broadcasted_iota(jnp.int32, (ncp, LANES), 1)
    ngrp = i // (2 * SLC_UNROLL) + 1
    nplain = jnp.maximum(ngrp - 2, 0)
    j0 = jnp.maximum(i - (WIN_TILES - 1), 0)
    ws = pl.multiple_of(j0 * tq, tq)
    kinds = []
    for jj in range(WIN_TILES):
        d = i - j0 - jj
        kinds.append(jnp.where(d < 0, N_WK - 1, d))
    u12 = kidx - (tq // CMP_STRIDE) * i + BAND_LO

    o_cmp, accw = [], []
    for g in range(NSA_KV_HEADS):
        qp = _place_heads(q_ref[:, g * gw:(g + 1) * gw], g)
        heads = [g * NSA_GROUP + r for r in range(NSA_GROUP)]
        lw = _mm_nt(jnp.concatenate(qp, axis=0).astype(MXU), kw_ref[pl.ds(ws, WIN_TILES * tq), :])
        lw = lw + jnp.concatenate(
            [jnp.concatenate([wt_ref[h * N_WK + kinds[jj]] for jj in range(WIN_TILES)], axis=1) for h in heads], axis=0)
        ew = jnp.exp(lw - jnp.max(lw, axis=1, keepdims=True)).astype(MXU)
        accw.append(jnp.dot(ew, vwa_ref[pl.ds(ws, WIN_TILES * tq), :], preferred_element_type=F32))
        f = klane - HEAD_DIM * (1 - g)
        fa = jnp.where(f >= N_BAND, f - N_BAND, f)
        feat = ((fa == u12) & (f >= 0) & (f < 2 * N_BAND)) | ((f == F_STEP) & (u12 >= N_BAND - 1))
        ka = jnp.where(klane // HEAD_DIM == g, kc_ref[0], jnp.where(feat, 1.0, 0.0)).astype(MXU)
        qa = jnp.concatenate([qp[r] + qft_ref[heads[r]] for r in range(NSA_GROUP)], axis=0).astype(MXU)
        lc = _mm_nt(qa, ka)
        ec = jnp.exp(lc - jnp.max(lc, axis=1, keepdims=True))
        sc = jnp.sum(ec, axis=1, keepdims=True)
        pc = (ec * jnp.where(row_live, 1.0 / sc, 0.0)).astype(MXU)
        o_cmp.append(jnp.dot(pc, vc_ref[0].astype(MXU), preferred_element_type=F32))
        imp = sum(jnp.dot(pc[r * tq:(r + 1) * tq], ov_ref[...], preferred_element_type=F32) for r in range(NSA_GROUP))
        selneg = (_top_blocks(imp, qpos_lane) - 1.0) * BIG
        qs_ref[g] = jnp.concatenate([jnp.concatenate([qp[r], selneg], axis=1) for r in range(NSA_GROUP)],
                                    axis=0).astype(qs_ref.dtype)

    gk = SLC_UNROLL * KT

    def group_body(g, carry):
        m_ref[...] = jnp.full(m_ref.shape, NEG, F32)

        def stage(grp, lt):
            lbuf_ref[grp] = lt
            mg = lt[:, 0:LANES]
            for c in range(1, gk // LANES):
                mg = jnp.maximum(mg, lt[:, c * LANES:(c + 1) * LANES])
            m_ref[...] = jnp.maximum(m_ref[...], mg)

        def plain_group(grp, c):
            stage(grp, _mm_nt(qs_ref[g], ksa_ref[pl.ds(pl.multiple_of(grp * gk, gk), gk), :]))
            return c

        def near_group(grp, c):
            tabs = []
            for r in range(NSA_GROUP):
                row = []
                for u in range(SLC_UNROLL):
                    dd = i - 2 * (grp * SLC_UNROLL + u)
                    kind = jnp.where((dd < 0) | (dd >= N_SK), N_SK - 1, dd)
                    row.append(st_ref[(g * NSA_GROUP + r) * N_SK + kind])
                tabs.append(jnp.concatenate(row, axis=1))
            stage(grp, _mm_nt(qs_ref[g], ksa_ref[pl.ds(pl.multiple_of(grp * gk, gk), gk), :])
                  + jnp.concatenate(tabs, axis=0))
            return c

        lax.fori_loop(0, nplain, plain_group, 0)
        lax.fori_loop(nplain, ngrp, near_group, 0)
        m_ref[...] = jnp.broadcast_to(jnp.max(m_ref[...], axis=1, keepdims=True), m_ref.shape)
        acc_ref[g] = jnp.zeros(acc_ref.shape[1:], F32)

        def pv_group(grp, c):
            e = jnp.exp(lbuf_ref[grp] - jnp.concatenate([m_ref[...]] * (gk // LANES), axis=1)).astype(MXU)
            acc_ref[g] += jnp.dot(e, vsa_ref[pl.ds(pl.multiple_of(grp * gk, gk), gk), :], preferred_element_type=F32)
            return c

        lax.fori_loop(0, ngrp, pv_group, 0)
        return carry

    lax.fori_loop(0, NSA_KV_HEADS, group_body, 0)
    for g in range(NSA_KV_HEADS):
        gates = gate_ref[:, g * LANES:(g + 1) * LANES]
        ohs = []
        for r in range(NSA_GROUP):
            rows = slice(r * tq, (r + 1) * tq)
            ohs.append(_gate_mix(gates, r, o_cmp[g][rows], acc_ref[g, rows, 0:LANES], accw[g][rows, 0:LANES],
                                 acc_ref[g, rows, LANES:LANES + 1], accw[g][rows, LANES:LANES + 1]))
        o_ref[:, g * gw:(g + 1) * gw] = _unplace_heads(ohs, g)


def _nsa_prompt(rel_bias, q, gates, kc, vc, ksa, vsa, kw, vwa, nseq, seq_len):
    tq = Q_BLOCK
    nq = seq_len // tq
    ncp = kc.shape[1]
    gk = SLC_UNROLL * KT
    ng = NSA_KV_HEADS
    rows = NSA_GROUP * tq
    assert seq_len % gk == 0 and seq_len >= WIN_TILES * tq and ncp % LANES == 0
    per_seq = lambda width: pl.BlockSpec((seq_len, width), lambda b, i: (b, 0), pipeline_mode=pl.Buffered(1))
    blk = lambda width: pl.BlockSpec((tq, width), lambda b, i: (b * nq + i, 0))
    cmp_spec = pl.BlockSpec((1, ncp, LANES), lambda b, i: (b, 0, 0))
    return pl.pallas_call(
        functools.partial(_nsa_prompt_kernel, ncp=ncp),
        grid=(nseq, nq),
        in_specs=[pl.BlockSpec(memory_space=pltpu.SMEM), blk(NSA_INNER), blk(ng * LANES), cmp_spec, cmp_spec,
                  per_seq(2 * LANES), per_seq(2 * LANES), per_seq(LANES), per_seq(2 * LANES)],
        out_specs=blk(NSA_INNER),
        out_shape=jax.ShapeDtypeStruct((nseq * seq_len, NSA_INNER), F32),
        scratch_shapes=[pltpu.VMEM((NSA_HEADS * N_SK, tq, KT), F32), pltpu.VMEM((NSA_HEADS * N_WK, tq, LANES), F32),
                        pltpu.VMEM((NSA_HEADS, tq, LANES), F32), pltpu.VMEM((ncp, LANES), MXU),
                        pltpu.VMEM((seq_len // gk, rows, gk), F32), pltpu.VMEM((rows, LANES), F32),
                        pltpu.VMEM((ng, rows, 2 * LANES), F32), pltpu.VMEM((ng, rows, 2 * LANES), MXU)],
        compiler_params=_cparams("arbitrary", "arbitrary"),
    )(rel_bias, q, gates, kc, vc, ksa, vsa, kw, vwa)


TAIL = 2 * LANES


def _nsa_sample_tables(rb_ref, cbt_ref, tail_ref, wtab_ref, ov_ref, eall_ref, *, ls, past, ncp, wk, wkp, nk):
    heads = range(NSA_HEADS)

    def per_group(tabs):
        return [jnp.concatenate(tabs[g * NSA_GROUP:(g + 1) * NSA_GROUP], axis=0) for g in range(NSA_KV_HEADS)]

    t = lax.broadcasted_iota(jnp.int32, (ls, ncp), 0)
    k = lax.broadcasted_iota(jnp.int32, (ls, ncp), 1)
    dist = past + t - CMP_STRIDE * k - (CMP_BLOCK - 1)
    dist = jnp.where(k < ncp - 1, dist, -1)
    for g, tab in enumerate(per_group(_bias_tables(dist, rb_ref, heads))):
        cbt_ref[g] = tab
    t = lax.broadcasted_iota(jnp.int32, (ls, TAIL), 0)
    c = lax.broadcasted_iota(jnp.int32, (ls, TAIL), 1) + (nk - TAIL)
    dist = jnp.where(c < past + ls, past + t - c, -1)
    for g, tab in enumerate(per_group(_bias_tables(dist, rb_ref, heads))):
        tail_ref[g] = tab
    t = lax.broadcasted_iota(jnp.int32, (ls, wkp), 0)
    c = lax.broadcasted_iota(jnp.int32, (ls, wkp), 1)
    dist = past + t - (past - wk + c)
    dist = jnp.where((c < wk + ls) & (dist < WINDOW), dist, -1)
    for g, tab in enumerate(per_group(_bias_tables(dist, rb_ref, heads))):
        wtab_ref[g] = tab
    kk = lax.broadcasted_iota(jnp.int32, (ncp, 2 * LANES), 0)
    ss = lax.broadcasted_iota(jnp.int32, (ncp, 2 * LANES), 1)
    ratio = SLC_BLOCK // CMP_STRIDE
    nover = (CMP_BLOCK - 1) // CMP_STRIDE
    ov = (kk >= ratio * ss - nover) & (kk < ratio * ss + ratio) & (kk < ncp - 1)
    ov_ref[...] = jnp.where(ov, 1.0, 0.0).astype(ov_ref.dtype)
    s2 = lax.broadcasted_iota(jnp.int32, (2 * LANES, nk), 0)
    c2 = lax.broadcasted_iota(jnp.int32, (2 * LANES, nk), 1)
    eall_ref[...] = jnp.where(c2 // SLC_BLOCK == s2, 1.0, 0.0).astype(eall_ref.dtype)


def _softmax_rows(l):
    m = jnp.max(l, axis=1, keepdims=True)
    e = jnp.exp(l - m)
    return e / jnp.sum(e, axis=1, keepdims=True)


def _t128(x):
    rows = x.shape[0]
    if rows < LANES:
        x = jnp.concatenate([x, jnp.zeros((LANES - rows, LANES), x.dtype)], axis=0)
    return x.T


def _nsa_sample_kernel(pt_ref, rb_ref, q_ref, gate_ref, kvs_ref, kvw_ref, kc_ref, vc_ref, win_ref, pool_ref,
                       o_ref, kt_ref, vt_ref, sem_ref, cbt_ref, tail_ref, wtab_ref, ov_ref, eall_ref,
                       *, ls, past, npages, page_size, ncp, wk, wkp, nk):
    b = pl.program_id(0)

    @pl.when(b == 0)
    def _():
        _nsa_sample_tables(rb_ref, cbt_ref, tail_ref, wtab_ref, ov_ref, eall_ref,
                           ls=ls, past=past, ncp=ncp, wk=wk, wkp=wkp, nk=nk)

    def page_copies(seq, slot, p):
        pg = pt_ref[seq * npages + p]
        cols = pl.ds(pl.multiple_of(p * page_size, page_size), page_size)
        return [pltpu.make_async_copy(pool_ref.at[pg, 0], kt_ref.at[slot, :, cols], sem_ref.at[0, slot]),
                pltpu.make_async_copy(pool_ref.at[pg, 1], vt_ref.at[slot, :, cols], sem_ref.at[1, slot])]

    slot = _gather_step(page_copies, npages)
    kvs = kvs_ref[...]
    kt_ref[slot, :, past:nk] = _t128(kvs[:, 0:KV_WIDTH])
    vt_ref[slot, :, past:nk] = _t128(kvs[:, KV_WIDTH:])
    kvw = kvw_ref[...]
    kwt = jnp.concatenate([win_ref[0, 0], _t128(kvw[:, 0:KV_WIDTH])], axis=1).astype(MXU)
    vwt = jnp.concatenate([win_ref[0, 1], _t128(kvw[:, KV_WIDTH:])], axis=1).astype(MXU)

    gw = NSA_GROUP * HEAD_DIM
    ng = NSA_KV_HEADS
    qall = jnp.concatenate([t for g in range(ng) for t in _place_heads(q_ref[:, g * gw:(g + 1) * gw], g)],
                           axis=0).astype(MXU)
    per_group = lambda ref: jnp.concatenate([ref[g] for g in range(ng)], axis=0)
    pc = _softmax_rows(_mm_nt(qall, kc_ref[0]) + per_group(cbt_ref)).astype(MXU)
    o_cmp = jnp.dot(pc, vc_ref[0].astype(MXU), preferred_element_type=F32)
    impm = jnp.dot(pc, ov_ref[...], preferred_element_type=F32)
    imp = jnp.concatenate([sum(impm[(g * NSA_GROUP + r) * ls:(g * NSA_GROUP + r + 1) * ls] for r in range(NSA_GROUP))
                           for g in range(ng)], axis=0)
    imp = jnp.concatenate([imp, jnp.zeros((LANES - ng * ls, imp.shape[1]), F32)], axis=0)
    qpos_lane = past + lax.broadcasted_iota(jnp.int32, (1, LANES), 1) % ls
    sel = _top_blocks(imp, qpos_lane)[0:ng * ls]
    selneg = ((sel - 1.0) * BIG).astype(MXU)
    selneg = jnp.concatenate([selneg[g * ls:(g + 1) * ls] for g in range(ng) for _ in range(NSA_GROUP)], axis=0)
    lsl = (jnp.dot(qall, kt_ref[slot].astype(MXU), preferred_element_type=F32)
           + jnp.dot(selneg, eall_ref[...], preferred_element_type=F32))
    lsl = jnp.concatenate([lsl[:, 0:nk - TAIL], lsl[:, nk - TAIL:] + per_group(tail_ref)], axis=1)
    o_slc = _mm_nt(_softmax_rows(lsl), vt_ref[slot])
    pw = _softmax_rows(jnp.dot(qall, kwt, preferred_element_type=F32) + per_group(wtab_ref))
    o_win = _mm_nt(pw, vwt)
    for g in range(ng):
        gates = gate_ref[:, g * LANES:(g + 1) * LANES]
        ohs = []
        for r in range(NSA_GROUP):
            rows = slice((g * NSA_GROUP + r) * ls, (g * NSA_GROUP + r + 1) * ls)
            ohs.append(_gate_mix(gates, r, o_cmp[rows], o_slc[rows], o_win[rows]))
        o_ref[:, g * gw:(g + 1) * gw] = _unplace_heads(ohs, g)


def _nsa_sample(rel_bias, q, gates, kvs, kvw, kc, vc, win_t, pool_t, page_table, ls):
    nseq, npages = page_table.shape
    page_size = pool_t.shape[3]
    past = npages * page_size
    ncp = kc.shape[1]
    wk = win_t.shape[3]
    wkp = wk + LANES
    nk = past + LANES
    assert past % SLC_BLOCK == 0 and ls % 8 == 0 and ls <= SLC_BLOCK and past >= wk and past // SLC_BLOCK + 1 <= 2 * LANES
    assert ncp % LANES == 0 and past >= CMP_BLOCK and page_size % LANES == 0 and wk % LANES == 0 and KV_WIDTH == LANES
    row = lambda width: pl.BlockSpec((ls, width), lambda b, pt: (b, 0))
    seq3 = lambda n, width: pl.BlockSpec((1, n, width), lambda b, pt: (b, 0, 0))
    kw = 2 * KV_WIDTH
    nrow = NSA_GROUP * ls
    return pl.pallas_call(
        functools.partial(_nsa_sample_kernel, ls=ls, past=past, npages=npages, page_size=page_size, ncp=ncp,
                          wk=wk, wkp=wkp, nk=nk),
        grid_spec=pltpu.PrefetchScalarGridSpec(
            num_scalar_prefetch=1,
            grid=(nseq,),
            in_specs=[pl.BlockSpec(memory_space=pltpu.SMEM), row(NSA_INNER), row(NSA_KV_HEADS * LANES), row(kw), row(kw),
                      seq3(ncp, LANES), seq3(ncp, LANES),
                      pl.BlockSpec((1, 2, KV_WIDTH, wk), lambda b, pt: (b, 0, 0, 0)), pl.BlockSpec(memory_space=pl.ANY)],
            out_specs=row(NSA_INNER),
            scratch_shapes=[pltpu.VMEM((2, KV_WIDTH, nk), F32), pltpu.VMEM((2, KV_WIDTH, nk), F32),
                            pltpu.SemaphoreType.DMA((2, 2)),
                            pltpu.VMEM((NSA_KV_HEADS, nrow, ncp), F32), pltpu.VMEM((NSA_KV_HEADS, nrow, TAIL), F32),
                            pltpu.VMEM((NSA_KV_HEADS, nrow, wkp), F32),
                            pltpu.VMEM((ncp, 2 * LANES), MXU), pltpu.VMEM((2 * LANES, nk), MXU)],
        ),
        out_shape=jax.ShapeDtypeStruct((nseq * ls, NSA_INNER), F32),
        compiler_params=_cparams("arbitrary"),
    )(page_table.reshape(-1), rel_bias, q, gates, kvs, kvw, kc, vc, win_t, pool_t)


def _mix_out_kernel(x_ref, yssd_ref, onsa_ref, gn_ref, wo_ref, gq_ref, wq_ref, mkv_ref, wmo_ref, o_ref, *, rows_per_seq):
    y = jnp.concatenate([yssd_ref[...], _rms(onsa_ref[...], gn_ref[...])], axis=1)
    x = x_ref[...] + jnp.dot(y.astype(MXU), wo_ref[...], preferred_element_type=F32)
    qm = jnp.dot(_rms(x, gq_ref[...]).astype(MXU), wq_ref[...], preferred_element_type=F32)
    spb, m = mkv_ref.shape[0:2]
    kv = mkv_ref[...].reshape(spb * m, 2 * MEM_INNER)
    km = kv[:, 0:MEM_INNER].astype(MXU)
    vm = kv[:, MEM_INNER:].astype(MXU)
    head = lax.broadcasted_iota(jnp.int32, qm.shape, 1) // HEAD_DIM
    if spb > 1:
        rseq = lax.broadcasted_iota(jnp.int32, (qm.shape[0], spb * m), 0) // rows_per_seq
        kseq = lax.broadcasted_iota(jnp.int32, (qm.shape[0], spb * m), 1) // m
        own = rseq == kseq
    o = jnp.zeros(qm.shape, F32)
    for h in range(MEM_HEADS):
        lg = _mm_nt(jnp.where(head == h, qm, 0.0), km) * (HEAD_DIM ** -0.5)
        if spb > 1:
            lg = jnp.where(own, lg, NEG)
        o = o + jnp.where(head == h, jnp.dot(_softmax_rows(lg).astype(MXU), vm, preferred_element_type=F32), 0.0)
    o_ref[...] = x + jnp.dot(o.astype(MXU), wmo_ref[...], preferred_element_type=F32)


def _mix_out(x, yssd, onsa, g_nsa, w_out, g_memq, w_mem_q, mem_kv, w_mem_out, nseq, seq_len, tm=512, short_rows=64):
    t, d = x.shape
    m = mem_kv.shape[1]
    if seq_len < short_rows and nseq % (short_rows // seq_len) == 0:
        spb = short_rows // seq_len
        tm, nt = spb * seq_len, 1
    else:
        spb, tm = 1, min(tm, seq_len)
        nt = seq_len // tm
        assert nt * tm == seq_len
    row = lambda width: pl.BlockSpec((tm, width), lambda b, i: (b * nt + i, 0))
    return pl.pallas_call(
        functools.partial(_mix_out_kernel, rows_per_seq=seq_len),
        grid=(nseq // spb, nt),
        in_specs=[row(d), row(SSD_INNER), row(NSA_INNER), _const_spec((1, NSA_INNER)), _const_spec(w_out.shape),
                  _const_spec((1, d)), _const_spec(w_mem_q.shape),
                  pl.BlockSpec((spb, m, 2 * MEM_INNER), lambda b, i: (b, 0, 0)), _const_spec(w_mem_out.shape)],
        out_specs=row(d),
        out_shape=jax.ShapeDtypeStruct((t, d), F32),
        compiler_params=_cparams("parallel", "parallel"),
    )(x, yssd, onsa, g_nsa.reshape(1, -1), w_out.astype(MXU), g_memq.reshape(1, d), w_mem_q.astype(MXU),
      mem_kv, w_mem_out.astype(MXU))


def _norm_matmul_kernel(x_ref, g_ref, w_ref, o_ref):
    o_ref[...] = jnp.dot(_rms(x_ref[...], g_ref[...]).astype(MXU), w_ref[...], preferred_element_type=F32)


def _norm_matmul(x, g, w, tm=256):
    t, d = x.shape
    n = w.shape[1]
    tm = min(tm, t)
    assert t % tm == 0
    return pl.pallas_call(
        _norm_matmul_kernel,
        grid=(t // tm,),
        in_specs=[pl.BlockSpec((tm, d), lambda i: (i, 0)), _const_spec((1, d)), _const_spec(w.shape)],
        out_specs=pl.BlockSpec((tm, n), lambda i: (i, 0)),
        out_shape=jax.ShapeDtypeStruct((t, n), F32),
        compiler_params=_cparams("parallel"),
    )(x, g.reshape(1, d), w.astype(MXU))


def _trunk_layer(x, lw, cw, rel_bias, mem_kv, conv0, h0, nseq, seq_len, sample=None):
    x1 = _ffn(x, lw['norm_ffn1'], lw['w_ffn1_in'], lw['w_ffn1_out'])
    outs = _proj(x1, lw['norm_mix'], lw['w_mix'], seq_len=None if sample else seq_len)
    z, xbc, q, kvc, kvs, kvw, gates, dt = outs[:8]
    y_ssd, h_new, conv_new = _ssd(z, xbc, dt, conv0, h0, lw['conv_w'], lw['conv_b'], lw['dt_bias'], lw['a_log'],
                                  lw['d_skip'], lw['norm_ssd_out'], nseq, seq_len)
    rows5 = lambda a: a.reshape(nseq, seq_len, 2, NSA_KV_HEADS, HEAD_DIM)
    if sample is None:
        kc, vc = _compress_prompt(kvc, cw, nseq, seq_len)
        o_nsa = _nsa_prompt(rel_bias, q, gates, kc, vc, *outs[8:12], nseq, seq_len)
        from_t = lambda a: jnp.transpose(a.reshape(nseq, 2, NSA_KV_HEADS, HEAD_DIM, seq_len), (0, 4, 1, 2, 3))
        kv_c, kv_s = from_t(outs[12]), from_t(outs[13])
    else:
        kc, vc = _compress_sample(sample['pool_c'], sample['page_table'], cw)
        o_nsa = _nsa_sample(rel_bias, q, gates, kvs, kvw, kc, vc, sample['win_t'], sample['pool_s'],
                            sample['page_table'], seq_len)
        kv_c, kv_s = rows5(kvc), rows5(kvs)
    x2 = _mix_out(x1, y_ssd, o_nsa, lw['norm_nsa_out'], lw['w_out'], lw['norm_mem_q'], lw['w_mem_q'], mem_kv,
                  lw['w_mem_out'], nseq, seq_len)
    keep = min(WINDOW, seq_len)
    kv_w = kvw.reshape(nseq, seq_len, -1)[:, seq_len - keep:].reshape(nseq, keep, 2, NSA_KV_HEADS, HEAD_DIM)
    return x2, kv_c, kv_s, kv_w, h_new, conv_new


def kernel(x_prompt, x_sample, cache_kv_cmp, cache_kv_slc, state_win_kv, state_ssd, state_conv, cache_mem_kv, page_table, mem_prompt, norm_ffn1, w_ffn1_in, w_ffn1_out, norm_mix, w_in, conv_w, conv_b, dt_bias, a_log, d_skip, norm_ssd_out, cmp_pos_k, cmp_w1_k, cmp_w2_k, cmp_pos_v, cmp_w1_v, cmp_w2_v, norm_nsa_out, w_out, norm_mem_q, norm_mem_kv, w_mem_q, w_mem_kv, w_mem_out, norm_ffn2, w_ffn2_in, w_ffn2_out, rel_bias, norm_final):
    bp, lp, d = x_prompt.shape
    bs, lsq, _ = x_sample.shape
    depth = w_in.shape[0]
    xp = x_prompt.reshape(bp * lp, d)
    xs = x_sample.reshape(bs * lsq, d)
    outs_p = [[] for _ in range(6)]
    outs_s = [[] for _ in range(5)]
    for layer in range(depth):
        last = layer == depth - 1
        lw = {
            'norm_ffn1': norm_ffn1[layer], 'w_ffn1_in': w_ffn1_in[layer], 'w_ffn1_out': w_ffn1_out[layer],
            'norm_mix': norm_mix[layer], 'w_mix': _regroup_w_in(w_in[layer]), 'conv_w': conv_w[layer],
            'conv_b': conv_b[layer], 'dt_bias': dt_bias[layer], 'a_log': a_log[layer], 'd_skip': d_skip[layer],
            'norm_ssd_out': norm_ssd_out[layer], 'norm_nsa_out': norm_nsa_out[layer], 'w_out': w_out[layer],
            'norm_mem_q': norm_mem_q[layer], 'w_mem_q': w_mem_q[layer], 'w_mem_out': w_mem_out[layer],
        }
        cw = _compress_weights(cmp_pos_k[layer], cmp_w1_k[layer], cmp_w2_k[layer],
                               cmp_pos_v[layer], cmp_w1_v[layer], cmp_w2_v[layer])
        g_final = norm_final if last else None
        mem_kv_p = _norm_matmul(mem_prompt.reshape(-1, d), norm_mem_kv[layer], w_mem_kv[layer])
        mem_kv_p = mem_kv_p.reshape(bp, -1, 2 * MEM_INNER)
        conv0 = jnp.zeros((bp, SSD_CONV - 1, SSD_CONV_DIM), F32)
        h0 = jnp.zeros((bp, SSD_HEADS, HEAD_DIM, SSD_STATE), F32)
        xp, c_p, s_p, w_p, h_p, conv_p = _trunk_layer(xp, lw, cw, rel_bias, mem_kv_p, conv0, h0, bp, lp)
        xp = _ffn(xp, norm_ffn2[layer], w_ffn2_in[layer], w_ffn2_out[layer], g_final)
        sample = dict(pool_c=_paged_view(cache_kv_cmp[layer]), pool_s=_paged_view(cache_kv_slc[layer]),
                      win_t=_paged_view(state_win_kv[layer]), page_table=page_table)
        mem_kv_s = cache_mem_kv[layer].reshape(bs, -1, 2 * MEM_INNER)
        xs, c_s, s_s, w_s, h_s, conv_s = _trunk_layer(xs, lw, cw, rel_bias, mem_kv_s, state_conv[layer],
                                                      state_ssd[layer], bs, lsq, sample=sample)
        xs = _ffn(xs, norm_ffn2[layer], w_ffn2_in[layer], w_ffn2_out[layer], g_final)
        w_s = jnp.concatenate([state_win_kv[layer], w_s], axis=1)[:, w_s.shape[1]:]
        for lst, v in zip(outs_p, (c_p, s_p, w_p, h_p, conv_p, mem_kv_p.reshape(bp, -1, 2, MEM_HEADS, HEAD_DIM))):
            lst.append(v)
        for lst, v in zip(outs_s, (c_s, s_s, w_s, h_s, conv_s)):
            lst.append(v)
    return (xp.reshape(bp, lp, d), xs.reshape(bs, lsq, d),
            *(jnp.stack(o) for o in outs_p), *(jnp.stack(o) for o in outs_s))
```

```python
import functools
import math

import numpy as np
import jax
import jax.numpy as jnp
from jax import lax
from jax.experimental import pallas as pl
from jax.experimental.pallas import tpu as pltpu

F32 = jnp.float32
MXU = jnp.bfloat16

HEAD_DIM = 64
SSD_HEADS = 8
SSD_GROUPS = 2
SSD_STATE = 128
SSD_CONV = 4
SSD_CHUNK = 128
NSA_HEADS = 8
NSA_KV_HEADS = 2
NSA_GROUP = NSA_HEADS // NSA_KV_HEADS
CMP_BLOCK = 32
CMP_STRIDE = 16
CMP_HIDDEN = 256
SLC_BLOCK = 64
SLC_TOPN = 16
WINDOW = 512
REL_BUCKETS = 32
REL_MAX_DIST = 128
MEM_HEADS = 4
Q_BLOCK = 128
EPS = 1e-6
NEG = -1e30

SSD_INNER = SSD_HEADS * HEAD_DIM
NSA_INNER = NSA_HEADS * HEAD_DIM
KV_WIDTH = NSA_KV_HEADS * HEAD_DIM
SSD_CONV_DIM = SSD_INNER + 2 * SSD_GROUPS * SSD_STATE
MEM_INNER = MEM_HEADS * HEAD_DIM
LANES = 128
VMEM_LIMIT = 56 * 1024 * 1024


def _cparams(*sem):
    return pltpu.CompilerParams(dimension_semantics=sem, vmem_limit_bytes=VMEM_LIMIT)


def _const_spec(shape):
    nd = len(shape)
    return pl.BlockSpec(shape, lambda *a: (0,) * nd, pipeline_mode=pl.Buffered(1))


def _mm(a, b):
    return jnp.dot(a.astype(MXU), b.astype(MXU), preferred_element_type=F32)


def _mm_nt(a, b):
    return lax.dot_general(a.astype(MXU), b.astype(MXU), (((1,), (1,)), ((), ())), preferred_element_type=F32)


def _rms(x, g):
    return x * lax.rsqrt(jnp.mean(x * x, axis=-1, keepdims=True) + EPS) * g


def _silu(x):
    return x * jax.nn.sigmoid(x)


FFN_CHUNK = 256


def _ffn_kernel(x_ref, g_ref, win_ref, wout_ref, *rest, nchunks, final_norm):
    if final_norm:
        gf_ref, o_ref, h_ref = rest
    else:
        o_ref, h_ref = rest
    x = x_ref[...]
    h_ref[...] = _rms(x, g_ref[...]).astype(h_ref.dtype)
    f = nchunks * FFN_CHUNK
    for j in range(nchunks):
        cols = slice(j * FFN_CHUNK, (j + 1) * FFN_CHUNK)
        h = h_ref[...]
        gate = jnp.dot(h, win_ref[:, cols], preferred_element_type=F32)
        up = jnp.dot(h, win_ref[:, f + j * FFN_CHUNK:f + (j + 1) * FFN_CHUNK], preferred_element_type=F32)
        a = (_silu(gate) * up).astype(h.dtype)
        t = jnp.dot(a, wout_ref[cols, :], preferred_element_type=F32)
        if j == 0:
            o_ref[...] = t
        else:
            o_ref[...] += t
    y = x_ref[...] + 0.5 * o_ref[...]
    if final_norm:
        y = _rms(y, gf_ref[...])
    o_ref[...] = y


def _ffn(x, g, w_in, w_out, g_final=None, tm=512):
    t, d = x.shape
    f = w_out.shape[0]
    nchunks = f // FFN_CHUNK
    tm = min(tm, t)
    assert nchunks * FFN_CHUNK == f and t % tm == 0
    win = w_in.astype(MXU)
    wout = w_out.astype(MXU)
    final_norm = g_final is not None
    args = [x, g.reshape(1, d), win, wout]
    specs = [pl.BlockSpec((tm, d), lambda i: (i, 0)), _const_spec((1, d)),
             _const_spec(win.shape), _const_spec(wout.shape)]
    if final_norm:
        args.append(g_final.reshape(1, d))
        specs.append(_const_spec((1, d)))
    return pl.pallas_call(
        functools.partial(_ffn_kernel, nchunks=nchunks, final_norm=final_norm),
        grid=(t // tm,),
        in_specs=specs,
        out_specs=pl.BlockSpec((tm, d), lambda i: (i, 0)),
        out_shape=jax.ShapeDtypeStruct((t, d), F32),
        scratch_shapes=[pltpu.VMEM((tm, d), MXU)],
        compiler_params=_cparams("parallel"),
    )(*args)


PROJ_COLS = 3200
_O_Z, _O_XBC, _O_Q, _O_KVC, _O_KVS, _O_KVW, _O_G, _O_DT = 0, 512, 1536, 2048, 2304, 2560, 2816, 3072


def _regroup_w_in(w_in):
    o = np.cumsum([0, SSD_INNER, SSD_CONV_DIM, SSD_HEADS, NSA_INNER] + [KV_WIDTH] * 6 + [3 * NSA_HEADS])
    z, xbc, dt, q, kvs, gl = (w_in[:, o[0]:o[1]], w_in[:, o[1]:o[2]], w_in[:, o[2]:o[3]], w_in[:, o[3]:o[4]],
                              w_in[:, o[4]:o[10]], w_in[:, o[10]:o[11]])
    pad = lambda w: jnp.pad(w, ((0, 0), (0, LANES - w.shape[1])))
    ng = 3 * NSA_GROUP
    gls = [pad(gl[:, g * ng:(g + 1) * ng]) for g in range(NSA_KV_HEADS)]
    return jnp.concatenate([z, xbc, q, kvs] + gls + [pad(dt)], axis=1).astype(MXU)


def _proj_kernel(x_ref, g_ref, w_ref, z_ref, xbc_ref, q_ref, kvc_ref, kvs_ref, kvw_ref, gate_ref, dt_ref, *aug,
                 tiles_per_seq):
    h = _rms(x_ref[...], g_ref[...]).astype(w_ref.dtype)
    res = jnp.dot(h, w_ref[...], preferred_element_type=F32)
    tm = res.shape[0]
    z_ref[...] = res[:, _O_Z:_O_XBC]
    xbc_ref[...] = res[:, _O_XBC:_O_Q]
    q_ref[...] = res[:, _O_Q:_O_KVC] * (HEAD_DIM ** -0.5)
    kvc_ref[...] = res[:, _O_KVC:_O_KVS]
    kvs = res[:, _O_KVS:_O_KVW]
    kvw = res[:, _O_KVW:_O_G]
    kvs_ref[...] = kvs
    kvw_ref[...] = kvw
    gate_ref[...] = jax.nn.sigmoid(res[:, _O_G:_O_DT])
    dt_ref[...] = res[:, _O_DT:PROJ_COLS]
    if aug:
        ksa_ref, vsa_ref, kw_ref, vwa_ref, kvct_ref, kvst_ref = aug
        kvct_ref[0] = res[:, _O_KVC:_O_KVS].T
        kvst_ref[0] = kvs.T
        lane = lax.broadcasted_iota(jnp.int32, (tm, LANES), 1)
        row = lax.broadcasted_iota(jnp.int32, (tm, LANES), 0)
        pos = (pl.program_id(0) % tiles_per_seq) * tm + row
        onehot = jnp.where(lane == pos // SLC_BLOCK, 1.0, 0.0)
        ones_col = jnp.where(lane == 0, 1.0, 0.0)
        dt_ = ksa_ref.dtype
        ksa_ref[:, 0:LANES] = kvs[:, 0:LANES].astype(dt_)
        ksa_ref[:, LANES:] = onehot.astype(dt_)
        vsa_ref[:, 0:LANES] = kvs[:, LANES:].astype(dt_)
        vsa_ref[:, LANES:] = ones_col.astype(dt_)
        kw_ref[...] = kvw[:, 0:LANES].astype(dt_)
        vwa_ref[:, 0:LANES] = kvw[:, LANES:].astype(dt_)
        vwa_ref[:, LANES:] = ones_col.astype(dt_)


def _proj(x, g, w, seq_len=None, tm=512):
    t, d = x.shape
    tm = min(tm, t)
    assert t % tm == 0
    widths = [512, 1024, 512, 256, 256, 256, NSA_KV_HEADS * LANES, LANES]
    shapes = [jax.ShapeDtypeStruct((t, wd), F32) for wd in widths]
    specs = [pl.BlockSpec((tm, wd), lambda i: (i, 0)) for wd in widths]
    aug = seq_len is not None
    tps = seq_len // tm if aug else 1
    if aug:
        assert seq_len % tm == 0 and seq_len // SLC_BLOCK <= LANES
        for wd in (256, 256, LANES, 256):
            shapes.append(jax.ShapeDtypeStruct((t, wd), MXU))
            specs.append(pl.BlockSpec((tm, wd), lambda i: (i, 0)))
        for _ in range(2):
            shapes.append(jax.ShapeDtypeStruct((t // seq_len, 2 * KV_WIDTH, seq_len), F32))
            specs.append(pl.BlockSpec((1, 2 * KV_WIDTH, tm), lambda i: (i // tps, 0, i % tps)))
    return pl.pallas_call(
        functools.partial(_proj_kernel, tiles_per_seq=tps),
        grid=(t // tm,),
        in_specs=[pl.BlockSpec((tm, d), lambda i: (i, 0)), _const_spec((1, d)), _const_spec(w.shape)],
        out_specs=specs,
        out_shape=shapes,
        compiler_params=_cparams("parallel"),
    )(x, g.reshape(1, d), w)


HALO = 8


def _split3(x):
    hi = x.astype(jnp.bfloat16).astype(F32)
    r = x - hi
    mid = r.astype(jnp.bfloat16).astype(F32)
    return hi, mid, r - mid


def _ssd_kernel(z_ref, xbc_ref, dt_ref, conv0_ref, h0_ref, cw_ref, cb_ref, dtb_ref, alog_ref, dskip_ref, g_ref,
                y_ref, hout_ref, convout_ref, xwin_ref, h_ref, *, lv):
    q = SSD_CHUNK
    c = pl.program_id(1)

    @pl.when(c == 0)
    def _():
        xwin_ref[0:HALO, :] = conv0_ref[0]
        h_ref[...] = h0_ref[0]

    def padrows(v):
        if lv == q:
            return v
        return jnp.concatenate([v, jnp.zeros((q - lv, v.shape[1]), v.dtype)], axis=0)

    xwin_ref[HALO:HALO + q, :] = padrows(xbc_ref[...])
    acc = cb_ref[...] + cw_ref[SSD_CONV - 1:SSD_CONV, :] * xwin_ref[HALO:HALO + q, :]
    for k in range(SSD_CONV - 1):
        acc = acc + cw_ref[k:k + 1, :] * xwin_ref[pl.ds(HALO - (SSD_CONV - 1) + k, q), :]
    xc = _silu(acc)
    tail = xwin_ref[lv:lv + HALO, :]
    xwin_ref[0:HALO, :] = tail
    convout_ref[0] = tail

    xs = xc[:, 0:SSD_INNER]
    rowi = lax.broadcasted_iota(jnp.int32, (q, q), 0)
    coli = lax.broadcasted_iota(jnp.int32, (q, q), 1)
    causal = rowi >= coli
    tri = jnp.where(causal, 1.0, 0.0).astype(jnp.bfloat16)

    xdt = padrows(dt_ref[...]) + dtb_ref[...]
    dt = jnp.maximum(xdt, 0.0) + jnp.log1p(jnp.exp(-jnp.abs(xdt)))
    if lv != q:
        dt = jnp.where(rowi < lv, dt, 0.0)
    da = dt * (-jnp.exp(alog_ref[...]))
    acs = sum(jnp.dot(tri, p.astype(jnp.bfloat16), preferred_element_type=F32) for p in _split3(da))
    acs_t = acs.T
    dt_t = dt.T
    last = acs[q - 1:q, :]
    last_t = acs_t[:, q - 1:q]
    xs_t = [xs[:, p * LANES:(p + 1) * LANES].T for p in range(SSD_HEADS // 2)]
    lane_hi = lax.broadcasted_iota(jnp.int32, (q, LANES), 1) >= HEAD_DIM
    row_hi = lax.broadcasted_iota(jnp.int32, (LANES, q), 0) >= HEAD_DIM
    hpg = SSD_HEADS // SSD_GROUPS
    ys = []
    for g in range(SSD_GROUPS):
        bm = xc[:, SSD_INNER + g * SSD_STATE:SSD_INNER + (g + 1) * SSD_STATE]
        cm = xc[:, SSD_INNER + (SSD_GROUPS + g) * SSD_STATE:SSD_INNER + (SSD_GROUPS + g + 1) * SSD_STATE]
        cb = _mm_nt(cm, bm)
        for pp in range(hpg // 2):
            p = g * (hpg // 2) + pp
            xpair = xs[:, p * LANES:(p + 1) * LANES]
            yi = []
            for hh in range(2):
                h = 2 * p + hh
                seg = acs[:, h:h + 1] - acs_t[h:h + 1, :]
                decay = jnp.where(causal, jnp.exp(jnp.where(causal, seg, 0.0)), 0.0)
                yi.append(_mm(cb * decay * dt_t[h:h + 1, :], xpair))
            y_intra = jnp.where(lane_hi, yi[1], yi[0])
            h0, h1 = 2 * p, 2 * p + 1
            grow = jnp.where(lane_hi, jnp.exp(acs[:, h1:h1 + 1]), jnp.exp(acs[:, h0:h0 + 1]))
            y_inter = _mm_nt(cm, h_ref[p]) * grow
            ys.append(y_intra + y_inter)
            te = jnp.where(row_hi,
                           jnp.exp(last_t[h1:h1 + 1, :] - acs_t[h1:h1 + 1, :]) * dt_t[h1:h1 + 1, :],
                           jnp.exp(last_t[h0:h0 + 1, :] - acs_t[h0:h0 + 1, :]) * dt_t[h0:h0 + 1, :])
            states = _mm(xs_t[p] * te, bm)
            rdec = jnp.where(row_hi[:, 0:1], jnp.exp(last_t[h1:h1 + 1, :]), jnp.exp(last_t[h0:h0 + 1, :]))
            h_ref[p] = h_ref[p] * rdec + states
    y = jnp.concatenate(ys, axis=1) + dskip_ref[...] * xs
    y = _rms(y * _silu(padrows(z_ref[...])), g_ref[...])
    y_ref[...] = y[0:lv, :]
    hout_ref[0] = h_ref[...]


def _ssd(z, xbc, dt, conv0, h0, conv_w, conv_b, dt_bias, a_log, d_skip, norm_g, nseq, seq_len):
    q = SSD_CHUNK
    lv = min(q, seq_len)
    nch = seq_len // lv
    assert nch * lv == seq_len
    t = nseq * seq_len
    cdim = SSD_CONV_DIM
    keep = SSD_CONV - 1
    conv0p = jnp.pad(conv0, ((0, 0), (HALO - keep, 0), (0, 0)))
    padl = lambda v: jnp.pad(v.reshape(1, -1), ((0, 0), (0, LANES - v.size)))
    hp = h0.reshape(nseq, SSD_HEADS // 2, 2 * HEAD_DIM, SSD_STATE)
    row = lambda b, c: (b * nch + c, 0)
    y, hout, convout = pl.pallas_call(
        functools.partial(_ssd_kernel, lv=lv),
        grid=(nseq, nch),
        in_specs=[pl.BlockSpec((lv, SSD_INNER), row), pl.BlockSpec((lv, cdim), row), pl.BlockSpec((lv, LANES), row),
                  pl.BlockSpec((1, HALO, cdim), lambda b, c: (b, 0, 0)),
                  pl.BlockSpec((1,) + hp.shape[1:], lambda b, c: (b, 0, 0, 0)),
                  _const_spec((SSD_CONV, cdim)), _const_spec((1, cdim)), _const_spec((1, LANES)),
                  _const_spec((1, LANES)), _const_spec((1, SSD_INNER)), _const_spec((1, SSD_INNER))],
        out_specs=[pl.BlockSpec((lv, SSD_INNER), row),
                   pl.BlockSpec((1,) + hp.shape[1:], lambda b, c: (b, 0, 0, 0)),
                   pl.BlockSpec((1, HALO, cdim), lambda b, c: (b, 0, 0))],
        out_shape=[jax.ShapeDtypeStruct((t, SSD_INNER), F32), jax.ShapeDtypeStruct(hp.shape, F32),
                   jax.ShapeDtypeStruct((nseq, HALO, cdim), F32)],
        scratch_shapes=[pltpu.VMEM((HALO + q, cdim), F32), pltpu.VMEM(hp.shape[1:], F32)],
        compiler_params=_cparams("parallel", "arbitrary"),
    )(z, xbc, dt, conv0p, hp, conv_w, conv_b.reshape(1, cdim), padl(dt_bias), padl(a_log),
      jnp.repeat(d_skip, HEAD_DIM).reshape(1, SSD_INNER), norm_g.reshape(1, SSD_INNER))
    return y, hout.reshape(nseq, SSD_HEADS, HEAD_DIM, SSD_STATE), convout[:, HALO - keep:, :]


def _compress_weights(pe_k, w1_k, w2_k, pe_v, w1_v, w2_v):
    def w1_pair(w1):
        w = w1.reshape(CMP_BLOCK, HEAD_DIM, CMP_HIDDEN)
        zz = jnp.zeros_like(w)
        w = jnp.concatenate([jnp.concatenate([w, zz], axis=2), jnp.concatenate([zz, w], axis=2)], axis=1)
        return w.reshape(CMP_BLOCK // 2, 2 * KV_WIDTH, NSA_KV_HEADS * CMP_HIDDEN).astype(MXU)

    def w2_pair(w2):
        zz = jnp.zeros_like(w2)
        return jnp.concatenate([jnp.concatenate([w2, zz], axis=1), jnp.concatenate([zz, w2], axis=1)], axis=0).astype(MXU)

    pe = jnp.concatenate([pe_k, pe_k, pe_v, pe_v], axis=1)
    return pe, w1_pair(w1_k), w1_pair(w1_v), w2_pair(w2_k), w2_pair(w2_v)


def _compress_first_layer(row_of_chunks, pe_ref, wk_ref, wv_ref):
    half = CMP_STRIDE
    a0k = a1k = a0v = a1v = None
    for s2 in range(half // 2):
        rows = [row_of_chunks(2 * s2 + u) for u in range(2)]

        def lhs(j, kv):
            lanes = slice(kv * LANES, (kv + 1) * LANES)
            return jnp.concatenate([(rows[u][kv] + pe_ref[j * half + 2 * s2 + u:j * half + 2 * s2 + u + 1, lanes])
                                    for u in range(2)], axis=1).astype(MXU)

        t0k = jnp.dot(lhs(0, 0), wk_ref[s2], preferred_element_type=F32)
        t0v = jnp.dot(lhs(0, 1), wv_ref[s2], preferred_element_type=F32)
        t1k = jnp.dot(lhs(1, 0), wk_ref[half // 2 + s2], preferred_element_type=F32)
        t1v = jnp.dot(lhs(1, 1), wv_ref[half // 2 + s2], preferred_element_type=F32)
        a0k, a1k = (t0k, t1k) if s2 == 0 else (a0k + t0k, a1k + t1k)
        a0v, a1v = (t0v, t1v) if s2 == 0 else (a0v + t0v, a1v + t1v)
    return a0k, a1k, a0v, a1v


def _compress_compute(rows_of_parts, pe_ref, wk_ref, wv_ref, w2k_ref, w2v_ref, sh_ref, kc_ref, vc_ref, nc):
    parts = [_compress_first_layer(f, pe_ref, wk_ref, wv_ref) for f in rows_of_parts]
    a0k, a1k, a0v, a1v = (p[0] if len(parts) == 1 else jnp.concatenate(p, axis=0) for p in zip(*parts))
    sh_ref[nc:nc + 8, :] = jnp.zeros((8, sh_ref.shape[1]), F32)
    for a0, a1, w2_ref, o_ref in ((a0k, a1k, w2k_ref, kc_ref), (a0v, a1v, w2v_ref, vc_ref)):
        sh_ref[0:nc, :] = a1
        hid = a0 + sh_ref[pl.ds(1, nc), :]
        o_ref[...] = jnp.dot(_silu(hid).astype(MXU), w2_ref[...], preferred_element_type=F32)


def _compress_prompt_kernel(rows_ref, pe_ref, wk_ref, wv_ref, w2k_ref, w2v_ref, kc_ref, vc_ref, sh_ref, *, nc):
    rw = 2 * KV_WIDTH

    def row_of_chunks(s):
        return rows_ref[:, s * rw:s * rw + LANES], rows_ref[:, s * rw + LANES:(s + 1) * rw]

    _compress_compute([row_of_chunks], pe_ref, wk_ref, wv_ref, w2k_ref, w2v_ref, sh_ref, kc_ref.at[0], vc_ref.at[0], nc)


def _compress_specs(cw):
    return [_const_spec(w.shape) for w in cw]


def _compress_prompt(kv_cmp, cw, nseq, seq_len):
    nc = seq_len // CMP_STRIDE
    out = jax.ShapeDtypeStruct((nseq, nc, LANES), F32)
    return pl.pallas_call(
        functools.partial(_compress_prompt_kernel, nc=nc),
        grid=(nseq,),
        in_specs=[pl.BlockSpec((nc, CMP_STRIDE * 2 * KV_WIDTH), lambda b: (b, 0))] + _compress_specs(cw),
        out_specs=[pl.BlockSpec((1, nc, LANES), lambda b: (b, 0, 0))] * 2,
        out_shape=[out, out],
        scratch_shapes=[pltpu.VMEM((nc + 8, NSA_KV_HEADS * CMP_HIDDEN), F32)],
        compiler_params=_cparams("parallel"),
    )(kv_cmp.reshape(nseq * nc, CMP_STRIDE * 2 * KV_WIDTH), *cw)


PAGE_UNROLL = 8


def _gather_step(page_copies, npages):
    b = pl.program_id(0)
    nb = pl.num_programs(0)
    slot = b % 2
    assert npages % PAGE_UNROLL == 0

    def for_pages(fn):
        def body(q, c):
            for u in range(PAGE_UNROLL):
                fn(q * PAGE_UNROLL + u)
            return c
        lax.fori_loop(0, npages // PAGE_UNROLL, body, 0)

    def start_all(seq, sl):
        for_pages(lambda p: [cp.start() for cp in page_copies(seq, sl, p)])

    @pl.when(b == 0)
    def _():
        start_all(0, 0)

    @pl.when(b + 1 < nb)
    def _():
        start_all(b + 1, 1 - slot)

    for_pages(lambda p: [cp.wait() for cp in page_copies(b, slot, p)])
    return slot


def _paged_view(cache):
    n_pool, page = cache.shape[:2]
    return jnp.transpose(cache, (0, 2, 3, 4, 1)).reshape(n_pool, 2, KV_WIDTH, page)


CMP_PARTS = 4


def _compress_sample_kernel(pt_ref, pool_ref, pe_ref, wk_ref, wv_ref, w2k_ref, w2v_ref, kc_ref, vc_ref,
                            buf_ref, sem_ref, rk_ref, rv_ref, sh_ref, *, nc, npages, page_size):
    def page_copies(seq, slot, p):
        return [pltpu.make_async_copy(pool_ref.at[pt_ref[seq * npages + p]], buf_ref.at[slot, p], sem_ref.at[slot])]

    slot = _gather_step(page_copies, npages)
    nparts = rk_ref.shape[0]
    ppp = npages // nparts
    ncp = nc // nparts
    rows_of_parts = []
    for part in range(nparts):
        for u in range(ppp):
            rows = slice(u * page_size, (u + 1) * page_size)
            rk_ref[part, rows, :] = buf_ref[slot, part * ppp + u, 0].T
            rv_ref[part, rows, :] = buf_ref[slot, part * ppp + u, 1].T
        rows_of_parts.append(lambda s, part=part: (rk_ref[part, pl.ds(s, ncp, stride=CMP_STRIDE), :],
                                                   rv_ref[part, pl.ds(s, ncp, stride=CMP_STRIDE), :]))
    _compress_compute(rows_of_parts, pe_ref, wk_ref, wv_ref, w2k_ref, w2v_ref, sh_ref, kc_ref.at[0], vc_ref.at[0], nc)


def _compress_sample(pool_t, page_table, cw):
    nseq, npages = page_table.shape
    page_size = pool_t.shape[3]
    past = npages * page_size
    nc = past // CMP_STRIDE
    assert page_size == LANES and KV_WIDTH == LANES
    out = jax.ShapeDtypeStruct((nseq, nc, LANES), F32)
    cspecs = [pl.BlockSpec(w.shape, lambda b, pt, nd=w.ndim: (0,) * nd, pipeline_mode=pl.Buffered(1)) for w in cw]
    return pl.pallas_call(
        functools.partial(_compress_sample_kernel, nc=nc, npages=npages, page_size=page_size),
        grid_spec=pltpu.PrefetchScalarGridSpec(
            num_scalar_prefetch=1,
            grid=(nseq,),
            in_specs=[pl.BlockSpec(memory_space=pl.ANY)] + cspecs,
            out_specs=[pl.BlockSpec((1, nc, LANES), lambda b, pt: (b, 0, 0))] * 2,
            scratch_shapes=[pltpu.VMEM((2, npages, 2, KV_WIDTH, page_size), F32), pltpu.SemaphoreType.DMA((2,)),
                            pltpu.VMEM((CMP_PARTS, past // CMP_PARTS, KV_WIDTH), F32),
                            pltpu.VMEM((CMP_PARTS, past // CMP_PARTS, KV_WIDTH), F32),
                            pltpu.VMEM((nc + 8, NSA_KV_HEADS * CMP_HIDDEN), F32)],
        ),
        out_shape=[out, out],
        compiler_params=_cparams("arbitrary"),
    )(page_table.reshape(-1), pool_t, *cw)


def _bucket_thresholds():
    n = np.arange(0, 2 * REL_MAX_DIST)
    exact = REL_BUCKETS // 2
    out = []
    for ft in (np.float32, np.float64):
        nf = np.maximum(n, exact).astype(ft)
        large = exact + (np.log(nf / ft(exact)) / ft(math.log(REL_MAX_DIST / exact)) * ft(REL_BUCKETS - exact)).astype(np.int64)
        out.append(np.where(n < exact, n, np.minimum(large, REL_BUCKETS - 1)))
    assert (out[0] == out[1]).all() and out[1][REL_MAX_DIST] == REL_BUCKETS - 1
    return [int(np.argmax(out[1] >= b)) for b in range(1, REL_BUCKETS)]


_THR = _bucket_thresholds()


def _bias_tables(dist, rb_ref, heads):
    steps = [jnp.where(dist >= _THR[b - 1], 1.0, 0.0) for b in range(1, REL_BUCKETS)]
    far = dist >= REL_MAX_DIST
    out = []
    for h in heads:
        acc = jnp.full(dist.shape, rb_ref[0, h] - rb_ref[REL_BUCKETS - 1, h], F32)
        for b in range(1, REL_BUCKETS):
            acc = acc + steps[b - 1] * (rb_ref[b, h] - rb_ref[b - 1, h])
        out.append(jnp.where(dist < 0, NEG, jnp.where(far, 0.0, acc)))
    return out


BIG = 1e30
KT = 256
SLC_UNROLL = 8
N_BAND = 20
BAND_LO = 12
F_STEP = 2 * N_BAND
N_WK = 6
N_SK = 4
WIN_TILES = WINDOW // Q_BLOCK + 1


def _nsa_tables(rb_ref, st_ref, wt_ref, qft_ref, ov_ref, ncp):
    tq = Q_BLOCK
    heads = range(NSA_HEADS)
    t2 = lax.broadcasted_iota(jnp.int32, (tq, KT), 0)
    c2 = lax.broadcasted_iota(jnp.int32, (tq, KT), 1)
    for dd in range(N_SK - 1):
        tabs = _bias_tables(tq * dd + t2 - c2, rb_ref, heads)
        for h in heads:
            st_ref[h * N_SK + dd] = tabs[h]
    t1 = lax.broadcasted_iota(jnp.int32, (tq, LANES), 0)
    c1 = lax.broadcasted_iota(jnp.int32, (tq, LANES), 1)
    zeros = jnp.zeros((tq, LANES), F32)
    for d in range(2):
        tabs = _bias_tables(tq * d + t1 - c1, rb_ref, heads)
        for h in heads:
            wt_ref[h * N_WK + d] = tabs[h]
    for h in heads:
        st_ref[h * N_SK + N_SK - 1] = jnp.zeros((tq, KT), F32)
        wt_ref[h * N_WK + 2] = zeros
        wt_ref[h * N_WK + 3] = zeros
        wt_ref[h * N_WK + 4] = jnp.where(c1 > t1, 0.0, NEG)
        wt_ref[h * N_WK + 5] = zeros + NEG
    for g in range(NSA_KV_HEADS):
        f = c1 - (HEAD_DIM if g == 0 else 0)
        is_lo = f >= N_BAND
        u = jnp.where(is_lo, f - N_BAND, f) - BAND_LO
        live = (f >= 0) & (f < 2 * N_BAND)
        tabs = _bias_tables(t1 - CMP_STRIDE * u - (CMP_BLOCK - 1), rb_ref, range(g * NSA_GROUP, (g + 1) * NSA_GROUP))
        for r in range(NSA_GROUP):
            hi = tabs[r].astype(jnp.bfloat16).astype(F32)
            lo = jnp.where(tabs[r] <= NEG, 0.0, tabs[r] - hi)
            val = jnp.where(live, jnp.where(is_lo, lo, hi), 0.0)
            qft_ref[g * NSA_GROUP + r] = jnp.where(f == F_STEP, NEG, val)
    kk = lax.broadcasted_iota(jnp.int32, (ncp, LANES), 0)
    ss = lax.broadcasted_iota(jnp.int32, (ncp, LANES), 1)
    ratio = SLC_BLOCK // CMP_STRIDE
    nover = (CMP_BLOCK - 1) // CMP_STRIDE
    ov = (kk >= ratio * ss - nover) & (kk < ratio * ss + ratio) & (kk < ncp - 1)
    ov_ref[...] = jnp.where(ov, 1.0, 0.0).astype(ov_ref.dtype)


def _lane_half(rows):
    return lax.broadcasted_iota(jnp.int32, (rows, LANES), 1) // HEAD_DIM


def _place_heads(qg, g):
    keep = _lane_half(qg.shape[0]) == g
    out = []
    for r in range(NSA_GROUP):
        v = qg[:, (r // 2) * LANES:(r // 2 + 1) * LANES]
        v = jnp.where(g == r % 2, v, pltpu.roll(v, HEAD_DIM, 1))
        out.append(jnp.where(keep, v, 0.0))
    return out


def _unplace_heads(ohs, g):
    keep = _lane_half(ohs[0].shape[0]) == g
    cols = []
    for c in range(NSA_GROUP // 2):
        pair = []
        for r in (2 * c, 2 * c + 1):
            oh = jnp.where(keep, ohs[r], 0.0)
            pair.append(jnp.where(g == r % 2, oh, pltpu.roll(oh, HEAD_DIM, 1)))
        cols.append(pair[0] + pair[1])
    return jnp.concatenate(cols, axis=1)


def _gate_mix(gates, r, o_cmp, o_slc, o_win, l_slc=None, l_win=None):
    c_slc = gates[:, 3 * r + 1:3 * r + 2]
    c_win = gates[:, 3 * r + 2:3 * r + 3]
    if l_slc is not None:
        c_slc = c_slc / l_slc
    if l_win is not None:
        c_win = c_win / l_win
    return gates[:, 3 * r:3 * r + 1] * o_cmp + c_slc * o_slc + c_win * o_win


def _top_blocks(imp, qpos_lane):
    ns = imp.shape[1]
    vals = jnp.concatenate([imp[:, c * LANES:(c + 1) * LANES].T for c in range(ns // LANES)], axis=0)
    srow = lax.broadcasted_iota(jnp.int32, vals.shape, 0)
    qblk = qpos_lane[0:1, :] // SLC_BLOCK
    forced = (srow == 0) | (srow == qblk) | (srow == qblk - 1)
    vals = jnp.where(forced, -2.0, vals)
    vals = jnp.where(srow <= qblk, vals, -1.0)
    srow_f = srow.astype(F32)
    sel = jnp.where(forced, 1.0, 0.0)
    for _ in range(SLC_TOPN - 3):
        mx = jnp.max(vals, axis=0, keepdims=True)
        idx = jnp.min(jnp.where(vals == mx, srow_f, float(ns)), axis=0, keepdims=True)
        pick = srow_f == idx
        sel = jnp.where(pick, 1.0, sel)
        vals = jnp.where(pick, -2.0, vals)
    sel = jnp.where(srow <= qblk, sel, 0.0)
    return jnp.concatenate([sel[c * LANES:(c + 1) * LANES].T for c in range(ns // LANES)], axis=1)


def _nsa_prompt_kernel(rb_ref, q_ref, gate_ref, kc_ref, vc_ref, ksa_ref, vsa_ref, kw_ref, vwa_ref, o_ref,
                       st_ref, wt_ref, qft_ref, ov_ref, lbuf_ref, m_ref, acc_ref, qs_ref, *, ncp):
    tq = Q_BLOCK
    i = pl.program_id(1)
    gw = NSA_GROUP * HEAD_DIM

    @pl.when((pl.program_id(0) == 0) & (i == 0))
    def _():
        _nsa_tables(rb_ref, st_ref, wt_ref, qft_ref, ov_ref, ncp)

    qpos_lane = i * tq + lax.broadcasted_iota(jnp.int32, (tq, LANES), 1)
    trow = lax.broadcasted_iota(jnp.int32, (NSA_GROUP * tq, 1), 0) % tq
    row_live = (i * tq + trow) >= CMP_BLOCK - 1
    kidx = lax.broadcasted_iota(jnp.int32, (ncp, LANES), 0)
    klane = lax.broadcasted_iota(jnp.int32, (ncp, LANES), 1)
    ngrp = i // (2 * SLC_UNROLL) + 1
    nplain = jnp.maximum(ngrp - 2, 0)
    j0 = jnp.maximum(i - (WIN_TILES - 1), 0)
    ws = pl.multiple_of(j0 * tq, tq)
    kinds = []
    for jj in range(WIN_TILES):
        d = i - j0 - jj
        kinds.append(jnp.where(d < 0, N_WK - 1, d))
    u12 = kidx - (tq // CMP_STRIDE) * i + BAND_LO

    o_cmp, accw = [], []
    for g in range(NSA_KV_HEADS):
        qp = _place_heads(q_ref[:, g * gw:(g + 1) * gw], g)
        heads = [g * NSA_GROUP + r for r in range(NSA_GROUP)]
        lw = _mm_nt(jnp.concatenate(qp, axis=0).astype(MXU), kw_ref[pl.ds(ws, WIN_TILES * tq), :])
        lw = lw + jnp.concatenate(
            [jnp.concatenate([wt_ref[h * N_WK + kinds[jj]] for jj in range(WIN_TILES)], axis=1) for h in heads], axis=0)
        ew = jnp.exp(lw - jnp.max(lw, axis=1, keepdims=True)).astype(MXU)
        accw.append(jnp.dot(ew, vwa_ref[pl.ds(ws, WIN_TILES * tq), :], preferred_element_type=F32))
        f = klane - HEAD_DIM * (1 - g)
        fa = jnp.where(f >= N_BAND, f - N_BAND, f)
        feat = ((fa == u12) & (f >= 0) & (f < 2 * N_BAND)) | ((f == F_STEP) & (u12 >= N_BAND - 1))
        ka = jnp.where(klane // HEAD_DIM == g, kc_ref[0], jnp.where(feat, 1.0, 0.0)).astype(MXU)
        qa = jnp.concatenate([qp[r] + qft_ref[heads[r]] for r in range(NSA_GROUP)], axis=0).astype(MXU)
        lc = _mm_nt(qa, ka)
        ec = jnp.exp(lc - jnp.max(lc, axis=1, keepdims=True))
        sc = jnp.sum(ec, axis=1, keepdims=True)
        pc = (ec * jnp.where(row_live, 1.0 / sc, 0.0)).astype(MXU)
        o_cmp.append(jnp.dot(pc, vc_ref[0].astype(MXU), preferred_element_type=F32))
        imp = sum(jnp.dot(pc[r * tq:(r + 1) * tq], ov_ref[...], preferred_element_type=F32) for r in range(NSA_GROUP))
        selneg = (_top_blocks(imp, qpos_lane) - 1.0) * BIG
        qs_ref[g] = jnp.concatenate([jnp.concatenate([qp[r], selneg], axis=1) for r in range(NSA_GROUP)],
                                    axis=0).astype(qs_ref.dtype)

    gk = SLC_UNROLL * KT

    def group_body(g, carry):
        m_ref[...] = jnp.full(m_ref.shape, NEG, F32)

        def stage(grp, lt):
            lbuf_ref[grp] = lt
            mg = lt[:, 0:LANES]
            for c in range(1, gk // LANES):
                mg = jnp.maximum(mg, lt[:, c * LANES:(c + 1) * LANES])
            m_ref[...] = jnp.maximum(m_ref[...], mg)

        def plain_group(grp, c):
            stage(grp, _mm_nt(qs_ref[g], ksa_ref[pl.ds(pl.multiple_of(grp * gk, gk), gk), :]))
            return c

        def near_group(grp, c):
            tabs = []
            for r in range(NSA_GROUP):
                row = []
                for u in range(SLC_UNROLL):
                    dd = i - 2 * (grp * SLC_UNROLL + u)
                    kind = jnp.where((dd < 0) | (dd >= N_SK), N_SK - 1, dd)
                    row.append(st_ref[(g * NSA_GROUP + r) * N_SK + kind])
                tabs.append(jnp.concatenate(row, axis=1))
            stage(grp, _mm_nt(qs_ref[g], ksa_ref[pl.ds(pl.multiple_of(grp * gk, gk), gk), :])
                  + jnp.concatenate(tabs, axis=0))
            return c

        lax.fori_loop(0, nplain, plain_group, 0)
        lax.fori_loop(nplain, ngrp, near_group, 0)
        m_ref[...] = jnp.broadcast_to(jnp.max(m_ref[...], axis=1, keepdims=True), m_ref.shape)
        acc_ref[g] = jnp.zeros(acc_ref.shape[1:], F32)

        def pv_group(grp, c):
            e = jnp.exp(lbuf_ref[grp] - jnp.concatenate([m_ref[...]] * (gk // LANES), axis=1)).astype(MXU)
            acc_ref[g] += jnp.dot(e, vsa_ref[pl.ds(pl.multiple_of(grp * gk, gk), gk), :], preferred_element_type=F32)
            return c

        lax.fori_loop(0, ngrp, pv_group, 0)
        return carry

    lax.fori_loop(0, NSA_KV_HEADS, group_body, 0)
    for g in range(NSA_KV_HEADS):
        gates = gate_ref[:, g * LANES:(g + 1) * LANES]
        ohs = []
        for r in range(NSA_GROUP):
            rows = slice(r * tq, (r + 1) * tq)
            ohs.append(_gate_mix(gates, r, o_cmp[g][rows], acc_ref[g, rows, 0:LANES], accw[g][rows, 0:LANES],
                                 acc_ref[g, rows, LANES:LANES + 1], accw[g][rows, LANES:LANES + 1]))
        o_ref[:, g * gw:(g + 1) * gw] = _unplace_heads(ohs, g)


def _nsa_prompt(rel_bias, q, gates, kc, vc, ksa, vsa, kw, vwa, nseq, seq_len):
    tq = Q_BLOCK
    nq = seq_len // tq
    ncp = kc.shape[1]
    gk = SLC_UNROLL * KT
    ng = NSA_KV_HEADS
    rows = NSA_GROUP * tq
    assert seq_len % gk == 0 and seq_len >= WIN_TILES * tq and ncp % LANES == 0
    per_seq = lambda width: pl.BlockSpec((seq_len, width), lambda b, i: (b, 0), pipeline_mode=pl.Buffered(1))
    blk = lambda width: pl.BlockSpec((tq, width), lambda b, i: (b * nq + i, 0))
    cmp_spec = pl.BlockSpec((1, ncp, LANES), lambda b, i: (b, 0, 0))
    return pl.pallas_call(
        functools.partial(_nsa_prompt_kernel, ncp=ncp),
        grid=(nseq, nq),
        in_specs=[pl.BlockSpec(memory_space=pltpu.SMEM), blk(NSA_INNER), blk(ng * LANES), cmp_spec, cmp_spec,
                  per_seq(2 * LANES), per_seq(2 * LANES), per_seq(LANES), per_seq(2 * LANES)],
        out_specs=blk(NSA_INNER),
        out_shape=jax.ShapeDtypeStruct((nseq * seq_len, NSA_INNER), F32),
        scratch_shapes=[pltpu.VMEM((NSA_HEADS * N_SK, tq, KT), F32), pltpu.VMEM((NSA_HEADS * N_WK, tq, LANES), F32),
                        pltpu.VMEM((NSA_HEADS, tq, LANES), F32), pltpu.VMEM((ncp, LANES), MXU),
                        pltpu.VMEM((seq_len // gk, rows, gk), F32), pltpu.VMEM((rows, LANES), F32),
                        pltpu.VMEM((ng, rows, 2 * LANES), F32), pltpu.VMEM((ng, rows, 2 * LANES), MXU)],
        compiler_params=_cparams("arbitrary", "arbitrary"),
    )(rel_bias, q, gates, kc, vc, ksa, vsa, kw, vwa)


TAIL = 2 * LANES


def _nsa_sample_tables(rb_ref, cbt_ref, tail_ref, wtab_ref, ov_ref, eall_ref, *, ls, past, ncp, wk, wkp, nk):
    heads = range(NSA_HEADS)

    def per_group(tabs):
        return [jnp.concatenate(tabs[g * NSA_GROUP:(g + 1) * NSA_GROUP], axis=0) for g in range(NSA_KV_HEADS)]

    t = lax.broadcasted_iota(jnp.int32, (ls, ncp), 0)
    k = lax.broadcasted_iota(jnp.int32, (ls, ncp), 1)
    dist = past + t - CMP_STRIDE * k - (CMP_BLOCK - 1)
    dist = jnp.where(k < ncp - 1, dist, -1)
    for g, tab in enumerate(per_group(_bias_tables(dist, rb_ref, heads))):
        cbt_ref[g] = tab
    t = lax.broadcasted_iota(jnp.int32, (ls, TAIL), 0)
    c = lax.broadcasted_iota(jnp.int32, (ls, TAIL), 1) + (nk - TAIL)
    dist = jnp.where(c < past + ls, past + t - c, -1)
    for g, tab in enumerate(per_group(_bias_tables(dist, rb_ref, heads))):
        tail_ref[g] = tab
    t = lax.broadcasted_iota(jnp.int32, (ls, wkp), 0)
    c = lax.broadcasted_iota(jnp.int32, (ls, wkp), 1)
    dist = past + t - (past - wk + c)
    dist = jnp.where((c < wk + ls) & (dist < WINDOW), dist, -1)
    for g, tab in enumerate(per_group(_bias_tables(dist, rb_ref, heads))):
        wtab_ref[g] = tab
    kk = lax.broadcasted_iota(jnp.int32, (ncp, 2 * LANES), 0)
    ss = lax.broadcasted_iota(jnp.int32, (ncp, 2 * LANES), 1)
    ratio = SLC_BLOCK // CMP_STRIDE
    nover = (CMP_BLOCK - 1) // CMP_STRIDE
    ov = (kk >= ratio * ss - nover) & (kk < ratio * ss + ratio) & (kk < ncp - 1)
    ov_ref[...] = jnp.where(ov, 1.0, 0.0).astype(ov_ref.dtype)
    s2 = lax.broadcasted_iota(jnp.int32, (2 * LANES, nk), 0)
    c2 = lax.broadcasted_iota(jnp.int32, (2 * LANES, nk), 1)
    eall_ref[...] = jnp.where(c2 // SLC_BLOCK == s2, 1.0, 0.0).astype(eall_ref.dtype)


def _softmax_rows(l):
    m = jnp.max(l, axis=1, keepdims=True)
    e = jnp.exp(l - m)
    return e / jnp.sum(e, axis=1, keepdims=True)


SAMPLE_SPS = 2


def _t128(x):
    rows = x.shape[0]
    if rows < LANES:
        x = jnp.concatenate([x, jnp.zeros((LANES - rows, LANES), x.dtype)], axis=0)
    return x.T


def _nsa_sample_kernel(pt_ref, rb_ref, q_ref, gate_ref, kvs_ref, kvw_ref, kc_ref, vc_ref, win_ref, pool_ref,
                       o_ref, kt_ref, vt_ref, sem_ref, cbt_ref, tail_ref, wtab_ref, ov_ref, eall_ref,
                       *, ls, past, npages, page_size, ncp, wk, wkp, nk):
    b = pl.program_id(0)

    @pl.when(b == 0)
    def _():
        _nsa_sample_tables(rb_ref, cbt_ref, tail_ref, wtab_ref, ov_ref, eall_ref,
                           ls=ls, past=past, ncp=ncp, wk=wk, wkp=wkp, nk=nk)

    sps = kt_ref.shape[1]

    def page_copies(step, slot, p):
        cols = pl.ds(pl.multiple_of(p * page_size, page_size), page_size)
        cps = []
        for j in range(sps):
            pg = pt_ref[(step * sps + j) * npages + p]
            cps.append(pltpu.make_async_copy(pool_ref.at[pg, 0], kt_ref.at[slot, j, :, cols], sem_ref.at[0, slot]))
            cps.append(pltpu.make_async_copy(pool_ref.at[pg, 1], vt_ref.at[slot, j, :, cols], sem_ref.at[1, slot]))
        return cps

    slot = _gather_step(page_copies, npages)
    gw = NSA_GROUP * HEAD_DIM
    ng = NSA_KV_HEADS
    nrow = ng * NSA_GROUP * ls
    stack = lambda xs: xs[0] if len(xs) == 1 else jnp.concatenate(xs, axis=0)
    seq_rows = lambda x, j: x[j * nrow:(j + 1) * nrow]
    per_group = lambda ref: stack([ref[g] for g in range(ng)] * sps)
    qall, kwt, vwt = [], [], []
    for j in range(sps):
        tok = slice(j * ls, (j + 1) * ls)
        kvs = kvs_ref[tok, :]
        kt_ref[slot, j, :, past:nk] = _t128(kvs[:, 0:KV_WIDTH])
        vt_ref[slot, j, :, past:nk] = _t128(kvs[:, KV_WIDTH:])
        kvw = kvw_ref[tok, :]
        kwt.append(jnp.concatenate([win_ref[j, 0], _t128(kvw[:, 0:KV_WIDTH])], axis=1).astype(MXU))
        vwt.append(jnp.concatenate([win_ref[j, 1], _t128(kvw[:, KV_WIDTH:])], axis=1).astype(MXU))
        qall.append(stack([t for g in range(ng) for t in _place_heads(q_ref[tok, g * gw:(g + 1) * gw], g)]).astype(MXU))
    pc = _softmax_rows(stack([_mm_nt(qall[j], kc_ref[j]) for j in range(sps)]) + per_group(cbt_ref)).astype(MXU)
    o_cmp = [jnp.dot(seq_rows(pc, j), vc_ref[j].astype(MXU), preferred_element_type=F32) for j in range(sps)]
    impm = jnp.dot(pc, ov_ref[...], preferred_element_type=F32)
    nsel = sps * ng * ls
    imp = stack([sum(impm[((j * ng + g) * NSA_GROUP + r) * ls:((j * ng + g) * NSA_GROUP + r + 1) * ls]
                     for r in range(NSA_GROUP)) for j in range(sps) for g in range(ng)])
    imp = jnp.concatenate([imp, jnp.zeros((LANES - nsel, imp.shape[1]), F32)], axis=0)
    qpos_lane = past + lax.broadcasted_iota(jnp.int32, (1, LANES), 1) % ls
    selneg = ((_top_blocks(imp, qpos_lane)[0:nsel] - 1.0) * BIG).astype(MXU)
    selneg = stack([selneg[jg * ls:(jg + 1) * ls] for jg in range(sps * ng) for _ in range(NSA_GROUP)])
    mask = jnp.dot(selneg, eall_ref[...], preferred_element_type=F32)
    lsl = stack([jnp.dot(qall[j], kt_ref[slot, j].astype(MXU), preferred_element_type=F32) for j in range(sps)]) + mask
    lsl = jnp.concatenate([lsl[:, 0:nk - TAIL], lsl[:, nk - TAIL:] + per_group(tail_ref)], axis=1)
    ps = _softmax_rows(lsl)
    o_slc = [_mm_nt(seq_rows(ps, j), vt_ref[slot, j]) for j in range(sps)]
    pw = _softmax_rows(stack([jnp.dot(qall[j], kwt[j], preferred_element_type=F32) for j in range(sps)])
                       + per_group(wtab_ref))
    o_win = [_mm_nt(seq_rows(pw, j), vwt[j]) for j in range(sps)]
    for j in range(sps):
        tok = slice(j * ls, (j + 1) * ls)
        for g in range(ng):
            gates = gate_ref[tok, g * LANES:(g + 1) * LANES]
            ohs = []
            for r in range(NSA_GROUP):
                rows = slice((g * NSA_GROUP + r) * ls, (g * NSA_GROUP + r + 1) * ls)
                ohs.append(_gate_mix(gates, r, o_cmp[j][rows], o_slc[j][rows], o_win[j][rows]))
            o_ref[tok, g * gw:(g + 1) * gw] = _unplace_heads(ohs, g)


def _nsa_sample(rel_bias, q, gates, kvs, kvw, kc, vc, win_t, pool_t, page_table, ls):
    nseq, npages = page_table.shape
    page_size = pool_t.shape[3]
    past = npages * page_size
    ncp = kc.shape[1]
    wk = win_t.shape[3]
    wkp = wk + LANES
    nk = past + LANES
    assert past % SLC_BLOCK == 0 and ls % 8 == 0 and ls <= SLC_BLOCK and past >= wk and past // SLC_BLOCK + 1 <= 2 * LANES
    assert ncp % LANES == 0 and past >= CMP_BLOCK and page_size % LANES == 0 and wk % LANES == 0 and KV_WIDTH == LANES
    sps = SAMPLE_SPS if nseq % SAMPLE_SPS == 0 else 1
    row = lambda width: pl.BlockSpec((sps * ls, width), lambda b, pt: (b, 0))
    seq3 = lambda n, width: pl.BlockSpec((sps, n, width), lambda b, pt: (b, 0, 0))
    kw = 2 * KV_WIDTH
    nrow = NSA_GROUP * ls
    return pl.pallas_call(
        functools.partial(_nsa_sample_kernel, ls=ls, past=past, npages=npages, page_size=page_size, ncp=ncp,
                          wk=wk, wkp=wkp, nk=nk),
        grid_spec=pltpu.PrefetchScalarGridSpec(
            num_scalar_prefetch=1,
            grid=(nseq // sps,),
            in_specs=[pl.BlockSpec(memory_space=pltpu.SMEM), row(NSA_INNER), row(NSA_KV_HEADS * LANES), row(kw), row(kw),
                      seq3(ncp, LANES), seq3(ncp, LANES),
                      pl.BlockSpec((sps, 2, KV_WIDTH, wk), lambda b, pt: (b, 0, 0, 0)), pl.BlockSpec(memory_space=pl.ANY)],
            out_specs=row(NSA_INNER),
            scratch_shapes=[pltpu.VMEM((2, sps, KV_WIDTH, nk), F32), pltpu.VMEM((2, sps, KV_WIDTH, nk), F32),
                            pltpu.SemaphoreType.DMA((2, 2)),
                            pltpu.VMEM((NSA_KV_HEADS, nrow, ncp), F32), pltpu.VMEM((NSA_KV_HEADS, nrow, TAIL), F32),
                            pltpu.VMEM((NSA_KV_HEADS, nrow, wkp), F32),
                            pltpu.VMEM((ncp, 2 * LANES), MXU), pltpu.VMEM((2 * LANES, nk), MXU)],
        ),
        out_shape=jax.ShapeDtypeStruct((nseq * ls, NSA_INNER), F32),
        compiler_params=_cparams("arbitrary"),
    )(page_table.reshape(-1), rel_bias, q, gates, kvs, kvw, kc, vc, win_t, pool_t)


def _mix_out_kernel(x_ref, yssd_ref, onsa_ref, gn_ref, wo_ref, gq_ref, wq_ref, mkv_ref, wmo_ref, o_ref, *, rows_per_seq):
    y = jnp.concatenate([yssd_ref[...], _rms(onsa_ref[...], gn_ref[...])], axis=1)
    x = x_ref[...] + jnp.dot(y.astype(MXU), wo_ref[...], preferred_element_type=F32)
    qm = jnp.dot(_rms(x, gq_ref[...]).astype(MXU), wq_ref[...], preferred_element_type=F32)
    spb, m = mkv_ref.shape[0:2]
    kv = mkv_ref[...].reshape(spb * m, 2 * MEM_INNER)
    km = kv[:, 0:MEM_INNER].astype(MXU)
    vm = kv[:, MEM_INNER:].astype(MXU)
    head = lax.broadcasted_iota(jnp.int32, qm.shape, 1) // HEAD_DIM
    if spb > 1:
        rseq = lax.broadcasted_iota(jnp.int32, (qm.shape[0], spb * m), 0) // rows_per_seq
        kseq = lax.broadcasted_iota(jnp.int32, (qm.shape[0], spb * m), 1) // m
        own = rseq == kseq
    o = jnp.zeros(qm.shape, F32)
    for h in range(MEM_HEADS):
        lg = _mm_nt(jnp.where(head == h, qm, 0.0), km) * (HEAD_DIM ** -0.5)
        if spb > 1:
            lg = jnp.where(own, lg, NEG)
        o = o + jnp.where(head == h, jnp.dot(_softmax_rows(lg).astype(MXU), vm, preferred_element_type=F32), 0.0)
    o_ref[...] = x + jnp.dot(o.astype(MXU), wmo_ref[...], preferred_element_type=F32)


def _mix_out(x, yssd, onsa, g_nsa, w_out, g_memq, w_mem_q, mem_kv, w_mem_out, nseq, seq_len, tm=512, short_rows=64):
    t, d = x.shape
    m = mem_kv.shape[1]
    if seq_len < short_rows and nseq % (short_rows // seq_len) == 0:
        spb = short_rows // seq_len
        tm, nt = spb * seq_len, 1
    else:
        spb, tm = 1, min(tm, seq_len)
        nt = seq_len // tm
        assert nt * tm == seq_len
    row = lambda width: pl.BlockSpec((tm, width), lambda b, i: (b * nt + i, 0))
    return pl.pallas_call(
        functools.partial(_mix_out_kernel, rows_per_seq=seq_len),
        grid=(nseq // spb, nt),
        in_specs=[row(d), row(SSD_INNER), row(NSA_INNER), _const_spec((1, NSA_INNER)), _const_spec(w_out.shape),
                  _const_spec((1, d)), _const_spec(w_mem_q.shape),
                  pl.BlockSpec((spb, m, 2 * MEM_INNER), lambda b, i: (b, 0, 0)), _const_spec(w_mem_out.shape)],
        out_specs=row(d),
        out_shape=jax.ShapeDtypeStruct((t, d), F32),
        compiler_params=_cparams("parallel", "parallel"),
    )(x, yssd, onsa, g_nsa.reshape(1, -1), w_out.astype(MXU), g_memq.reshape(1, d), w_mem_q.astype(MXU),
      mem_kv, w_mem_out.astype(MXU))


def _norm_matmul_kernel(x_ref, g_ref, w_ref, o_ref):
    o_ref[...] = jnp.dot(_rms(x_ref[...], g_ref[...]).astype(MXU), w_ref[...], preferred_element_type=F32)


def _norm_matmul(x, g, w, tm=256):
    t, d = x.shape
    n = w.shape[1]
    tm = min(tm, t)
    assert t % tm == 0
    return pl.pallas_call(
        _norm_matmul_kernel,
        grid=(t // tm,),
        in_specs=[pl.BlockSpec((tm, d), lambda i: (i, 0)), _const_spec((1, d)), _const_spec(w.shape)],
        out_specs=pl.BlockSpec((tm, n), lambda i: (i, 0)),
        out_shape=jax.ShapeDtypeStruct((t, n), F32),
        compiler_params=_cparams("parallel"),
    )(x, g.reshape(1, d), w.astype(MXU))


def _trunk_layer(x, lw, cw, rel_bias, mem_kv, conv0, h0, nseq, seq_len, sample=None):
    x1 = _ffn(x, lw['norm_ffn1'], lw['w_ffn1_in'], lw['w_ffn1_out'])
    outs = _proj(x1, lw['norm_mix'], lw['w_mix'], seq_len=None if sample else seq_len)
    z, xbc, q, kvc, kvs, kvw, gates, dt = outs[:8]
    y_ssd, h_new, conv_new = _ssd(z, xbc, dt, conv0, h0, lw['conv_w'], lw['conv_b'], lw['dt_bias'], lw['a_log'],
                                  lw['d_skip'], lw['norm_ssd_out'], nseq, seq_len)
    rows5 = lambda a: a.reshape(nseq, seq_len, 2, NSA_KV_HEADS, HEAD_DIM)
    if sample is None:
        kc, vc = _compress_prompt(kvc, cw, nseq, seq_len)
        o_nsa = _nsa_prompt(rel_bias, q, gates, kc, vc, *outs[8:12], nseq, seq_len)
        from_t = lambda a: jnp.transpose(a.reshape(nseq, 2, NSA_KV_HEADS, HEAD_DIM, seq_len), (0, 4, 1, 2, 3))
        kv_c, kv_s = from_t(outs[12]), from_t(outs[13])
    else:
        kc, vc = _compress_sample(sample['pool_c'], sample['page_table'], cw)
        o_nsa = _nsa_sample(rel_bias, q, gates, kvs, kvw, kc, vc, sample['win_t'], sample['pool_s'],
                            sample['page_table'], seq_len)
        kv_c, kv_s = rows5(kvc), rows5(kvs)
    x2 = _mix_out(x1, y_ssd, o_nsa, lw['norm_nsa_out'], lw['w_out'], lw['norm_mem_q'], lw['w_mem_q'], mem_kv,
                  lw['w_mem_out'], nseq, seq_len)
    keep = min(WINDOW, seq_len)
    kv_w = kvw.reshape(nseq, seq_len, -1)[:, seq_len - keep:].reshape(nseq, keep, 2, NSA_KV_HEADS, HEAD_DIM)
    return x2, kv_c, kv_s, kv_w, h_new, conv_new


def kernel(x_prompt, x_sample, cache_kv_cmp, cache_kv_slc, state_win_kv, state_ssd, state_conv, cache_mem_kv, page_table, mem_prompt, norm_ffn1, w_ffn1_in, w_ffn1_out, norm_mix, w_in, conv_w, conv_b, dt_bias, a_log, d_skip, norm_ssd_out, cmp_pos_k, cmp_w1_k, cmp_w2_k, cmp_pos_v, cmp_w1_v, cmp_w2_v, norm_nsa_out, w_out, norm_mem_q, norm_mem_kv, w_mem_q, w_mem_kv, w_mem_out, norm_ffn2, w_ffn2_in, w_ffn2_out, rel_bias, norm_final):
    bp, lp, d = x_prompt.shape
    bs, lsq, _ = x_sample.shape
    depth = w_in.shape[0]
    xp = x_prompt.reshape(bp * lp, d)
    xs = x_sample.reshape(bs * lsq, d)
    outs_p = [[] for _ in range(6)]
    outs_s = [[] for _ in range(5)]
    for layer in range(depth):
        last = layer == depth - 1
        lw = {
            'norm_ffn1': norm_ffn1[layer], 'w_ffn1_in': w_ffn1_in[layer], 'w_ffn1_out': w_ffn1_out[layer],
            'norm_mix': norm_mix[layer], 'w_mix': _regroup_w_in(w_in[layer]), 'conv_w': conv_w[layer],
            'conv_b': conv_b[layer], 'dt_bias': dt_bias[layer], 'a_log': a_log[layer], 'd_skip': d_skip[layer],
            'norm_ssd_out': norm_ssd_out[layer], 'norm_nsa_out': norm_nsa_out[layer], 'w_out': w_out[layer],
            'norm_mem_q': norm_mem_q[layer], 'w_mem_q': w_mem_q[layer], 'w_mem_out': w_mem_out[layer],
        }
        cw = _compress_weights(cmp_pos_k[layer], cmp_w1_k[layer], cmp_w2_k[layer],
                               cmp_pos_v[layer], cmp_w1_v[layer], cmp_w2_v[layer])
        g_final = norm_final if last else None
        mem_kv_p = _norm_matmul(mem_prompt.reshape(-1, d), norm_mem_kv[layer], w_mem_kv[layer])
        mem_kv_p = mem_kv_p.reshape(bp, -1, 2 * MEM_INNER)
        conv0 = jnp.zeros((bp, SSD_CONV - 1, SSD_CONV_DIM), F32)
        h0 = jnp.zeros((bp, SSD_HEADS, HEAD_DIM, SSD_STATE), F32)
        xp, c_p, s_p, w_p, h_p, conv_p = _trunk_layer(xp, lw, cw, rel_bias, mem_kv_p, conv0, h0, bp, lp)
        xp = _ffn(xp, norm_ffn2[layer], w_ffn2_in[layer], w_ffn2_out[layer], g_final)
        sample = dict(pool_c=_paged_view(cache_kv_cmp[layer]), pool_s=_paged_view(cache_kv_slc[layer]),
                      win_t=_paged_view(state_win_kv[layer]), page_table=page_table)
        mem_kv_s = cache_mem_kv[layer].reshape(bs, -1, 2 * MEM_INNER)
        xs, c_s, s_s, w_s, h_s, conv_s = _trunk_layer(xs, lw, cw, rel_bias, mem_kv_s, state_conv[layer],
                                                      state_ssd[layer], bs, lsq, sample=sample)
        xs = _ffn(xs, norm_ffn2[layer], w_ffn2_in[layer], w_ffn2_out[layer], g_final)
        w_s = jnp.concatenate([state_win_kv[layer], w_s], axis=1)[:, w_s.shape[1]:]
        for lst, v in zip(outs_p, (c_p, s_p, w_p, h_p, conv_p, mem_kv_p.reshape(bp, -1, 2, MEM_HEADS, HEAD_DIM))):
            lst.append(v)
        for lst, v in zip(outs_s, (c_s, s_s, w_s, h_s, conv_s)):
            lst.append(v)
    return (xp.reshape(bp, lp, d), xs.reshape(bs, lsq, d),
            *(jnp.stack(o) for o in outs_p), *(jnp.stack(o) for o in outs_s))
```

```python
import functools
import math

import numpy as np
import jax
import jax.numpy as jnp
from jax import lax
from jax.experimental import pallas as pl
from jax.experimental.pallas import tpu as pltpu

F32 = jnp.float32
MXU = jnp.bfloat16

HEAD_DIM = 64
SSD_HEADS = 8
SSD_GROUPS = 2
SSD_STATE = 128
SSD_CONV = 4
SSD_CHUNK = 128
NSA_HEADS = 8
NSA_KV_HEADS = 2
NSA_GROUP = NSA_HEADS // NSA_KV_HEADS
CMP_BLOCK = 32
CMP_STRIDE = 16
CMP_HIDDEN = 256
SLC_BLOCK = 64
SLC_TOPN = 16
WINDOW = 512
REL_BUCKETS = 32
REL_MAX_DIST = 128
MEM_HEADS = 4
Q_BLOCK = 128
EPS = 1e-6
NEG = -1e30

SSD_INNER = SSD_HEADS * HEAD_DIM
NSA_INNER = NSA_HEADS * HEAD_DIM
KV_WIDTH = NSA_KV_HEADS * HEAD_DIM
SSD_CONV_DIM = SSD_INNER + 2 * SSD_GROUPS * SSD_STATE
MEM_INNER = MEM_HEADS * HEAD_DIM
LANES = 128
VMEM_LIMIT = 56 * 1024 * 1024


def _cparams(*sem):
    return pltpu.CompilerParams(dimension_semantics=sem, vmem_limit_bytes=VMEM_LIMIT)


def _const_spec(shape):
    nd = len(shape)
    return pl.BlockSpec(shape, lambda *a: (0,) * nd, pipeline_mode=pl.Buffered(1))


def _mm(a, b):
    return jnp.dot(a.astype(MXU), b.astype(MXU), preferred_element_type=F32)


def _mm_nt(a, b):
    return lax.dot_general(a.astype(MXU), b.astype(MXU), (((1,), (1,)), ((), ())), preferred_element_type=F32)


def _rms(x, g):
    return x * lax.rsqrt(jnp.mean(x * x, axis=-1, keepdims=True) + EPS) * g


def _silu(x):
    return x * jax.nn.sigmoid(x)


FFN_CHUNK = 256


def _ffn_apply(x_ref, g_ref, win_ref, wout_ref, gf_ref, o_ref, h_ref):
    final_norm = gf_ref is not None
    nchunks = wout_ref.shape[0] // FFN_CHUNK
    x = x_ref[...]
    h_ref[...] = _rms(x, g_ref[...]).astype(h_ref.dtype)
    f = nchunks * FFN_CHUNK
    for j in range(nchunks):
        cols = slice(j * FFN_CHUNK, (j + 1) * FFN_CHUNK)
        h = h_ref[...]
        gate = jnp.dot(h, win_ref[:, cols], preferred_element_type=F32)
        up = jnp.dot(h, win_ref[:, f + j * FFN_CHUNK:f + (j + 1) * FFN_CHUNK], preferred_element_type=F32)
        a = (_silu(gate) * up).astype(h.dtype)
        t = jnp.dot(a, wout_ref[cols, :], preferred_element_type=F32)
        if j == 0:
            o_ref[...] = t
        else:
            o_ref[...] += t
    y = x_ref[...] + 0.5 * o_ref[...]
    if final_norm:
        y = _rms(y, gf_ref[...])
    o_ref[...] = y


def _ffn_kernel(x_ref, g_ref, win_ref, wout_ref, *rest, final_norm):
    gf_ref, o_ref, h_ref = rest if final_norm else (None,) + rest
    _ffn_apply(x_ref, g_ref, win_ref, wout_ref, gf_ref, o_ref, h_ref)


def _ffn(x, g, w_in, w_out, g_final=None, tm=512):
    t, d = x.shape
    f = w_out.shape[0]
    nchunks = f // FFN_CHUNK
    tm = min(tm, t)
    assert nchunks * FFN_CHUNK == f and t % tm == 0
    win = w_in.astype(MXU)
    wout = w_out.astype(MXU)
    final_norm = g_final is not None
    args = [x, g.reshape(1, d), win, wout]
    specs = [pl.BlockSpec((tm, d), lambda i: (i, 0)), _const_spec((1, d)),
             _const_spec(win.shape), _const_spec(wout.shape)]
    if final_norm:
        args.append(g_final.reshape(1, d))
        specs.append(_const_spec((1, d)))
    return pl.pallas_call(
        functools.partial(_ffn_kernel, final_norm=final_norm),
        grid=(t // tm,),
        in_specs=specs,
        out_specs=pl.BlockSpec((tm, d), lambda i: (i, 0)),
        out_shape=jax.ShapeDtypeStruct((t, d), F32),
        scratch_shapes=[pltpu.VMEM((tm, d), MXU)],
        compiler_params=_cparams("parallel"),
    )(*args)


PROJ_COLS = 3200
_O_Z, _O_XBC, _O_Q, _O_KVC, _O_KVS, _O_KVW, _O_G, _O_DT = 0, 512, 1536, 2048, 2304, 2560, 2816, 3072


def _regroup_w_in(w_in):
    o = np.cumsum([0, SSD_INNER, SSD_CONV_DIM, SSD_HEADS, NSA_INNER] + [KV_WIDTH] * 6 + [3 * NSA_HEADS])
    z, xbc, dt, q, kvs, gl = (w_in[:, o[0]:o[1]], w_in[:, o[1]:o[2]], w_in[:, o[2]:o[3]], w_in[:, o[3]:o[4]],
                              w_in[:, o[4]:o[10]], w_in[:, o[10]:o[11]])
    pad = lambda w: jnp.pad(w, ((0, 0), (0, LANES - w.shape[1])))
    ng = 3 * NSA_GROUP
    gls = [pad(gl[:, g * ng:(g + 1) * ng]) for g in range(NSA_KV_HEADS)]
    return jnp.concatenate([z, xbc, q, kvs] + gls + [pad(dt)], axis=1).astype(MXU)


def _proj_kernel(x_ref, g_ref, w_ref, z_ref, xbc_ref, q_ref, kvc_ref, kvs_ref, kvw_ref, gate_ref, dt_ref, *aug,
                 tiles_per_seq):
    h = _rms(x_ref[...], g_ref[...]).astype(w_ref.dtype)
    res = jnp.dot(h, w_ref[...], preferred_element_type=F32)
    tm = res.shape[0]
    z_ref[...] = res[:, _O_Z:_O_XBC]
    xbc_ref[...] = res[:, _O_XBC:_O_Q]
    q_ref[...] = res[:, _O_Q:_O_KVC] * (HEAD_DIM ** -0.5)
    kvc_ref[...] = res[:, _O_KVC:_O_KVS]
    kvs = res[:, _O_KVS:_O_KVW]
    kvw = res[:, _O_KVW:_O_G]
    kvs_ref[...] = kvs
    kvw_ref[...] = kvw
    gate_ref[...] = jax.nn.sigmoid(res[:, _O_G:_O_DT])
    dt_ref[...] = res[:, _O_DT:PROJ_COLS]
    if aug:
        ksa_ref, vsa_ref, kw_ref, vwa_ref, kvct_ref, kvst_ref = aug
        kvct_ref[0] = res[:, _O_KVC:_O_KVS].T
        kvst_ref[0] = kvs.T
        lane = lax.broadcasted_iota(jnp.int32, (tm, LANES), 1)
        row = lax.broadcasted_iota(jnp.int32, (tm, LANES), 0)
        pos = (pl.program_id(0) % tiles_per_seq) * tm + row
        onehot = jnp.where(lane == pos // SLC_BLOCK, 1.0, 0.0)
        ones_col = jnp.where(lane == 0, 1.0, 0.0)
        dt_ = ksa_ref.dtype
        ksa_ref[:, 0:LANES] = kvs[:, 0:LANES].astype(dt_)
        ksa_ref[:, LANES:] = onehot.astype(dt_)
        vsa_ref[:, 0:LANES] = kvs[:, LANES:].astype(dt_)
        vsa_ref[:, LANES:] = ones_col.astype(dt_)
        kw_ref[...] = kvw[:, 0:LANES].astype(dt_)
        vwa_ref[:, 0:LANES] = kvw[:, LANES:].astype(dt_)
        vwa_ref[:, LANES:] = ones_col.astype(dt_)


def _proj(x, g, w, seq_len=None, tm=512):
    t, d = x.shape
    tm = min(tm, t)
    assert t % tm == 0
    widths = [512, 1024, 512, 256, 256, 256, NSA_KV_HEADS * LANES, LANES]
    shapes = [jax.ShapeDtypeStruct((t, wd), F32) for wd in widths]
    specs = [pl.BlockSpec((tm, wd), lambda i: (i, 0)) for wd in widths]
    aug = seq_len is not None
    tps = seq_len // tm if aug else 1
    if aug:
        assert seq_len % tm == 0 and seq_len // SLC_BLOCK <= LANES
        for wd in (256, 256, LANES, 256):
            shapes.append(jax.ShapeDtypeStruct((t, wd), MXU))
            specs.append(pl.BlockSpec((tm, wd), lambda i: (i, 0)))
        for _ in range(2):
            shapes.append(jax.ShapeDtypeStruct((t // seq_len, 2 * KV_WIDTH, seq_len), F32))
            specs.append(pl.BlockSpec((1, 2 * KV_WIDTH, tm), lambda i: (i // tps, 0, i % tps)))
    return pl.pallas_call(
        functools.partial(_proj_kernel, tiles_per_seq=tps),
        grid=(t // tm,),
        in_specs=[pl.BlockSpec((tm, d), lambda i: (i, 0)), _const_spec((1, d)), _const_spec(w.shape)],
        out_specs=specs,
        out_shape=shapes,
        compiler_params=_cparams("parallel"),
    )(x, g.reshape(1, d), w)


HALO = 8


def _split3(x):
    hi = x.astype(jnp.bfloat16).astype(F32)
    r = x - hi
    mid = r.astype(jnp.bfloat16).astype(F32)
    return hi, mid, r - mid


def _ssd_kernel(z_ref, xbc_ref, dt_ref, conv0_ref, h0_ref, cw_ref, cb_ref, dtb_ref, alog_ref, dskip_ref, g_ref,
                y_ref, hout_ref, convout_ref, xwin_ref, h_ref, *, lv):
    q = SSD_CHUNK
    c = pl.program_id(1)

    @pl.when(c == 0)
    def _():
        xwin_ref[0:HALO, :] = conv0_ref[0]
        h_ref[...] = h0_ref[0]

    def padrows(v):
        if lv == q:
            return v
        return jnp.concatenate([v, jnp.zeros((q - lv, v.shape[1]), v.dtype)], axis=0)

    xwin_ref[HALO:HALO + q, :] = padrows(xbc_ref[...])
    acc = cb_ref[...] + cw_ref[SSD_CONV - 1:SSD_CONV, :] * xwin_ref[HALO:HALO + q, :]
    for k in range(SSD_CONV - 1):
        acc = acc + cw_ref[k:k + 1, :] * xwin_ref[pl.ds(HALO - (SSD_CONV - 1) + k, q), :]
    xc = _silu(acc)
    tail = xwin_ref[lv:lv + HALO, :]
    xwin_ref[0:HALO, :] = tail
    convout_ref[0] = tail

    xs = xc[:, 0:SSD_INNER]
    rowi = lax.broadcasted_iota(jnp.int32, (q, q), 0)
    coli = lax.broadcasted_iota(jnp.int32, (q, q), 1)
    causal = rowi >= coli
    tri = jnp.where(causal, 1.0, 0.0).astype(jnp.bfloat16)

    xdt = padrows(dt_ref[...]) + dtb_ref[...]
    dt = jnp.maximum(xdt, 0.0) + jnp.log1p(jnp.exp(-jnp.abs(xdt)))
    if lv != q:
        dt = jnp.where(rowi < lv, dt, 0.0)
    da = dt * (-jnp.exp(alog_ref[...]))
    acs = sum(jnp.dot(tri, p.astype(jnp.bfloat16), preferred_element_type=F32) for p in _split3(da))
    acs_t = acs.T
    dt_t = dt.T
    last = acs[q - 1:q, :]
    last_t = acs_t[:, q - 1:q]
    xs_t = [xs[:, p * LANES:(p + 1) * LANES].T for p in range(SSD_HEADS // 2)]
    lane_hi = lax.broadcasted_iota(jnp.int32, (q, LANES), 1) >= HEAD_DIM
    row_hi = lax.broadcasted_iota(jnp.int32, (LANES, q), 0) >= HEAD_DIM
    hpg = SSD_HEADS // SSD_GROUPS
    ys = []
    for g in range(SSD_GROUPS):
        bm = xc[:, SSD_INNER + g * SSD_STATE:SSD_INNER + (g + 1) * SSD_STATE]
        cm = xc[:, SSD_INNER + (SSD_GROUPS + g) * SSD_STATE:SSD_INNER + (SSD_GROUPS + g + 1) * SSD_STATE]
        cb = _mm_nt(cm, bm)
        for pp in range(hpg // 2):
            p = g * (hpg // 2) + pp
            xpair = xs[:, p * LANES:(p + 1) * LANES]
            yi = []
            for hh in range(2):
                h = 2 * p + hh
                seg = acs[:, h:h + 1] - acs_t[h:h + 1, :]
                decay = jnp.where(causal, jnp.exp(jnp.where(causal, seg, 0.0)), 0.0)
                yi.append(_mm(cb * decay * dt_t[h:h + 1, :], xpair))
            y_intra = jnp.where(lane_hi, yi[1], yi[0])
            h0, h1 = 2 * p, 2 * p + 1
            grow = jnp.where(lane_hi, jnp.exp(acs[:, h1:h1 + 1]), jnp.exp(acs[:, h0:h0 + 1]))
            y_inter = _mm_nt(cm, h_ref[p]) * grow
            ys.append(y_intra + y_inter)
            te = jnp.where(row_hi,
                           jnp.exp(last_t[h1:h1 + 1, :] - acs_t[h1:h1 + 1, :]) * dt_t[h1:h1 + 1, :],
                           jnp.exp(last_t[h0:h0 + 1, :] - acs_t[h0:h0 + 1, :]) * dt_t[h0:h0 + 1, :])
            states = _mm(xs_t[p] * te, bm)
            rdec = jnp.where(row_hi[:, 0:1], jnp.exp(last_t[h1:h1 + 1, :]), jnp.exp(last_t[h0:h0 + 1, :]))
            h_ref[p] = h_ref[p] * rdec + states
    y = jnp.concatenate(ys, axis=1) + dskip_ref[...] * xs
    y = _rms(y * _silu(padrows(z_ref[...])), g_ref[...])
    y_ref[...] = y[0:lv, :]
    hout_ref[0] = h_ref[...]


def _ssd(z, xbc, dt, conv0, h0, conv_w, conv_b, dt_bias, a_log, d_skip, norm_g, nseq, seq_len):
    q = SSD_CHUNK
    lv = min(q, seq_len)
    nch = seq_len // lv
    assert nch * lv == seq_len
    t = nseq * seq_len
    cdim = SSD_CONV_DIM
    keep = SSD_CONV - 1
    conv0p = jnp.pad(conv0, ((0, 0), (HALO - keep, 0), (0, 0)))
    padl = lambda v: jnp.pad(v.reshape(1, -1), ((0, 0), (0, LANES - v.size)))
    hp = h0.reshape(nseq, SSD_HEADS // 2, 2 * HEAD_DIM, SSD_STATE)
    row = lambda b, c: (b * nch + c, 0)
    y, hout, convout = pl.pallas_call(
        functools.partial(_ssd_kernel, lv=lv),
        grid=(nseq, nch),
        in_specs=[pl.BlockSpec((lv, SSD_INNER), row), pl.BlockSpec((lv, cdim), row), pl.BlockSpec((lv, LANES), row),
                  pl.BlockSpec((1, HALO, cdim), lambda b, c: (b, 0, 0)),
                  pl.BlockSpec((1,) + hp.shape[1:], lambda b, c: (b, 0, 0, 0)),
                  _const_spec((SSD_CONV, cdim)), _const_spec((1, cdim)), _const_spec((1, LANES)),
                  _const_spec((1, LANES)), _const_spec((1, SSD_INNER)), _const_spec((1, SSD_INNER))],
        out_specs=[pl.BlockSpec((lv, SSD_INNER), row),
                   pl.BlockSpec((1,) + hp.shape[1:], lambda b, c: (b, 0, 0, 0)),
                   pl.BlockSpec((1, HALO, cdim), lambda b, c: (b, 0, 0))],
        out_shape=[jax.ShapeDtypeStruct((t, SSD_INNER), F32), jax.ShapeDtypeStruct(hp.shape, F32),
                   jax.ShapeDtypeStruct((nseq, HALO, cdim), F32)],
        scratch_shapes=[pltpu.VMEM((HALO + q, cdim), F32), pltpu.VMEM(hp.shape[1:], F32)],
        compiler_params=_cparams("parallel", "arbitrary"),
    )(z, xbc, dt, conv0p, hp, conv_w, conv_b.reshape(1, cdim), padl(dt_bias), padl(a_log),
      jnp.repeat(d_skip, HEAD_DIM).reshape(1, SSD_INNER), norm_g.reshape(1, SSD_INNER))
    return y, hout.reshape(nseq, SSD_HEADS, HEAD_DIM, SSD_STATE), convout[:, HALO - keep:, :]


def _compress_weights(pe_k, w1_k, w2_k, pe_v, w1_v, w2_v):
    def w1_pair(w1):
        w = w1.reshape(CMP_BLOCK, HEAD_DIM, CMP_HIDDEN)
        zz = jnp.zeros_like(w)
        w = jnp.concatenate([jnp.concatenate([w, zz], axis=2), jnp.concatenate([zz, w], axis=2)], axis=1)
        return w.reshape(CMP_BLOCK // 2, 2 * KV_WIDTH, NSA_KV_HEADS * CMP_HIDDEN).astype(MXU)

    def w2_pair(w2):
        zz = jnp.zeros_like(w2)
        return jnp.concatenate([jnp.concatenate([w2, zz], axis=1), jnp.concatenate([zz, w2], axis=1)], axis=0).astype(MXU)

    pe = jnp.concatenate([pe_k, pe_k, pe_v, pe_v], axis=1)
    return pe, w1_pair(w1_k), w1_pair(w1_v), w2_pair(w2_k), w2_pair(w2_v)


def _compress_first_layer(row_of_chunks, pe_ref, wk_ref, wv_ref):
    half = CMP_STRIDE
    a0k = a1k = a0v = a1v = None
    for s2 in range(half // 2):
        rows = [row_of_chunks(2 * s2 + u) for u in range(2)]

        def lhs(j, kv):
            lanes = slice(kv * LANES, (kv + 1) * LANES)
            return jnp.concatenate([(rows[u][kv] + pe_ref[j * half + 2 * s2 + u:j * half + 2 * s2 + u + 1, lanes])
                                    for u in range(2)], axis=1).astype(MXU)

        t0k = jnp.dot(lhs(0, 0), wk_ref[s2], preferred_element_type=F32)
        t0v = jnp.dot(lhs(0, 1), wv_ref[s2], preferred_element_type=F32)
        t1k = jnp.dot(lhs(1, 0), wk_ref[half // 2 + s2], preferred_element_type=F32)
        t1v = jnp.dot(lhs(1, 1), wv_ref[half // 2 + s2], preferred_element_type=F32)
        a0k, a1k = (t0k, t1k) if s2 == 0 else (a0k + t0k, a1k + t1k)
        a0v, a1v = (t0v, t1v) if s2 == 0 else (a0v + t0v, a1v + t1v)
    return a0k, a1k, a0v, a1v


def _compress_compute(rows_of_parts, pe_ref, wk_ref, wv_ref, w2k_ref, w2v_ref, sh_ref, kc_ref, vc_ref, nc):
    parts = [_compress_first_layer(f, pe_ref, wk_ref, wv_ref) for f in rows_of_parts]
    a0k, a1k, a0v, a1v = (p[0] if len(parts) == 1 else jnp.concatenate(p, axis=0) for p in zip(*parts))
    sh_ref[nc:nc + 8, :] = jnp.zeros((8, sh_ref.shape[1]), F32)
    for a0, a1, w2_ref, o_ref in ((a0k, a1k, w2k_ref, kc_ref), (a0v, a1v, w2v_ref, vc_ref)):
        sh_ref[0:nc, :] = a1
        hid = a0 + sh_ref[pl.ds(1, nc), :]
        o_ref[...] = jnp.dot(_silu(hid).astype(MXU), w2_ref[...], preferred_element_type=F32)


def _compress_prompt_kernel(rows_ref, pe_ref, wk_ref, wv_ref, w2k_ref, w2v_ref, kc_ref, vc_ref, sh_ref, *, nc):
    rw = 2 * KV_WIDTH

    def row_of_chunks(s):
        return rows_ref[:, s * rw:s * rw + LANES], rows_ref[:, s * rw + LANES:(s + 1) * rw]

    _compress_compute([row_of_chunks], pe_ref, wk_ref, wv_ref, w2k_ref, w2v_ref, sh_ref, kc_ref.at[0], vc_ref.at[0], nc)


def _compress_specs(cw):
    return [_const_spec(w.shape) for w in cw]


def _compress_prompt(kv_cmp, cw, nseq, seq_len):
    nc = seq_len // CMP_STRIDE
    out = jax.ShapeDtypeStruct((nseq, nc, LANES), F32)
    return pl.pallas_call(
        functools.partial(_compress_prompt_kernel, nc=nc),
        grid=(nseq,),
        in_specs=[pl.BlockSpec((nc, CMP_STRIDE * 2 * KV_WIDTH), lambda b: (b, 0))] + _compress_specs(cw),
        out_specs=[pl.BlockSpec((1, nc, LANES), lambda b: (b, 0, 0))] * 2,
        out_shape=[out, out],
        scratch_shapes=[pltpu.VMEM((nc + 8, NSA_KV_HEADS * CMP_HIDDEN), F32)],
        compiler_params=_cparams("parallel"),
    )(kv_cmp.reshape(nseq * nc, CMP_STRIDE * 2 * KV_WIDTH), *cw)


PAGE_UNROLL = 8


def _gather_step(page_copies, npages):
    b = pl.program_id(0)
    nb = pl.num_programs(0)
    slot = b % 2
    assert npages % PAGE_UNROLL == 0

    def for_pages(fn):
        def body(q, c):
            for u in range(PAGE_UNROLL):
                fn(q * PAGE_UNROLL + u)
            return c
        lax.fori_loop(0, npages // PAGE_UNROLL, body, 0)

    def start_all(seq, sl):
        for_pages(lambda p: [cp.start() for cp in page_copies(seq, sl, p)])

    @pl.when(b == 0)
    def _():
        start_all(0, 0)

    @pl.when(b + 1 < nb)
    def _():
        start_all(b + 1, 1 - slot)

    for_pages(lambda p: [cp.wait() for cp in page_copies(b, slot, p)])
    return slot


def _paged_view(cache):
    n_pool, page = cache.shape[:2]
    return jnp.transpose(cache, (0, 2, 3, 4, 1)).reshape(n_pool, 2, KV_WIDTH, page)


CMP_PARTS = 4


def _compress_sample_kernel(pt_ref, pool_ref, pe_ref, wk_ref, wv_ref, w2k_ref, w2v_ref, kc_ref, vc_ref,
                            buf_ref, sem_ref, rk_ref, rv_ref, sh_ref, *, nc, npages, page_size):
    def page_copies(seq, slot, p):
        return [pltpu.make_async_copy(pool_ref.at[pt_ref[seq * npages + p]], buf_ref.at[slot, p], sem_ref.at[slot])]

    slot = _gather_step(page_copies, npages)
    nparts = rk_ref.shape[0]
    ppp = npages // nparts
    ncp = nc // nparts
    rows_of_parts = []
    for part in range(nparts):
        for u in range(ppp):
            rows = slice(u * page_size, (u + 1) * page_size)
            rk_ref[part, rows, :] = buf_ref[slot, part * ppp + u, 0].T
            rv_ref[part, rows, :] = buf_ref[slot, part * ppp + u, 1].T
        rows_of_parts.append(lambda s, part=part: (rk_ref[part, pl.ds(s, ncp, stride=CMP_STRIDE), :],
                                                   rv_ref[part, pl.ds(s, ncp, stride=CMP_STRIDE), :]))
    _compress_compute(rows_of_parts, pe_ref, wk_ref, wv_ref, w2k_ref, w2v_ref, sh_ref, kc_ref.at[0], vc_ref.at[0], nc)


def _compress_sample(pool_t, page_table, cw):
    nseq, npages = page_table.shape
    page_size = pool_t.shape[3]
    past = npages * page_size
    nc = past // CMP_STRIDE
    assert page_size == LANES and KV_WIDTH == LANES
    out = jax.ShapeDtypeStruct((nseq, nc, LANES), F32)
    cspecs = [pl.BlockSpec(w.shape, lambda b, pt, nd=w.ndim: (0,) * nd, pipeline_mode=pl.Buffered(1)) for w in cw]
    return pl.pallas_call(
        functools.partial(_compress_sample_kernel, nc=nc, npages=npages, page_size=page_size),
        grid_spec=pltpu.PrefetchScalarGridSpec(
            num_scalar_prefetch=1,
            grid=(nseq,),
            in_specs=[pl.BlockSpec(memory_space=pl.ANY)] + cspecs,
            out_specs=[pl.BlockSpec((1, nc, LANES), lambda b, pt: (b, 0, 0))] * 2,
            scratch_shapes=[pltpu.VMEM((2, npages, 2, KV_WIDTH, page_size), F32), pltpu.SemaphoreType.DMA((2,)),
                            pltpu.VMEM((CMP_PARTS, past // CMP_PARTS, KV_WIDTH), F32),
                            pltpu.VMEM((CMP_PARTS, past // CMP_PARTS, KV_WIDTH), F32),
                            pltpu.VMEM((nc + 8, NSA_KV_HEADS * CMP_HIDDEN), F32)],
        ),
        out_shape=[out, out],
        compiler_params=_cparams("arbitrary"),
    )(page_table.reshape(-1), pool_t, *cw)


def _bucket_thresholds():
    n = np.arange(0, 2 * REL_MAX_DIST)
    exact = REL_BUCKETS // 2
    out = []
    for ft in (np.float32, np.float64):
        nf = np.maximum(n, exact).astype(ft)
        large = exact + (np.log(nf / ft(exact)) / ft(math.log(REL_MAX_DIST / exact)) * ft(REL_BUCKETS - exact)).astype(np.int64)
        out.append(np.where(n < exact, n, np.minimum(large, REL_BUCKETS - 1)))
    assert (out[0] == out[1]).all() and out[1][REL_MAX_DIST] == REL_BUCKETS - 1
    return [int(np.argmax(out[1] >= b)) for b in range(1, REL_BUCKETS)]


_THR = _bucket_thresholds()


def _bias_tables(dist, rb_ref, heads):
    steps = [jnp.where(dist >= _THR[b - 1], 1.0, 0.0) for b in range(1, REL_BUCKETS)]
    far = dist >= REL_MAX_DIST
    out = []
    for h in heads:
        acc = jnp.full(dist.shape, rb_ref[0, h] - rb_ref[REL_BUCKETS - 1, h], F32)
        for b in range(1, REL_BUCKETS):
            acc = acc + steps[b - 1] * (rb_ref[b, h] - rb_ref[b - 1, h])
        out.append(jnp.where(dist < 0, NEG, jnp.where(far, 0.0, acc)))
    return out


BIG = 1e30
KT = 256
SLC_UNROLL = 8
N_BAND = 20
BAND_LO = 12
F_STEP = 2 * N_BAND
N_WK = 6
N_SK = 4
WIN_TILES = WINDOW // Q_BLOCK + 1


def _nsa_tables(rb_ref, st_ref, wt_ref, qft_ref, ov_ref, ncp):
    tq = Q_BLOCK
    heads = range(NSA_HEADS)
    t2 = lax.broadcasted_iota(jnp.int32, (tq, KT), 0)
    c2 = lax.broadcasted_iota(jnp.int32, (tq, KT), 1)
    for dd in range(N_SK - 1):
        tabs = _bias_tables(tq * dd + t2 - c2, rb_ref, heads)
        for h in heads:
            st_ref[h * N_SK + dd] = tabs[h]
    t1 = lax.broadcasted_iota(jnp.int32, (tq, LANES), 0)
    c1 = lax.broadcasted_iota(jnp.int32, (tq, LANES), 1)
    zeros = jnp.zeros((tq, LANES), F32)
    for d in range(2):
        tabs = _bias_tables(tq * d + t1 - c1, rb_ref, heads)
        for h in heads:
            wt_ref[h * N_WK + d] = tabs[h]
    for h in heads:
        st_ref[h * N_SK + N_SK - 1] = jnp.zeros((tq, KT), F32)
        wt_ref[h * N_WK + 2] = zeros
        wt_ref[h * N_WK + 3] = zeros
        wt_ref[h * N_WK + 4] = jnp.where(c1 > t1, 0.0, NEG)
        wt_ref[h * N_WK + 5] = zeros + NEG
    for g in range(NSA_KV_HEADS):
        f = c1 - (HEAD_DIM if g == 0 else 0)
        is_lo = f >= N_BAND
        u = jnp.where(is_lo, f - N_BAND, f) - BAND_LO
        live = (f >= 0) & (f < 2 * N_BAND)
        tabs = _bias_tables(t1 - CMP_STRIDE * u - (CMP_BLOCK - 1), rb_ref, range(g * NSA_GROUP, (g + 1) * NSA_GROUP))
        for r in range(NSA_GROUP):
            hi = tabs[r].astype(jnp.bfloat16).astype(F32)
            lo = jnp.where(tabs[r] <= NEG, 0.0, tabs[r] - hi)
            val = jnp.where(live, jnp.where(is_lo, lo, hi), 0.0)
            qft_ref[g * NSA_GROUP + r] = jnp.where(f == F_STEP, NEG, val)
    kk = lax.broadcasted_iota(jnp.int32, (ncp, LANES), 0)
    ss = lax.broadcasted_iota(jnp.int32, (ncp, LANES), 1)
    ratio = SLC_BLOCK // CMP_STRIDE
    nover = (CMP_BLOCK - 1) // CMP_STRIDE
    ov = (kk >= ratio * ss - nover) & (kk < ratio * ss + ratio) & (kk < ncp - 1)
    ov_ref[...] = jnp.where(ov, 1.0, 0.0).astype(ov_ref.dtype)


def _lane_half(rows):
    return lax.broadcasted_iota(jnp.int32, (rows, LANES), 1) // HEAD_DIM


def _place_heads(qg, g):
    keep = _lane_half(qg.shape[0]) == g
    out = []
    for r in range(NSA_GROUP):
        v = qg[:, (r // 2) * LANES:(r // 2 + 1) * LANES]
        v = jnp.where(g == r % 2, v, pltpu.roll(v, HEAD_DIM, 1))
        out.append(jnp.where(keep, v, 0.0))
    return out


def _unplace_heads(ohs, g):
    keep = _lane_half(ohs[0].shape[0]) == g
    cols = []
    for c in range(NSA_GROUP // 2):
        pair = []
        for r in (2 * c, 2 * c + 1):
            oh = jnp.where(keep, ohs[r], 0.0)
            pair.append(jnp.where(g == r % 2, oh, pltpu.roll(oh, HEAD_DIM, 1)))
        cols.append(pair[0] + pair[1])
    return jnp.concatenate(cols, axis=1)


def _gate_mix(gates, r, o_cmp, o_slc, o_win, l_slc=None, l_win=None):
    c_slc = gates[:, 3 * r + 1:3 * r + 2]
    c_win = gates[:, 3 * r + 2:3 * r + 3]
    if l_slc is not None:
        c_slc = c_slc / l_slc
    if l_win is not None:
        c_win = c_win / l_win
    return gates[:, 3 * r:3 * r + 1] * o_cmp + c_slc * o_slc + c_win * o_win


def _top_blocks(imp, qpos_lane):
    ns = imp.shape[1]
    vals = jnp.concatenate([imp[:, c * LANES:(c + 1) * LANES].T for c in range(ns // LANES)], axis=0)
    srow = lax.broadcasted_iota(jnp.int32, vals.shape, 0)
    qblk = qpos_lane[0:1, :] // SLC_BLOCK
    forced = (srow == 0) | (srow == qblk) | (srow == qblk - 1)
    vals = jnp.where(forced, -2.0, vals)
    vals = jnp.where(srow <= qblk, vals, -1.0)
    srow_f = srow.astype(F32)
    sel = jnp.where(forced, 1.0, 0.0)
    for _ in range(SLC_TOPN - 3):
        mx = jnp.max(vals, axis=0, keepdims=True)
        idx = jnp.min(jnp.where(vals == mx, srow_f, float(ns)), axis=0, keepdims=True)
        pick = srow_f == idx
        sel = jnp.where(pick, 1.0, sel)
        vals = jnp.where(pick, -2.0, vals)
    sel = jnp.where(srow <= qblk, sel, 0.0)
    return jnp.concatenate([sel[c * LANES:(c + 1) * LANES].T for c in range(ns // LANES)], axis=1)


def _nsa_prompt_kernel(rb_ref, q_ref, gate_ref, kc_ref, vc_ref, ksa_ref, vsa_ref, kw_ref, vwa_ref, o_ref,
                       st_ref, wt_ref, qft_ref, ov_ref, lbuf_ref, m_ref, acc_ref, qs_ref, *, ncp):
    tq = Q_BLOCK
    i = pl.program_id(1)
    gw = NSA_GROUP * HEAD_DIM

    @pl.when((pl.program_id(0) == 0) & (i == 0))
    def _():
        _nsa_tables(rb_ref, st_ref, wt_ref, qft_ref, ov_ref, ncp)

    qpos_lane = i * tq + lax.broadcasted_iota(jnp.int32, (tq, LANES), 1)
    trow = lax.broadcasted_iota(jnp.int32, (NSA_GROUP * tq, 1), 0) % tq
    row_live = (i * tq + trow) >= CMP_BLOCK - 1
    kidx = lax.broadcasted_iota(jnp.int32, (ncp, LANES), 0)
    klane = lax.broadcasted_iota(jnp.int32, (ncp, LANES), 1)
    ngrp = i // (2 * SLC_UNROLL) + 1
    nplain = jnp.maximum(ngrp - 2, 0)
    j0 = jnp.maximum(i - (WIN_TILES - 1), 0)
    ws = pl.multiple_of(j0 * tq, tq)
    kinds = []
    for jj in range(WIN_TILES):
        d = i - j0 - jj
        kinds.append(jnp.where(d < 0, N_WK - 1, d))
    u12 = kidx - (tq // CMP_STRIDE) * i + BAND_LO

    o_cmp, accw = [], []
    for g in range(NSA_KV_HEADS):
        qp = _place_heads(q_ref[:, g * gw:(g + 1) * gw], g)
        heads = [g * NSA_GROUP + r for r in range(NSA_GROUP)]
        lw = _mm_nt(jnp.concatenate(qp, axis=0).astype(MXU), kw_ref[pl.ds(ws, WIN_TILES * tq), :])
        lw = lw + jnp.concatenate(
            [jnp.concatenate([wt_ref[h * N_WK + kinds[jj]] for jj in range(WIN_TILES)], axis=1) for h in heads], axis=0)
        ew = jnp.exp(lw - jnp.max(lw, axis=1, keepdims=True)).astype(MXU)
        accw.append(jnp.dot(ew, vwa_ref[pl.ds(ws, WIN_TILES * tq), :], preferred_element_type=F32))
        f = klane - HEAD_DIM * (1 - g)
        fa = jnp.where(f >= N_BAND, f - N_BAND, f)
        feat = ((fa == u12) & (f >= 0) & (f < 2 * N_BAND)) | ((f == F_STEP) & (u12 >= N_BAND - 1))
        ka = jnp.where(klane // HEAD_DIM == g, kc_ref[0], jnp.where(feat, 1.0, 0.0)).astype(MXU)
        qa = jnp.concatenate([qp[r] + qft_ref[heads[r]] for r in range(NSA_GROUP)], axis=0).astype(MXU)
        lc = _mm_nt(qa, ka)
        ec = jnp.exp(lc - jnp.max(lc, axis=1, keepdims=True))
        sc = jnp.sum(ec, axis=1, keepdims=True)
        pc = (ec * jnp.where(row_live, 1.0 / sc, 0.0)).astype(MXU)
        o_cmp.append(jnp.dot(pc, vc_ref[0].astype(MXU), preferred_element_type=F32))
        imp = sum(jnp.dot(pc[r * tq:(r + 1) * tq], ov_ref[...], preferred_element_type=F32) for r in range(NSA_GROUP))
        selneg = (_top_blocks(imp, qpos_lane) - 1.0) * BIG
        qs_ref[g] = jnp.concatenate([jnp.concatenate([qp[r], selneg], axis=1) for r in range(NSA_GROUP)],
                                    axis=0).astype(qs_ref.dtype)

    gk = SLC_UNROLL * KT

    def group_body(g, carry):
        m_ref[...] = jnp.full(m_ref.shape, NEG, F32)

        def stage(grp, lt):
            lbuf_ref[grp] = lt
            mg = lt[:, 0:LANES]
            for c in range(1, gk // LANES):
                mg = jnp.maximum(mg, lt[:, c * LANES:(c + 1) * LANES])
            m_ref[...] = jnp.maximum(m_ref[...], mg)

        def plain_group(grp, c):
            stage(grp, _mm_nt(qs_ref[g], ksa_ref[pl.ds(pl.multiple_of(grp * gk, gk), gk), :]))
            return c

        def near_group(grp, c):
            tabs = []
            for r in range(NSA_GROUP):
                row = []
                for u in range(SLC_UNROLL):
                    dd = i - 2 * (grp * SLC_UNROLL + u)
                    kind = jnp.where((dd < 0) | (dd >= N_SK), N_SK - 1, dd)
                    row.append(st_ref[(g * NSA_GROUP + r) * N_SK + kind])
                tabs.append(jnp.concatenate(row, axis=1))
            stage(grp, _mm_nt(qs_ref[g], ksa_ref[pl.ds(pl.multiple_of(grp * gk, gk), gk), :])
                  + jnp.concatenate(tabs, axis=0))
            return c

        lax.fori_loop(0, nplain, plain_group, 0)
        lax.fori_loop(nplain, ngrp, near_group, 0)
        m_ref[...] = jnp.broadcast_to(jnp.max(m_ref[...], axis=1, keepdims=True), m_ref.shape)
        acc_ref[g] = jnp.zeros(acc_ref.shape[1:], F32)

        def pv_group(grp, c):
            e = jnp.exp(lbuf_ref[grp] - jnp.concatenate([m_ref[...]] * (gk // LANES), axis=1)).astype(MXU)
            acc_ref[g] += jnp.dot(e, vsa_ref[pl.ds(pl.multiple_of(grp * gk, gk), gk), :], preferred_element_type=F32)
            return c

        lax.fori_loop(0, ngrp, pv_group, 0)
        return carry

    lax.fori_loop(0, NSA_KV_HEADS, group_body, 0)
    for g in range(NSA_KV_HEADS):
        gates = gate_ref[:, g * LANES:(g + 1) * LANES]
        ohs = []
        for r in range(NSA_GROUP):
            rows = slice(r * tq, (r + 1) * tq)
            ohs.append(_gate_mix(gates, r, o_cmp[g][rows], acc_ref[g, rows, 0:LANES], accw[g][rows, 0:LANES],
                                 acc_ref[g, rows, LANES:LANES + 1], accw[g][rows, LANES:LANES + 1]))
        o_ref[:, g * gw:(g + 1) * gw] = _unplace_heads(ohs, g)


def _nsa_prompt(rel_bias, q, gates, kc, vc, ksa, vsa, kw, vwa, nseq, seq_len):
    tq = Q_BLOCK
    nq = seq_len // tq
    ncp = kc.shape[1]
    gk = SLC_UNROLL * KT
    ng = NSA_KV_HEADS
    rows = NSA_GROUP * tq
    assert seq_len % gk == 0 and seq_len >= WIN_TILES * tq and ncp % LANES == 0
    per_seq = lambda width: pl.BlockSpec((seq_len, width), lambda b, i: (b, 0), pipeline_mode=pl.Buffered(1))
    blk = lambda width: pl.BlockSpec((tq, width), lambda b, i: (b * nq + i, 0))
    cmp_spec = pl.BlockSpec((1, ncp, LANES), lambda b, i: (b, 0, 0))
    return pl.pallas_call(
        functools.partial(_nsa_prompt_kernel, ncp=ncp),
        grid=(nseq, nq),
        in_specs=[pl.BlockSpec(memory_space=pltpu.SMEM), blk(NSA_INNER), blk(ng * LANES), cmp_spec, cmp_spec,
                  per_seq(2 * LANES), per_seq(2 * LANES), per_seq(LANES), per_seq(2 * LANES)],
        out_specs=blk(NSA_INNER),
        out_shape=jax.ShapeDtypeStruct((nseq * seq_len, NSA_INNER), F32),
        scratch_shapes=[pltpu.VMEM((NSA_HEADS * N_SK, tq, KT), F32), pltpu.VMEM((NSA_HEADS * N_WK, tq, LANES), F32),
                        pltpu.VMEM((NSA_HEADS, tq, LANES), F32), pltpu.VMEM((ncp, LANES), MXU),
                        pltpu.VMEM((seq_len // gk, rows, gk), F32), pltpu.VMEM((rows, LANES), F32),
                        pltpu.VMEM((ng, rows, 2 * LANES), F32), pltpu.VMEM((ng, rows, 2 * LANES), MXU)],
        compiler_params=_cparams("arbitrary", "arbitrary"),
    )(rel_bias, q, gates, kc, vc, ksa, vsa, kw, vwa)


TAIL = 2 * LANES


def _nsa_sample_tables(rb_ref, cbt_ref, tail_ref, wtab_ref, ov_ref, eall_ref, *, ls, past, ncp, wk, wkp, nk):
    heads = range(NSA_HEADS)

    def per_group(tabs):
        return [jnp.concatenate(tabs[g * NSA_GROUP:(g + 1) * NSA_GROUP], axis=0) for g in range(NSA_KV_HEADS)]

    t = lax.broadcasted_iota(jnp.int32, (ls, ncp), 0)
    k = lax.broadcasted_iota(jnp.int32, (ls, ncp), 1)
    dist = past + t - CMP_STRIDE * k - (CMP_BLOCK - 1)
    dist = jnp.where(k < ncp - 1, dist, -1)
    for g, tab in enumerate(per_group(_bias_tables(dist, rb_ref, heads))):
        cbt_ref[g] = tab
    t = lax.broadcasted_iota(jnp.int32, (ls, TAIL), 0)
    c = lax.broadcasted_iota(jnp.int32, (ls, TAIL), 1) + (nk - TAIL)
    dist = jnp.where(c < past + ls, past + t - c, -1)
    for g, tab in enumerate(per_group(_bias_tables(dist, rb_ref, heads))):
        tail_ref[g] = tab
    t = lax.broadcasted_iota(jnp.int32, (ls, wkp), 0)
    c = lax.broadcasted_iota(jnp.int32, (ls, wkp), 1)
    dist = past + t - (past - wk + c)
    dist = jnp.where((c < wk + ls) & (dist < WINDOW), dist, -1)
    for g, tab in enumerate(per_group(_bias_tables(dist, rb_ref, heads))):
        wtab_ref[g] = tab
    kk = lax.broadcasted_iota(jnp.int32, (ncp, 2 * LANES), 0)
    ss = lax.broadcasted_iota(jnp.int32, (ncp, 2 * LANES), 1)
    ratio = SLC_BLOCK // CMP_STRIDE
    nover = (CMP_BLOCK - 1) // CMP_STRIDE
    ov = (kk >= ratio * ss - nover) & (kk < ratio * ss + ratio) & (kk < ncp - 1)
    ov_ref[...] = jnp.where(ov, 1.0, 0.0).astype(ov_ref.dtype)
    s2 = lax.broadcasted_iota(jnp.int32, (2 * LANES, nk), 0)
    c2 = lax.broadcasted_iota(jnp.int32, (2 * LANES, nk), 1)
    eall_ref[...] = jnp.where(c2 // SLC_BLOCK == s2, 1.0, 0.0).astype(eall_ref.dtype)


def _softmax_rows(l):
    m = jnp.max(l, axis=1, keepdims=True)
    e = jnp.exp(l - m)
    return e / jnp.sum(e, axis=1, keepdims=True)


SAMPLE_SPS = 2


def _t128(x):
    rows = x.shape[0]
    if rows < LANES:
        x = jnp.concatenate([x, jnp.zeros((LANES - rows, LANES), x.dtype)], axis=0)
    return x.T


def _nsa_sample_kernel(pt_ref, rb_ref, q_ref, gate_ref, kvs_ref, kvw_ref, kc_ref, vc_ref, win_ref, pool_ref,
                       o_ref, kt_ref, vt_ref, sem_ref, cbt_ref, tail_ref, wtab_ref, ov_ref, eall_ref,
                       *, ls, past, npages, page_size, ncp, wk, wkp, nk):
    b = pl.program_id(0)

    @pl.when(b == 0)
    def _():
        _nsa_sample_tables(rb_ref, cbt_ref, tail_ref, wtab_ref, ov_ref, eall_ref,
                           ls=ls, past=past, ncp=ncp, wk=wk, wkp=wkp, nk=nk)

    sps = kt_ref.shape[1]

    def page_copies(step, slot, p):
        cols = pl.ds(pl.multiple_of(p * page_size, page_size), page_size)
        cps = []
        for j in range(sps):
            pg = pt_ref[(step * sps + j) * npages + p]
            cps.append(pltpu.make_async_copy(pool_ref.at[pg, 0], kt_ref.at[slot, j, :, cols], sem_ref.at[0, slot]))
            cps.append(pltpu.make_async_copy(pool_ref.at[pg, 1], vt_ref.at[slot, j, :, cols], sem_ref.at[1, slot]))
        return cps

    slot = _gather_step(page_copies, npages)
    gw = NSA_GROUP * HEAD_DIM
    ng = NSA_KV_HEADS
    nrow = ng * NSA_GROUP * ls
    stack = lambda xs: xs[0] if len(xs) == 1 else jnp.concatenate(xs, axis=0)
    seq_rows = lambda x, j: x[j * nrow:(j + 1) * nrow]
    per_group = lambda ref: stack([ref[g] for g in range(ng)] * sps)
    qall, kwt, vwt = [], [], []
    for j in range(sps):
        tok = slice(j * ls, (j + 1) * ls)
        kvs = kvs_ref[tok, :]
        kt_ref[slot, j, :, past:nk] = _t128(kvs[:, 0:KV_WIDTH])
        vt_ref[slot, j, :, past:nk] = _t128(kvs[:, KV_WIDTH:])
        kvw = kvw_ref[tok, :]
        kwt.append(jnp.concatenate([win_ref[j, 0], _t128(kvw[:, 0:KV_WIDTH])], axis=1).astype(MXU))
        vwt.append(jnp.concatenate([win_ref[j, 1], _t128(kvw[:, KV_WIDTH:])], axis=1).astype(MXU))
        qall.append(stack([t for g in range(ng) for t in _place_heads(q_ref[tok, g * gw:(g + 1) * gw], g)]).astype(MXU))
    pc = _softmax_rows(stack([_mm_nt(qall[j], kc_ref[j]) for j in range(sps)]) + per_group(cbt_ref)).astype(MXU)
    o_cmp = [jnp.dot(seq_rows(pc, j), vc_ref[j].astype(MXU), preferred_element_type=F32) for j in range(sps)]
    impm = jnp.dot(pc, ov_ref[...], preferred_element_type=F32)
    nsel = sps * ng * ls
    imp = stack([sum(impm[((j * ng + g) * NSA_GROUP + r) * ls:((j * ng + g) * NSA_GROUP + r + 1) * ls]
                     for r in range(NSA_GROUP)) for j in range(sps) for g in range(ng)])
    imp = jnp.concatenate([imp, jnp.zeros((LANES - nsel, imp.shape[1]), F32)], axis=0)
    qpos_lane = past + lax.broadcasted_iota(jnp.int32, (1, LANES), 1) % ls
    selneg = ((_top_blocks(imp, qpos_lane)[0:nsel] - 1.0) * BIG).astype(MXU)
    selneg = stack([selneg[jg * ls:(jg + 1) * ls] for jg in range(sps * ng) for _ in range(NSA_GROUP)])
    mask = jnp.dot(selneg, eall_ref[...], preferred_element_type=F32)
    lsl = stack([jnp.dot(qall[j], kt_ref[slot, j].astype(MXU), preferred_element_type=F32) for j in range(sps)]) + mask
    lsl = jnp.concatenate([lsl[:, 0:nk - TAIL], lsl[:, nk - TAIL:] + per_group(tail_ref)], axis=1)
    ps = _softmax_rows(lsl)
    o_slc = [_mm_nt(seq_rows(ps, j), vt_ref[slot, j]) for j in range(sps)]
    pw = _softmax_rows(stack([jnp.dot(qall[j], kwt[j], preferred_element_type=F32) for j in range(sps)])
                       + per_group(wtab_ref))
    o_win = [_mm_nt(seq_rows(pw, j), vwt[j]) for j in range(sps)]
    for j in range(sps):
        tok = slice(j * ls, (j + 1) * ls)
        for g in range(ng):
            gates = gate_ref[tok, g * LANES:(g + 1) * LANES]
            ohs = []
            for r in range(NSA_GROUP):
                rows = slice((g * NSA_GROUP + r) * ls, (g * NSA_GROUP + r + 1) * ls)
                ohs.append(_gate_mix(gates, r, o_cmp[j][rows], o_slc[j][rows], o_win[j][rows]))
            o_ref[tok, g * gw:(g + 1) * gw] = _unplace_heads(ohs, g)


def _nsa_sample(rel_bias, q, gates, kvs, kvw, kc, vc, win_t, pool_t, page_table, ls):
    nseq, npages = page_table.shape
    page_size = pool_t.shape[3]
    past = npages * page_size
    ncp = kc.shape[1]
    wk = win_t.shape[3]
    wkp = wk + LANES
    nk = past + LANES
    assert past % SLC_BLOCK == 0 and ls % 8 == 0 and ls <= SLC_BLOCK and past >= wk and past // SLC_BLOCK + 1 <= 2 * LANES
    assert ncp % LANES == 0 and past >= CMP_BLOCK and page_size % LANES == 0 and wk % LANES == 0 and KV_WIDTH == LANES
    sps = SAMPLE_SPS if nseq % SAMPLE_SPS == 0 else 1
    row = lambda width: pl.BlockSpec((sps * ls, width), lambda b, pt: (b, 0))
    seq3 = lambda n, width: pl.BlockSpec((sps, n, width), lambda b, pt: (b, 0, 0))
    kw = 2 * KV_WIDTH
    nrow = NSA_GROUP * ls
    return pl.pallas_call(
        functools.partial(_nsa_sample_kernel, ls=ls, past=past, npages=npages, page_size=page_size, ncp=ncp,
                          wk=wk, wkp=wkp, nk=nk),
        grid_spec=pltpu.PrefetchScalarGridSpec(
            num_scalar_prefetch=1,
            grid=(nseq // sps,),
            in_specs=[pl.BlockSpec(memory_space=pltpu.SMEM), row(NSA_INNER), row(NSA_KV_HEADS * LANES), row(kw), row(kw),
                      seq3(ncp, LANES), seq3(ncp, LANES),
                      pl.BlockSpec((sps, 2, KV_WIDTH, wk), lambda b, pt: (b, 0, 0, 0)), pl.BlockSpec(memory_space=pl.ANY)],
            out_specs=row(NSA_INNER),
            scratch_shapes=[pltpu.VMEM((2, sps, KV_WIDTH, nk), F32), pltpu.VMEM((2, sps, KV_WIDTH, nk), F32),
                            pltpu.SemaphoreType.DMA((2, 2)),
                            pltpu.VMEM((NSA_KV_HEADS, nrow, ncp), F32), pltpu.VMEM((NSA_KV_HEADS, nrow, TAIL), F32),
                            pltpu.VMEM((NSA_KV_HEADS, nrow, wkp), F32),
                            pltpu.VMEM((ncp, 2 * LANES), MXU), pltpu.VMEM((2 * LANES, nk), MXU)],
        ),
        out_shape=jax.ShapeDtypeStruct((nseq * ls, NSA_INNER), F32),
        compiler_params=_cparams("arbitrary"),
    )(page_table.reshape(-1), rel_bias, q, gates, kvs, kvw, kc, vc, win_t, pool_t)


def _mix_out_kernel(x_ref, yssd_ref, onsa_ref, gn_ref, wo_ref, gq_ref, wq_ref, mkv_ref, wmo_ref, *rest,
                    rows_per_seq, ffn, final_norm):
    if ffn:
        g2_ref, win_ref, wout_ref = rest[0:3]
        gf_ref = rest[3] if final_norm else None
        out_ref, x2_ref, h_ref = rest[-3:]
        o_ref = x2_ref
    else:
        o_ref, = rest
    y = jnp.concatenate([yssd_ref[...], _rms(onsa_ref[...], gn_ref[...])], axis=1)
    x = x_ref[...] + jnp.dot(y.astype(MXU), wo_ref[...], preferred_element_type=F32)
    qm = jnp.dot(_rms(x, gq_ref[...]).astype(MXU), wq_ref[...], preferred_element_type=F32)
    transposed = len(mkv_ref.shape) == 4
    spb = mkv_ref.shape[0]
    m = mkv_ref.shape[3] if transposed else mkv_ref.shape[1]
    if transposed:
        side = lambda kv_i: jnp.concatenate([mkv_ref[s, kv_i] for s in range(spb)], axis=1).astype(MXU)
        km, vm = side(0), side(1)
        logits = lambda qh: jnp.dot(qh.astype(MXU), km, preferred_element_type=F32)
        attend = lambda p: _mm_nt(p, vm)
    else:
        kv = mkv_ref[...].reshape(spb * m, 2 * MEM_INNER)
        km = kv[:, 0:MEM_INNER].astype(MXU)
        vm = kv[:, MEM_INNER:].astype(MXU)
        logits = lambda qh: _mm_nt(qh, km)
        attend = lambda p: jnp.dot(p.astype(MXU), vm, preferred_element_type=F32)
    head = lax.broadcasted_iota(jnp.int32, qm.shape, 1) // HEAD_DIM
    if spb > 1:
        rseq = lax.broadcasted_iota(jnp.int32, (qm.shape[0], spb * m), 0) // rows_per_seq
        kseq = lax.broadcasted_iota(jnp.int32, (qm.shape[0], spb * m), 1) // m
        own = rseq == kseq
    o = jnp.zeros(qm.shape, F32)
    for h in range(MEM_HEADS):
        lg = logits(jnp.where(head == h, qm, 0.0)) * (HEAD_DIM ** -0.5)
        if spb > 1:
            lg = jnp.where(own, lg, NEG)
        o = o + jnp.where(head == h, attend(_softmax_rows(lg)), 0.0)
    o_ref[...] = x + jnp.dot(o.astype(MXU), wmo_ref[...], preferred_element_type=F32)
    if ffn:
        _ffn_apply(x2_ref, g2_ref, win_ref, wout_ref, gf_ref, out_ref, h_ref)


def _mix_out(x, yssd, onsa, g_nsa, w_out, g_memq, w_mem_q, mem_kv, w_mem_out, nseq, seq_len, ffn=None,
             tm=512, short_rows=64):
    t, d = x.shape
    mkv_block = (lambda spb: (spb,) + mem_kv.shape[1:])
    if seq_len < short_rows and nseq % (short_rows // seq_len) == 0:
        spb = short_rows // seq_len
        tm, nt = spb * seq_len, 1
    else:
        spb, tm = 1, min(tm, seq_len)
        nt = seq_len // tm
        assert nt * tm == seq_len
    row = lambda width: pl.BlockSpec((tm, width), lambda b, i: (b * nt + i, 0))
    args = [x, yssd, onsa, g_nsa.reshape(1, -1), w_out.astype(MXU), g_memq.reshape(1, d), w_mem_q.astype(MXU),
            mem_kv, w_mem_out.astype(MXU)]
    specs = [row(d), row(SSD_INNER), row(NSA_INNER), _const_spec((1, NSA_INNER)), _const_spec(w_out.shape),
             _const_spec((1, d)), _const_spec(w_mem_q.shape),
             pl.BlockSpec(mkv_block(spb), lambda b, i: (b,) + (0,) * (mem_kv.ndim - 1)),
             _const_spec(w_mem_out.shape)]
    scratch = []
    final_norm = False
    if ffn:
        g2, w_ffn_in, w_ffn_out, g_final = ffn
        assert w_ffn_out.shape[0] % FFN_CHUNK == 0
        final_norm = g_final is not None
        args += [g2.reshape(1, d), w_ffn_in.astype(MXU), w_ffn_out.astype(MXU)]
        specs += [_const_spec((1, d)), _const_spec(w_ffn_in.shape), _const_spec(w_ffn_out.shape)]
        if final_norm:
            args.append(g_final.reshape(1, d))
            specs.append(_const_spec((1, d)))
        scratch = [pltpu.VMEM((tm, d), F32), pltpu.VMEM((tm, d), MXU)]
    return pl.pallas_call(
        functools.partial(_mix_out_kernel, rows_per_seq=seq_len, ffn=bool(ffn), final_norm=final_norm),
        grid=(nseq // spb, nt),
        in_specs=specs,
        out_specs=row(d),
        out_shape=jax.ShapeDtypeStruct((t, d), F32),
        scratch_shapes=scratch,
        compiler_params=_cparams("parallel", "parallel"),
    )(*args)


def _norm_matmul_kernel(x_ref, g_ref, w_ref, o_ref):
    o_ref[...] = jnp.dot(_rms(x_ref[...], g_ref[...]).astype(MXU), w_ref[...], preferred_element_type=F32)


def _norm_matmul(x, g, w, tm=256):
    t, d = x.shape
    n = w.shape[1]
    tm = min(tm, t)
    assert t % tm == 0
    return pl.pallas_call(
        _norm_matmul_kernel,
        grid=(t // tm,),
        in_specs=[pl.BlockSpec((tm, d), lambda i: (i, 0)), _const_spec((1, d)), _const_spec(w.shape)],
        out_specs=pl.BlockSpec((tm, n), lambda i: (i, 0)),
        out_shape=jax.ShapeDtypeStruct((t, n), F32),
        compiler_params=_cparams("parallel"),
    )(x, g.reshape(1, d), w.astype(MXU))


def _trunk_layer(x, lw, cw, rel_bias, mem_kv, conv0, h0, nseq, seq_len, ffn2, sample=None):
    x1 = _ffn(x, lw['norm_ffn1'], lw['w_ffn1_in'], lw['w_ffn1_out'])
    outs = _proj(x1, lw['norm_mix'], lw['w_mix'], seq_len=None if sample else seq_len)
    z, xbc, q, kvc, kvs, kvw, gates, dt = outs[:8]
    y_ssd, h_new, conv_new = _ssd(z, xbc, dt, conv0, h0, lw['conv_w'], lw['conv_b'], lw['dt_bias'], lw['a_log'],
                                  lw['d_skip'], lw['norm_ssd_out'], nseq, seq_len)
    rows5 = lambda a: a.reshape(nseq, seq_len, 2, NSA_KV_HEADS, HEAD_DIM)
    if sample is None:
        kc, vc = _compress_prompt(kvc, cw, nseq, seq_len)
        o_nsa = _nsa_prompt(rel_bias, q, gates, kc, vc, *outs[8:12], nseq, seq_len)
        from_t = lambda a: jnp.transpose(a.reshape(nseq, 2, NSA_KV_HEADS, HEAD_DIM, seq_len), (0, 4, 1, 2, 3))
        kv_c, kv_s = from_t(outs[12]), from_t(outs[13])
    else:
        kc, vc = _compress_sample(sample['pool_c'], sample['page_table'], cw)
        o_nsa = _nsa_sample(rel_bias, q, gates, kvs, kvw, kc, vc, sample['win_t'], sample['pool_s'],
                            sample['page_table'], seq_len)
        kv_c, kv_s = rows5(kvc), rows5(kvs)
    if sample is None:
        x2 = _mix_out(x1, y_ssd, o_nsa, lw['norm_nsa_out'], lw['w_out'], lw['norm_mem_q'], lw['w_mem_q'], mem_kv,
                      lw['w_mem_out'], nseq, seq_len, ffn=ffn2)
    else:
        x2 = _mix_out(x1, y_ssd, o_nsa, lw['norm_nsa_out'], lw['w_out'], lw['norm_mem_q'], lw['w_mem_q'], mem_kv,
                      lw['w_mem_out'], nseq, seq_len)
        x2 = _ffn(x2, *ffn2)
    keep = min(WINDOW, seq_len)
    kv_w = kvw.reshape(nseq, seq_len, -1)[:, seq_len - keep:].reshape(nseq, keep, 2, NSA_KV_HEADS, HEAD_DIM)
    return x2, kv_c, kv_s, kv_w, h_new, conv_new


def kernel(x_prompt, x_sample, cache_kv_cmp, cache_kv_slc, state_win_kv, state_ssd, state_conv, cache_mem_kv, page_table, mem_prompt, norm_ffn1, w_ffn1_in, w_ffn1_out, norm_mix, w_in, conv_w, conv_b, dt_bias, a_log, d_skip, norm_ssd_out, cmp_pos_k, cmp_w1_k, cmp_w2_k, cmp_pos_v, cmp_w1_v, cmp_w2_v, norm_nsa_out, w_out, norm_mem_q, norm_mem_kv, w_mem_q, w_mem_kv, w_mem_out, norm_ffn2, w_ffn2_in, w_ffn2_out, rel_bias, norm_final):
    bp, lp, d = x_prompt.shape
    bs, lsq, _ = x_sample.shape
    depth = w_in.shape[0]
    xp = x_prompt.reshape(bp * lp, d)
    xs = x_sample.reshape(bs * lsq, d)
    outs_p = [[] for _ in range(6)]
    outs_s = [[] for _ in range(5)]
    for layer in range(depth):
        last = layer == depth - 1
        lw = {
            'norm_ffn1': norm_ffn1[layer], 'w_ffn1_in': w_ffn1_in[layer], 'w_ffn1_out': w_ffn1_out[layer],
            'norm_mix': norm_mix[layer], 'w_mix': _regroup_w_in(w_in[layer]), 'conv_w': conv_w[layer],
            'conv_b': conv_b[layer], 'dt_bias': dt_bias[layer], 'a_log': a_log[layer], 'd_skip': d_skip[layer],
            'norm_ssd_out': norm_ssd_out[layer], 'norm_nsa_out': norm_nsa_out[layer], 'w_out': w_out[layer],
            'norm_mem_q': norm_mem_q[layer], 'w_mem_q': w_mem_q[layer], 'w_mem_out': w_mem_out[layer],
        }
        cw = _compress_weights(cmp_pos_k[layer], cmp_w1_k[layer], cmp_w2_k[layer],
                               cmp_pos_v[layer], cmp_w1_v[layer], cmp_w2_v[layer])
        g_final = norm_final if last else None
        mem_kv_p = _norm_matmul(mem_prompt.reshape(-1, d), norm_mem_kv[layer], w_mem_kv[layer])
        mem_kv_p = mem_kv_p.reshape(bp, -1, 2 * MEM_INNER)
        conv0 = jnp.zeros((bp, SSD_CONV - 1, SSD_CONV_DIM), F32)
        h0 = jnp.zeros((bp, SSD_HEADS, HEAD_DIM, SSD_STATE), F32)
        ffn2 = (norm_ffn2[layer], w_ffn2_in[layer], w_ffn2_out[layer], g_final)
        xp, c_p, s_p, w_p, h_p, conv_p = _trunk_layer(xp, lw, cw, rel_bias, mem_kv_p, conv0, h0, bp, lp, ffn2)
        sample = dict(pool_c=_paged_view(cache_kv_cmp[layer]), pool_s=_paged_view(cache_kv_slc[layer]),
                      win_t=_paged_view(state_win_kv[layer]), page_table=page_table)
        mem_kv_s = jnp.transpose(cache_mem_kv[layer], (0, 2, 3, 4, 1)).reshape(bs, 2, MEM_INNER, -1)
        xs, c_s, s_s, w_s, h_s, conv_s = _trunk_layer(xs, lw, cw, rel_bias, mem_kv_s, state_conv[layer],
                                                      state_ssd[layer], bs, lsq, ffn2, sample=sample)
        w_s = jnp.concatenate([state_win_kv[layer], w_s], axis=1)[:, w_s.shape[1]:]
        for lst, v in zip(outs_p, (c_p, s_p, w_p, h_p, conv_p, mem_kv_p.reshape(bp, -1, 2, MEM_HEADS, HEAD_DIM))):
            lst.append(v)
        for lst, v in zip(outs_s, (c_s, s_s, w_s, h_s, conv_s)):
            lst.append(v)
    return (xp.reshape(bp, lp, d), xs.reshape(bs, lsq, d),
            *(jnp.stack(o) for o in outs_p), *(jnp.stack(o) for o in outs_s))
```

```python
import functools
import math

import numpy as np
import jax
import jax.numpy as jnp
from jax import lax
from jax.experimental import pallas as pl
from jax.experimental.pallas import tpu as pltpu

F32 = jnp.float32
MXU = jnp.bfloat16

HEAD_DIM = 64
SSD_HEADS = 8
SSD_GROUPS = 2
SSD_STATE = 128
SSD_CONV = 4
SSD_CHUNK = 128
NSA_HEADS = 8
NSA_KV_HEADS = 2
NSA_GROUP = NSA_HEADS // NSA_KV_HEADS
CMP_BLOCK = 32
CMP_STRIDE = 16
CMP_HIDDEN = 256
SLC_BLOCK = 64
SLC_TOPN = 16
WINDOW = 512
REL_BUCKETS = 32
REL_MAX_DIST = 128
MEM_HEADS = 4
Q_BLOCK = 128
EPS = 1e-6
NEG = -1e30

SSD_INNER = SSD_HEADS * HEAD_DIM
NSA_INNER = NSA_HEADS * HEAD_DIM
KV_WIDTH = NSA_KV_HEADS * HEAD_DIM
SSD_CONV_DIM = SSD_INNER + 2 * SSD_GROUPS * SSD_STATE
MEM_INNER = MEM_HEADS * HEAD_DIM
LANES = 128
VMEM_LIMIT = 56 * 1024 * 1024


def _cparams(*sem):
    return pltpu.CompilerParams(dimension_semantics=sem, vmem_limit_bytes=VMEM_LIMIT)


def _const_spec(shape):
    nd = len(shape)
    return pl.BlockSpec(shape, lambda *a: (0,) * nd, pipeline_mode=pl.Buffered(1))


def _mm(a, b):
    return jnp.dot(a.astype(MXU), b.astype(MXU), preferred_element_type=F32)


def _mm_nt(a, b):
    return lax.dot_general(a.astype(MXU), b.astype(MXU), (((1,), (1,)), ((), ())), preferred_element_type=F32)


def _rms(x, g):
    return x * lax.rsqrt(jnp.mean(x * x, axis=-1, keepdims=True) + EPS) * g


def _silu(x):
    return x * jax.nn.sigmoid(x)


FFN_CHUNK = 256


def _ffn_apply(x_ref, g_ref, win_ref, wout_ref, gf_ref, o_ref, h_ref):
    final_norm = gf_ref is not None
    nchunks = wout_ref.shape[0] // FFN_CHUNK
    x = x_ref[...]
    h_ref[...] = _rms(x, g_ref[...]).astype(h_ref.dtype)
    f = nchunks * FFN_CHUNK
    for j in range(nchunks):
        cols = slice(j * FFN_CHUNK, (j + 1) * FFN_CHUNK)
        h = h_ref[...]
        gate = jnp.dot(h, win_ref[:, cols], preferred_element_type=F32)
        up = jnp.dot(h, win_ref[:, f + j * FFN_CHUNK:f + (j + 1) * FFN_CHUNK], preferred_element_type=F32)
        a = (_silu(gate) * up).astype(h.dtype)
        t = jnp.dot(a, wout_ref[cols, :], preferred_element_type=F32)
        if j == 0:
            o_ref[...] = t
        else:
            o_ref[...] += t
    y = x_ref[...] + 0.5 * o_ref[...]
    if final_norm:
        y = _rms(y, gf_ref[...])
    o_ref[...] = y


def _ffn_kernel(x_ref, g_ref, win_ref, wout_ref, *rest, final_norm):
    gf_ref, o_ref, h_ref = rest if final_norm else (None,) + rest
    _ffn_apply(x_ref, g_ref, win_ref, wout_ref, gf_ref, o_ref, h_ref)


def _ffn(x, g, w_in, w_out, g_final=None, tm=512):
    t, d = x.shape
    f = w_out.shape[0]
    nchunks = f // FFN_CHUNK
    tm = min(tm, t)
    assert nchunks * FFN_CHUNK == f and t % tm == 0
    win = w_in.astype(MXU)
    wout = w_out.astype(MXU)
    final_norm = g_final is not None
    args = [x, g.reshape(1, d), win, wout]
    specs = [pl.BlockSpec((tm, d), lambda i: (i, 0)), _const_spec((1, d)),
             _const_spec(win.shape), _const_spec(wout.shape)]
    if final_norm:
        args.append(g_final.reshape(1, d))
        specs.append(_const_spec((1, d)))
    return pl.pallas_call(
        functools.partial(_ffn_kernel, final_norm=final_norm),
        grid=(t // tm,),
        in_specs=specs,
        out_specs=pl.BlockSpec((tm, d), lambda i: (i, 0)),
        out_shape=jax.ShapeDtypeStruct((t, d), F32),
        scratch_shapes=[pltpu.VMEM((tm, d), MXU)],
        compiler_params=_cparams("parallel"),
    )(*args)


PROJ_COLS = 3200
_O_Z, _O_XBC, _O_Q, _O_KVC, _O_KVS, _O_KVW, _O_G, _O_DT = 0, 512, 1536, 2048, 2304, 2560, 2816, 3072


def _regroup_w_in(w_in):
    o = np.cumsum([0, SSD_INNER, SSD_CONV_DIM, SSD_HEADS, NSA_INNER] + [KV_WIDTH] * 6 + [3 * NSA_HEADS])
    z, xbc, dt, q, kvs, gl = (w_in[:, o[0]:o[1]], w_in[:, o[1]:o[2]], w_in[:, o[2]:o[3]], w_in[:, o[3]:o[4]],
                              w_in[:, o[4]:o[10]], w_in[:, o[10]:o[11]])
    pad = lambda w: jnp.pad(w, ((0, 0), (0, LANES - w.shape[1])))
    ng = 3 * NSA_GROUP
    gls = [pad(gl[:, g * ng:(g + 1) * ng]) for g in range(NSA_KV_HEADS)]
    return jnp.concatenate([z, xbc, q, kvs] + gls + [pad(dt)], axis=1).astype(MXU)


def _proj_kernel(x_ref, g_ref, w_ref, z_ref, xbc_ref, q_ref, kvc_ref, kvs_ref, kvw_ref, gate_ref, dt_ref, *aug,
                 tiles_per_seq):
    h = _rms(x_ref[...], g_ref[...]).astype(w_ref.dtype)
    res = jnp.dot(h, w_ref[...], preferred_element_type=F32)
    tm = res.shape[0]
    z_ref[...] = res[:, _O_Z:_O_XBC]
    xbc_ref[...] = res[:, _O_XBC:_O_Q]
    q_ref[...] = res[:, _O_Q:_O_KVC] * (HEAD_DIM ** -0.5)
    kvc_ref[...] = res[:, _O_KVC:_O_KVS]
    kvs = res[:, _O_KVS:_O_KVW]
    kvw = res[:, _O_KVW:_O_G]
    kvs_ref[...] = kvs
    kvw_ref[...] = kvw
    gate_ref[...] = jax.nn.sigmoid(res[:, _O_G:_O_DT])
    dt_ref[...] = res[:, _O_DT:PROJ_COLS]
    if aug:
        ksa_ref, vsa_ref, kw_ref, vwa_ref, kvct_ref, kvst_ref = aug
        kvct_ref[0] = res[:, _O_KVC:_O_KVS].T
        kvst_ref[0] = kvs.T
        lane = lax.broadcasted_iota(jnp.int32, (tm, LANES), 1)
        row = lax.broadcasted_iota(jnp.int32, (tm, LANES), 0)
        pos = (pl.program_id(0) % tiles_per_seq) * tm + row
        onehot = jnp.where(lane == pos // SLC_BLOCK, 1.0, 0.0)
        ones_col = jnp.where(lane == 0, 1.0, 0.0)
        dt_ = ksa_ref.dtype
        ksa_ref[:, 0:LANES] = kvs[:, 0:LANES].astype(dt_)
        ksa_ref[:, LANES:] = onehot.astype(dt_)
        vsa_ref[:, 0:LANES] = kvs[:, LANES:].astype(dt_)
        vsa_ref[:, LANES:] = ones_col.astype(dt_)
        kw_ref[...] = kvw[:, 0:LANES].astype(dt_)
        vwa_ref[:, 0:LANES] = kvw[:, LANES:].astype(dt_)
        vwa_ref[:, LANES:] = ones_col.astype(dt_)


def _proj(x, g, w, seq_len=None, tm=512):
    t, d = x.shape
    tm = min(tm, t)
    assert t % tm == 0
    widths = [512, 1024, 512, 256, 256, 256, NSA_KV_HEADS * LANES, LANES]
    shapes = [jax.ShapeDtypeStruct((t, wd), F32) for wd in widths]
    specs = [pl.BlockSpec((tm, wd), lambda i: (i, 0)) for wd in widths]
    aug = seq_len is not None
    tps = seq_len // tm if aug else 1
    if aug:
        assert seq_len % tm == 0 and seq_len // SLC_BLOCK <= LANES
        for wd in (256, 256, LANES, 256):
            shapes.append(jax.ShapeDtypeStruct((t, wd), MXU))
            specs.append(pl.BlockSpec((tm, wd), lambda i: (i, 0)))
        for _ in range(2):
            shapes.append(jax.ShapeDtypeStruct((t // seq_len, 2 * KV_WIDTH, seq_len), F32))
            specs.append(pl.BlockSpec((1, 2 * KV_WIDTH, tm), lambda i: (i // tps, 0, i % tps)))
    return pl.pallas_call(
        functools.partial(_proj_kernel, tiles_per_seq=tps),
        grid=(t // tm,),
        in_specs=[pl.BlockSpec((tm, d), lambda i: (i, 0)), _const_spec((1, d)), _const_spec(w.shape)],
        out_specs=specs,
        out_shape=shapes,
        compiler_params=_cparams("parallel"),
    )(x, g.reshape(1, d), w)


HALO = 8


def _split3(x):
    hi = x.astype(jnp.bfloat16).astype(F32)
    r = x - hi
    mid = r.astype(jnp.bfloat16).astype(F32)
    return hi, mid, r - mid


def _ssd_kernel(z_ref, xbc_ref, dt_ref, conv0_ref, h0_ref, cw_ref, cb_ref, dtb_ref, alog_ref, dskip_ref, g_ref,
                y_ref, hout_ref, convout_ref, xwin_ref, h_ref, *, lv):
    q = SSD_CHUNK
    c = pl.program_id(1)

    @pl.when(c == 0)
    def _():
        xwin_ref[0:HALO, :] = conv0_ref[0]
        h_ref[...] = h0_ref[0]

    def padrows(v):
        if lv == q:
            return v
        return jnp.concatenate([v, jnp.zeros((q - lv, v.shape[1]), v.dtype)], axis=0)

    xwin_ref[HALO:HALO + q, :] = padrows(xbc_ref[...])
    acc = cb_ref[...] + cw_ref[SSD_CONV - 1:SSD_CONV, :] * xwin_ref[HALO:HALO + q, :]
    for k in range(SSD_CONV - 1):
        acc = acc + cw_ref[k:k + 1, :] * xwin_ref[pl.ds(HALO - (SSD_CONV - 1) + k, q), :]
    xc = _silu(acc)
    tail = xwin_ref[lv:lv + HALO, :]
    xwin_ref[0:HALO, :] = tail
    convout_ref[0] = tail

    xs = xc[:, 0:SSD_INNER]
    rowi = lax.broadcasted_iota(jnp.int32, (q, q), 0)
    coli = lax.broadcasted_iota(jnp.int32, (q, q), 1)
    causal = rowi >= coli
    tri = jnp.where(causal, 1.0, 0.0).astype(jnp.bfloat16)

    xdt = padrows(dt_ref[...]) + dtb_ref[...]
    dt = jnp.maximum(xdt, 0.0) + jnp.log1p(jnp.exp(-jnp.abs(xdt)))
    if lv != q:
        dt = jnp.where(rowi < lv, dt, 0.0)
    da = dt * (-jnp.exp(alog_ref[...]))
    acs = sum(jnp.dot(tri, p.astype(jnp.bfloat16), preferred_element_type=F32) for p in _split3(da))
    acs_t = acs.T
    dt_t = dt.T
    last = acs[q - 1:q, :]
    last_t = acs_t[:, q - 1:q]
    xs_t = [xs[:, p * LANES:(p + 1) * LANES].T for p in range(SSD_HEADS // 2)]
    lane_hi = lax.broadcasted_iota(jnp.int32, (q, LANES), 1) >= HEAD_DIM
    row_hi = lax.broadcasted_iota(jnp.int32, (LANES, q), 0) >= HEAD_DIM
    hpg = SSD_HEADS // SSD_GROUPS
    ys = []
    for g in range(SSD_GROUPS):
        bm = xc[:, SSD_INNER + g * SSD_STATE:SSD_INNER + (g + 1) * SSD_STATE]
        cm = xc[:, SSD_INNER + (SSD_GROUPS + g) * SSD_STATE:SSD_INNER + (SSD_GROUPS + g + 1) * SSD_STATE]
        cb = _mm_nt(cm, bm)
        for pp in range(hpg // 2):
            p = g * (hpg // 2) + pp
            xpair = xs[:, p * LANES:(p + 1) * LANES]
            yi = []
            for hh in range(2):
                h = 2 * p + hh
                seg = acs[:, h:h + 1] - acs_t[h:h + 1, :]
                decay = jnp.where(causal, jnp.exp(jnp.where(causal, seg, 0.0)), 0.0)
                yi.append(_mm(cb * decay * dt_t[h:h + 1, :], xpair))
            y_intra = jnp.where(lane_hi, yi[1], yi[0])
            h0, h1 = 2 * p, 2 * p + 1
            grow = jnp.where(lane_hi, jnp.exp(acs[:, h1:h1 + 1]), jnp.exp(acs[:, h0:h0 + 1]))
            y_inter = _mm_nt(cm, h_ref[p]) * grow
            ys.append(y_intra + y_inter)
            te = jnp.where(row_hi,
                           jnp.exp(last_t[h1:h1 + 1, :] - acs_t[h1:h1 + 1, :]) * dt_t[h1:h1 + 1, :],
                           jnp.exp(last_t[h0:h0 + 1, :] - acs_t[h0:h0 + 1, :]) * dt_t[h0:h0 + 1, :])
            states = _mm(xs_t[p] * te, bm)
            rdec = jnp.where(row_hi[:, 0:1], jnp.exp(last_t[h1:h1 + 1, :]), jnp.exp(last_t[h0:h0 + 1, :]))
            h_ref[p] = h_ref[p] * rdec + states
    y = jnp.concatenate(ys, axis=1) + dskip_ref[...] * xs
    y = _rms(y * _silu(padrows(z_ref[...])), g_ref[...])
    y_ref[...] = y[0:lv, :]
    hout_ref[0] = h_ref[...]


def _ssd(z, xbc, dt, conv0, h0, conv_w, conv_b, dt_bias, a_log, d_skip, norm_g, nseq, seq_len):
    q = SSD_CHUNK
    lv = min(q, seq_len)
    nch = seq_len // lv
    assert nch * lv == seq_len
    t = nseq * seq_len
    cdim = SSD_CONV_DIM
    keep = SSD_CONV - 1
    conv0p = jnp.pad(conv0, ((0, 0), (HALO - keep, 0), (0, 0)))
    padl = lambda v: jnp.pad(v.reshape(1, -1), ((0, 0), (0, LANES - v.size)))
    hp = h0.reshape(nseq, SSD_HEADS // 2, 2 * HEAD_DIM, SSD_STATE)
    row = lambda b, c: (b * nch + c, 0)
    y, hout, convout = pl.pallas_call(
        functools.partial(_ssd_kernel, lv=lv),
        grid=(nseq, nch),
        in_specs=[pl.BlockSpec((lv, SSD_INNER), row), pl.BlockSpec((lv, cdim), row), pl.BlockSpec((lv, LANES), row),
                  pl.BlockSpec((1, HALO, cdim), lambda b, c: (b, 0, 0)),
                  pl.BlockSpec((1,) + hp.shape[1:], lambda b, c: (b, 0, 0, 0)),
                  _const_spec((SSD_CONV, cdim)), _const_spec((1, cdim)), _const_spec((1, LANES)),
                  _const_spec((1, LANES)), _const_spec((1, SSD_INNER)), _const_spec((1, SSD_INNER))],
        out_specs=[pl.BlockSpec((lv, SSD_INNER), row),
                   pl.BlockSpec((1,) + hp.shape[1:], lambda b, c: (b, 0, 0, 0)),
                   pl.BlockSpec((1, HALO, cdim), lambda b, c: (b, 0, 0))],
        out_shape=[jax.ShapeDtypeStruct((t, SSD_INNER), F32), jax.ShapeDtypeStruct(hp.shape, F32),
                   jax.ShapeDtypeStruct((nseq, HALO, cdim), F32)],
        scratch_shapes=[pltpu.VMEM((HALO + q, cdim), F32), pltpu.VMEM(hp.shape[1:], F32)],
        compiler_params=_cparams("parallel", "arbitrary"),
    )(z, xbc, dt, conv0p, hp, conv_w, conv_b.reshape(1, cdim), padl(dt_bias), padl(a_log),
      jnp.repeat(d_skip, HEAD_DIM).reshape(1, SSD_INNER), norm_g.reshape(1, SSD_INNER))
    return y, hout.reshape(nseq, SSD_HEADS, HEAD_DIM, SSD_STATE), convout[:, HALO - keep:, :]


def _compress_weights(pe_k, w1_k, w2_k, pe_v, w1_v, w2_v):
    def w1_pair(w1):
        w = w1.reshape(CMP_BLOCK, HEAD_DIM, CMP_HIDDEN)
        zz = jnp.zeros_like(w)
        w = jnp.concatenate([jnp.concatenate([w, zz], axis=2), jnp.concatenate([zz, w], axis=2)], axis=1)
        return w.reshape(CMP_BLOCK // 2, 2 * KV_WIDTH, NSA_KV_HEADS * CMP_HIDDEN).astype(MXU)

    def w2_pair(w2):
        zz = jnp.zeros_like(w2)
        return jnp.concatenate([jnp.concatenate([w2, zz], axis=1), jnp.concatenate([zz, w2], axis=1)], axis=0).astype(MXU)

    pe = jnp.concatenate([pe_k, pe_k, pe_v, pe_v], axis=1)
    return pe, w1_pair(w1_k), w1_pair(w1_v), w2_pair(w2_k), w2_pair(w2_v)


def _compress_first_layer(row_of_chunks, pe_ref, wk_ref, wv_ref):
    half = CMP_STRIDE
    a0k = a1k = a0v = a1v = None
    for s2 in range(half // 2):
        rows = [row_of_chunks(2 * s2 + u) for u in range(2)]

        def lhs(j, kv):
            lanes = slice(kv * LANES, (kv + 1) * LANES)
            return jnp.concatenate([(rows[u][kv] + pe_ref[j * half + 2 * s2 + u:j * half + 2 * s2 + u + 1, lanes])
                                    for u in range(2)], axis=1).astype(MXU)

        t0k = jnp.dot(lhs(0, 0), wk_ref[s2], preferred_element_type=F32)
        t0v = jnp.dot(lhs(0, 1), wv_ref[s2], preferred_element_type=F32)
        t1k = jnp.dot(lhs(1, 0), wk_ref[half // 2 + s2], preferred_element_type=F32)
        t1v = jnp.dot(lhs(1, 1), wv_ref[half // 2 + s2], preferred_element_type=F32)
        a0k, a1k = (t0k, t1k) if s2 == 0 else (a0k + t0k, a1k + t1k)
        a0v, a1v = (t0v, t1v) if s2 == 0 else (a0v + t0v, a1v + t1v)
    return a0k, a1k, a0v, a1v


def _compress_compute(rows_of_parts, pe_ref, wk_ref, wv_ref, w2k_ref, w2v_ref, sh_ref, kc_ref, vc_ref, nc):
    parts = [_compress_first_layer(f, pe_ref, wk_ref, wv_ref) for f in rows_of_parts]
    a0k, a1k, a0v, a1v = (p[0] if len(parts) == 1 else jnp.concatenate(p, axis=0) for p in zip(*parts))
    sh_ref[nc:nc + 8, :] = jnp.zeros((8, sh_ref.shape[1]), F32)
    for a0, a1, w2_ref, o_ref in ((a0k, a1k, w2k_ref, kc_ref), (a0v, a1v, w2v_ref, vc_ref)):
        sh_ref[0:nc, :] = a1
        hid = a0 + sh_ref[pl.ds(1, nc), :]
        o_ref[...] = jnp.dot(_silu(hid).astype(MXU), w2_ref[...], preferred_element_type=F32)


def _compress_prompt_kernel(rows_ref, pe_ref, wk_ref, wv_ref, w2k_ref, w2v_ref, kc_ref, vc_ref, sh_ref, *, nc):
    rw = 2 * KV_WIDTH

    def row_of_chunks(s):
        return rows_ref[:, s * rw:s * rw + LANES], rows_ref[:, s * rw + LANES:(s + 1) * rw]

    _compress_compute([row_of_chunks], pe_ref, wk_ref, wv_ref, w2k_ref, w2v_ref, sh_ref, kc_ref.at[0], vc_ref.at[0], nc)


def _compress_specs(cw):
    return [_const_spec(w.shape) for w in cw]


def _compress_prompt(kv_cmp, cw, nseq, seq_len):
    nc = seq_len // CMP_STRIDE
    out = jax.ShapeDtypeStruct((nseq, nc, LANES), F32)
    return pl.pallas_call(
        functools.partial(_compress_prompt_kernel, nc=nc),
        grid=(nseq,),
        in_specs=[pl.BlockSpec((nc, CMP_STRIDE * 2 * KV_WIDTH), lambda b: (b, 0))] + _compress_specs(cw),
        out_specs=[pl.BlockSpec((1, nc, LANES), lambda b: (b, 0, 0))] * 2,
        out_shape=[out, out],
        scratch_shapes=[pltpu.VMEM((nc + 8, NSA_KV_HEADS * CMP_HIDDEN), F32)],
        compiler_params=_cparams("parallel"),
    )(kv_cmp.reshape(nseq * nc, CMP_STRIDE * 2 * KV_WIDTH), *cw)


PAGE_UNROLL = 8


def _gather_step(page_copies, npages):
    b = pl.program_id(0)
    nb = pl.num_programs(0)
    slot = b % 2
    assert npages % PAGE_UNROLL == 0

    def for_pages(fn):
        def body(q, c):
            for u in range(PAGE_UNROLL):
                fn(q * PAGE_UNROLL + u)
            return c
        lax.fori_loop(0, npages // PAGE_UNROLL, body, 0)

    def start_all(seq, sl):
        for_pages(lambda p: [cp.start() for cp in page_copies(seq, sl, p)])

    @pl.when(b == 0)
    def _():
        start_all(0, 0)

    @pl.when(b + 1 < nb)
    def _():
        start_all(b + 1, 1 - slot)

    for_pages(lambda p: [cp.wait() for cp in page_copies(b, slot, p)])
    return slot


def _paged_view(cache):
    n_pool, page = cache.shape[:2]
    return jnp.transpose(cache, (0, 2, 3, 4, 1)).reshape(n_pool, 2, KV_WIDTH, page)


CMP_PARTS = 4


def _compress_sample_kernel(pt_ref, pool_ref, pe_ref, wk_ref, wv_ref, w2k_ref, w2v_ref, kc_ref, vc_ref,
                            buf_ref, sem_ref, rk_ref, rv_ref, sh_ref, *, nc, npages, page_size):
    def page_copies(seq, slot, p):
        return [pltpu.make_async_copy(pool_ref.at[pt_ref[seq * npages + p]], buf_ref.at[slot, p], sem_ref.at[slot])]

    slot = _gather_step(page_copies, npages)
    nparts = rk_ref.shape[0]
    ppp = npages // nparts
    ncp = nc // nparts
    rows_of_parts = []
    for part in range(nparts):
        for u in range(ppp):
            rows = slice(u * page_size, (u + 1) * page_size)
            rk_ref[part, rows, :] = buf_ref[slot, part * ppp + u, 0].T
            rv_ref[part, rows, :] = buf_ref[slot, part * ppp + u, 1].T
        rows_of_parts.append(lambda s, part=part: (rk_ref[part, pl.ds(s, ncp, stride=CMP_STRIDE), :],
                                                   rv_ref[part, pl.ds(s, ncp, stride=CMP_STRIDE), :]))
    _compress_compute(rows_of_parts, pe_ref, wk_ref, wv_ref, w2k_ref, w2v_ref, sh_ref, kc_ref.at[0], vc_ref.at[0], nc)


def _compress_sample(pool_t, page_table, cw):
    nseq, npages = page_table.shape
    page_size = pool_t.shape[3]
    past = npages * page_size
    nc = past // CMP_STRIDE
    assert page_size == LANES and KV_WIDTH == LANES
    out = jax.ShapeDtypeStruct((nseq, nc, LANES), F32)
    cspecs = [pl.BlockSpec(w.shape, lambda b, pt, nd=w.ndim: (0,) * nd, pipeline_mode=pl.Buffered(1)) for w in cw]
    return pl.pallas_call(
        functools.partial(_compress_sample_kernel, nc=nc, npages=npages, page_size=page_size),
        grid_spec=pltpu.PrefetchScalarGridSpec(
            num_scalar_prefetch=1,
            grid=(nseq,),
            in_specs=[pl.BlockSpec(memory_space=pl.ANY)] + cspecs,
            out_specs=[pl.BlockSpec((1, nc, LANES), lambda b, pt: (b, 0, 0))] * 2,
            scratch_shapes=[pltpu.VMEM((2, npages, 2, KV_WIDTH, page_size), F32), pltpu.SemaphoreType.DMA((2,)),
                            pltpu.VMEM((CMP_PARTS, past // CMP_PARTS, KV_WIDTH), F32),
                            pltpu.VMEM((CMP_PARTS, past // CMP_PARTS, KV_WIDTH), F32),
                            pltpu.VMEM((nc + 8, NSA_KV_HEADS * CMP_HIDDEN), F32)],
        ),
        out_shape=[out, out],
        compiler_params=_cparams("arbitrary"),
    )(page_table.reshape(-1), pool_t, *cw)


def _bucket_thresholds():
    n = np.arange(0, 2 * REL_MAX_DIST)
    exact = REL_BUCKETS // 2
    out = []
    for ft in (np.float32, np.float64):
        nf = np.maximum(n, exact).astype(ft)
        large = exact + (np.log(nf / ft(exact)) / ft(math.log(REL_MAX_DIST / exact)) * ft(REL_BUCKETS - exact)).astype(np.int64)
        out.append(np.where(n < exact, n, np.minimum(large, REL_BUCKETS - 1)))
    assert (out[0] == out[1]).all() and out[1][REL_MAX_DIST] == REL_BUCKETS - 1
    return [int(np.argmax(out[1] >= b)) for b in range(1, REL_BUCKETS)]


_THR = _bucket_thresholds()


def _bias_tables(dist, rb_ref, heads):
    steps = [jnp.where(dist >= _THR[b - 1], 1.0, 0.0) for b in range(1, REL_BUCKETS)]
    far = dist >= REL_MAX_DIST
    out = []
    for h in heads:
        acc = jnp.full(dist.shape, rb_ref[0, h] - rb_ref[REL_BUCKETS - 1, h], F32)
        for b in range(1, REL_BUCKETS):
            acc = acc + steps[b - 1] * (rb_ref[b, h] - rb_ref[b - 1, h])
        out.append(jnp.where(dist < 0, NEG, jnp.where(far, 0.0, acc)))
    return out


BIG = 1e30
KT = 256
SLC_UNROLL = 8
N_BAND = 20
BAND_LO = 12
F_STEP = 2 * N_BAND
N_WK = 6
N_SK = 4
WIN_TILES = WINDOW // Q_BLOCK + 1


def _nsa_tables(rb_ref, st_ref, wt_ref, qft_ref, ov_ref, ncp):
    tq = Q_BLOCK
    heads = range(NSA_HEADS)
    t2 = lax.broadcasted_iota(jnp.int32, (tq, KT), 0)
    c2 = lax.broadcasted_iota(jnp.int32, (tq, KT), 1)
    for dd in range(N_SK - 1):
        tabs = _bias_tables(tq * dd + t2 - c2, rb_ref, heads)
        for h in heads:
            st_ref[h * N_SK + dd] = tabs[h]
    t1 = lax.broadcasted_iota(jnp.int32, (tq, LANES), 0)
    c1 = lax.broadcasted_iota(jnp.int32, (tq, LANES), 1)
    zeros = jnp.zeros((tq, LANES), F32)
    for d in range(2):
        tabs = _bias_tables(tq * d + t1 - c1, rb_ref, heads)
        for h in heads:
            wt_ref[h * N_WK + d] = tabs[h]
    for h in heads:
        st_ref[h * N_SK + N_SK - 1] = jnp.zeros((tq, KT), F32)
        wt_ref[h * N_WK + 2] = zeros
        wt_ref[h * N_WK + 3] = zeros
        wt_ref[h * N_WK + 4] = jnp.where(c1 > t1, 0.0, NEG)
        wt_ref[h * N_WK + 5] = zeros + NEG
    for g in range(NSA_KV_HEADS):
        f = c1 - (HEAD_DIM if g == 0 else 0)
        is_lo = f >= N_BAND
        u = jnp.where(is_lo, f - N_BAND, f) - BAND_LO
        live = (f >= 0) & (f < 2 * N_BAND)
        tabs = _bias_tables(t1 - CMP_STRIDE * u - (CMP_BLOCK - 1), rb_ref, range(g * NSA_GROUP, (g + 1) * NSA_GROUP))
        for r in range(NSA_GROUP):
            hi = tabs[r].astype(jnp.bfloat16).astype(F32)
            lo = jnp.where(tabs[r] <= NEG, 0.0, tabs[r] - hi)
            val = jnp.where(live, jnp.where(is_lo, lo, hi), 0.0)
            qft_ref[g * NSA_GROUP + r] = jnp.where(f == F_STEP, NEG, val)
    kk = lax.broadcasted_iota(jnp.int32, (ncp, LANES), 0)
    ss = lax.broadcasted_iota(jnp.int32, (ncp, LANES), 1)
    ratio = SLC_BLOCK // CMP_STRIDE
    nover = (CMP_BLOCK - 1) // CMP_STRIDE
    ov = (kk >= ratio * ss - nover) & (kk < ratio * ss + ratio) & (kk < ncp - 1)
    ov_ref[...] = jnp.where(ov, 1.0, 0.0).astype(ov_ref.dtype)


def _lane_half(rows):
    return lax.broadcasted_iota(jnp.int32, (rows, LANES), 1) // HEAD_DIM


def _place_heads(qg, g):
    keep = _lane_half(qg.shape[0]) == g
    out = []
    for r in range(NSA_GROUP):
        v = qg[:, (r // 2) * LANES:(r // 2 + 1) * LANES]
        v = jnp.where(g == r % 2, v, pltpu.roll(v, HEAD_DIM, 1))
        out.append(jnp.where(keep, v, 0.0))
    return out


def _unplace_heads(ohs, g):
    keep = _lane_half(ohs[0].shape[0]) == g
    cols = []
    for c in range(NSA_GROUP // 2):
        pair = []
        for r in (2 * c, 2 * c + 1):
            oh = jnp.where(keep, ohs[r], 0.0)
            pair.append(jnp.where(g == r % 2, oh, pltpu.roll(oh, HEAD_DIM, 1)))
        cols.append(pair[0] + pair[1])
    return jnp.concatenate(cols, axis=1)


def _gate_mix(gates, r, o_cmp, o_slc, o_win, l_slc=None, l_win=None):
    c_slc = gates[:, 3 * r + 1:3 * r + 2]
    c_win = gates[:, 3 * r + 2:3 * r + 3]
    if l_slc is not None:
        c_slc = c_slc / l_slc
    if l_win is not None:
        c_win = c_win / l_win
    return gates[:, 3 * r:3 * r + 1] * o_cmp + c_slc * o_slc + c_win * o_win


def _top_blocks(imp, qpos_lane):
    ns = imp.shape[1]
    vals = jnp.concatenate([imp[:, c * LANES:(c + 1) * LANES].T for c in range(ns // LANES)], axis=0)
    srow = lax.broadcasted_iota(jnp.int32, vals.shape, 0)
    qblk = qpos_lane[0:1, :] // SLC_BLOCK
    forced = (srow == 0) | (srow == qblk) | (srow == qblk - 1)
    vals = jnp.where(forced, -2.0, vals)
    vals = jnp.where(srow <= qblk, vals, -1.0)
    srow_f = srow.astype(F32)
    sel = jnp.where(forced, 1.0, 0.0)
    for _ in range(SLC_TOPN - 3):
        mx = jnp.max(vals, axis=0, keepdims=True)
        idx = jnp.min(jnp.where(vals == mx, srow_f, float(ns)), axis=0, keepdims=True)
        pick = srow_f == idx
        sel = jnp.where(pick, 1.0, sel)
        vals = jnp.where(pick, -2.0, vals)
    sel = jnp.where(srow <= qblk, sel, 0.0)
    return jnp.concatenate([sel[c * LANES:(c + 1) * LANES].T for c in range(ns // LANES)], axis=1)


def _nsa_prompt_kernel(rb_ref, q_ref, gate_ref, kc_ref, vc_ref, ksa_ref, vsa_ref, kw_ref, vwa_ref, o_ref,
                       st_ref, wt_ref, qft_ref, ov_ref, lbuf_ref, m_ref, acc_ref, qs_ref, *, ncp):
    tq = Q_BLOCK
    i = pl.program_id(1)
    gw = NSA_GROUP * HEAD_DIM

    @pl.when((pl.program_id(0) == 0) & (i == 0))
    def _():
        _nsa_tables(rb_ref, st_ref, wt_ref, qft_ref, ov_ref, ncp)

    qpos_lane = i * tq + lax.broadcasted_iota(jnp.int32, (tq, LANES), 1)
    trow = lax.broadcasted_iota(jnp.int32, (NSA_GROUP * tq, 1), 0) % tq
    row_live = (i * tq + trow) >= CMP_BLOCK - 1
    kidx = lax.broadcasted_iota(jnp.int32, (ncp, LANES), 0)
    klane = lax.broadcasted_iota(jnp.int32, (ncp, LANES), 1)
    ngrp = i // (2 * SLC_UNROLL) + 1
    nplain = jnp.maximum(ngrp - 2, 0)
    j0 = jnp.maximum(i - (WIN_TILES - 1), 0)
    ws = pl.multiple_of(j0 * tq, tq)
    kinds = []
    for jj in range(WIN_TILES):
        d = i - j0 - jj
        kinds.append(jnp.where(d < 0, N_WK - 1, d))
    u12 = kidx - (tq // CMP_STRIDE) * i + BAND_LO

    o_cmp, accw = [], []
    for g in range(NSA_KV_HEADS):
        qp = _place_heads(q_ref[:, g * gw:(g + 1) * gw], g)
        heads = [g * NSA_GROUP + r for r in range(NSA_GROUP)]
        lw = _mm_nt(jnp.concatenate(qp, axis=0).astype(MXU), kw_ref[pl.ds(ws, WIN_TILES * tq), :])
        lw = lw + jnp.concatenate(
            [jnp.concatenate([wt_ref[h * N_WK + kinds[jj]] for jj in range(WIN_TILES)], axis=1) for h in heads], axis=0)
        ew = jnp.exp(lw - jnp.max(lw, axis=1, keepdims=True)).astype(MXU)
        accw.append(jnp.dot(ew, vwa_ref[pl.ds(ws, WIN_TILES * tq), :], preferred_element_type=F32))
        f = klane - HEAD_DIM * (1 - g)
        fa = jnp.where(f >= N_BAND, f - N_BAND, f)
        feat = ((fa == u12) & (f >= 0) & (f < 2 * N_BAND)) | ((f == F_STEP) & (u12 >= N_BAND - 1))
        ka = jnp.where(klane // HEAD_DIM == g, kc_ref[0], jnp.where(feat, 1.0, 0.0)).astype(MXU)
        qa = jnp.concatenate([qp[r] + qft_ref[heads[r]] for r in range(NSA_GROUP)], axis=0).astype(MXU)
        lc = _mm_nt(qa, ka)
        ec = jnp.exp(lc - jnp.max(lc, axis=1, keepdims=True))
        sc = jnp.sum(ec, axis=1, keepdims=True)
        pc = (ec * jnp.where(row_live, 1.0 / sc, 0.0)).astype(MXU)
        o_cmp.append(jnp.dot(pc, vc_ref[0].astype(MXU), preferred_element_type=F32))
        imp = sum(jnp.dot(pc[r * tq:(r + 1) * tq], ov_ref[...], preferred_element_type=F32) for r in range(NSA_GROUP))
        selneg = (_top_blocks(imp, qpos_lane) - 1.0) * BIG
        qs_ref[g] = jnp.concatenate([jnp.concatenate([qp[r], selneg], axis=1) for r in range(NSA_GROUP)],
                                    axis=0).astype(qs_ref.dtype)

    gk = SLC_UNROLL * KT

    def group_body(g, carry):
        m_ref[...] = jnp.full(m_ref.shape, NEG, F32)

        def stage(grp, lt):
            width = lt.shape[1]
            lbuf_ref[grp, :, 0:width] = lt
            mg = lt[:, 0:LANES]
            for c in range(1, width // LANES):
                mg = jnp.maximum(mg, lt[:, c * LANES:(c + 1) * LANES])
            m_ref[...] = jnp.maximum(m_ref[...], mg)

        def keys(ref, grp, width):
            return ref[pl.ds(pl.multiple_of(grp * gk, gk), width), :]

        def plain_group(grp, c):
            stage(grp, _mm_nt(qs_ref[g], keys(ksa_ref, grp, gk)))
            return c

        def near_logits(grp, width):
            tabs = []
            for r in range(NSA_GROUP):
                row = []
                for u in range(width // KT):
                    dd = i - 2 * (grp * SLC_UNROLL + u)
                    kind = jnp.where((dd < 0) | (dd >= N_SK), N_SK - 1, dd)
                    row.append(st_ref[(g * NSA_GROUP + r) * N_SK + kind])
                tabs.append(jnp.concatenate(row, axis=1))
            stage(grp, _mm_nt(qs_ref[g], keys(ksa_ref, grp, width)) + jnp.concatenate(tabs, axis=0))

        def near_group(grp, c):
            near_logits(grp, gk)
            return c

        def pv(grp, width):
            e = jnp.exp(lbuf_ref[grp, :, 0:width] - jnp.concatenate([m_ref[...]] * (width // LANES), axis=1))
            acc_ref[g] += jnp.dot(e.astype(MXU), keys(vsa_ref, grp, width), preferred_element_type=F32)

        def pv_group(grp, c):
            pv(grp, gk)
            return c

        last = ngrp - 1
        short = i // 2 - SLC_UNROLL * last < SLC_UNROLL // 2
        lax.fori_loop(0, nplain, plain_group, 0)
        lax.fori_loop(nplain, last, near_group, 0)
        pl.when(short)(lambda: near_logits(last, gk // 2))
        pl.when(jnp.logical_not(short))(lambda: near_logits(last, gk))
        m_ref[...] = jnp.broadcast_to(jnp.max(m_ref[...], axis=1, keepdims=True), m_ref.shape)
        acc_ref[g] = jnp.zeros(acc_ref.shape[1:], F32)
        lax.fori_loop(0, last, pv_group, 0)
        pl.when(short)(lambda: pv(last, gk // 2))
        pl.when(jnp.logical_not(short))(lambda: pv(last, gk))
        return carry

    lax.fori_loop(0, NSA_KV_HEADS, group_body, 0)
    for g in range(NSA_KV_HEADS):
        gates = gate_ref[:, g * LANES:(g + 1) * LANES]
        ohs = []
        for r in range(NSA_GROUP):
            rows = slice(r * tq, (r + 1) * tq)
            ohs.append(_gate_mix(gates, r, o_cmp[g][rows], acc_ref[g, rows, 0:LANES], accw[g][rows, 0:LANES],
                                 acc_ref[g, rows, LANES:LANES + 1], accw[g][rows, LANES:LANES + 1]))
        o_ref[:, g * gw:(g + 1) * gw] = _unplace_heads(ohs, g)


def _nsa_prompt(rel_bias, q, gates, kc, vc, ksa, vsa, kw, vwa, nseq, seq_len):
    tq = Q_BLOCK
    nq = seq_len // tq
    ncp = kc.shape[1]
    gk = SLC_UNROLL * KT
    ng = NSA_KV_HEADS
    rows = NSA_GROUP * tq
    assert seq_len % gk == 0 and seq_len >= WIN_TILES * tq and ncp % LANES == 0
    per_seq = lambda width: pl.BlockSpec((seq_len, width), lambda b, i: (b, 0), pipeline_mode=pl.Buffered(1))
    blk = lambda width: pl.BlockSpec((tq, width), lambda b, i: (b * nq + i, 0))
    cmp_spec = pl.BlockSpec((1, ncp, LANES), lambda b, i: (b, 0, 0))
    return pl.pallas_call(
        functools.partial(_nsa_prompt_kernel, ncp=ncp),
        grid=(nseq, nq),
        in_specs=[pl.BlockSpec(memory_space=pltpu.SMEM), blk(NSA_INNER), blk(ng * LANES), cmp_spec, cmp_spec,
                  per_seq(2 * LANES), per_seq(2 * LANES), per_seq(LANES), per_seq(2 * LANES)],
        out_specs=blk(NSA_INNER),
        out_shape=jax.ShapeDtypeStruct((nseq * seq_len, NSA_INNER), F32),
        scratch_shapes=[pltpu.VMEM((NSA_HEADS * N_SK, tq, KT), F32), pltpu.VMEM((NSA_HEADS * N_WK, tq, LANES), F32),
                        pltpu.VMEM((NSA_HEADS, tq, LANES), F32), pltpu.VMEM((ncp, LANES), MXU),
                        pltpu.VMEM((seq_len // gk, rows, gk), F32), pltpu.VMEM((rows, LANES), F32),
                        pltpu.VMEM((ng, rows, 2 * LANES), F32), pltpu.VMEM((ng, rows, 2 * LANES), MXU)],
        compiler_params=_cparams("arbitrary", "arbitrary"),
    )(rel_bias, q, gates, kc, vc, ksa, vsa, kw, vwa)


TAIL = 2 * LANES


def _nsa_sample_tables(rb_ref, cbt_ref, tail_ref, wtab_ref, ov_ref, eall_ref, *, ls, past, ncp, wk, wkp, nk):
    heads = range(NSA_HEADS)

    def per_group(tabs):
        return [jnp.concatenate(tabs[g * NSA_GROUP:(g + 1) * NSA_GROUP], axis=0) for g in range(NSA_KV_HEADS)]

    t = lax.broadcasted_iota(jnp.int32, (ls, ncp), 0)
    k = lax.broadcasted_iota(jnp.int32, (ls, ncp), 1)
    dist = past + t - CMP_STRIDE * k - (CMP_BLOCK - 1)
    dist = jnp.where(k < ncp - 1, dist, -1)
    for g, tab in enumerate(per_group(_bias_tables(dist, rb_ref, heads))):
        cbt_ref[g] = tab
    t = lax.broadcasted_iota(jnp.int32, (ls, TAIL), 0)
    c = lax.broadcasted_iota(jnp.int32, (ls, TAIL), 1) + (nk - TAIL)
    dist = jnp.where(c < past + ls, past + t - c, -1)
    for g, tab in enumerate(per_group(_bias_tables(dist, rb_ref, heads))):
        tail_ref[g] = tab
    t = lax.broadcasted_iota(jnp.int32, (ls, wkp), 0)
    c = lax.broadcasted_iota(jnp.int32, (ls, wkp), 1)
    dist = past + t - (past - wk + c)
    dist = jnp.where((c < wk + ls) & (dist < WINDOW), dist, -1)
    for g, tab in enumerate(per_group(_bias_tables(dist, rb_ref, heads))):
        wtab_ref[g] = tab
    kk = lax.broadcasted_iota(jnp.int32, (ncp, 2 * LANES), 0)
    ss = lax.broadcasted_iota(jnp.int32, (ncp, 2 * LANES), 1)
    ratio = SLC_BLOCK // CMP_STRIDE
    nover = (CMP_BLOCK - 1) // CMP_STRIDE
    ov = (kk >= ratio * ss - nover) & (kk < ratio * ss + ratio) & (kk < ncp - 1)
    ov_ref[...] = jnp.where(ov, 1.0, 0.0).astype(ov_ref.dtype)
    s2 = lax.broadcasted_iota(jnp.int32, (2 * LANES, nk), 0)
    c2 = lax.broadcasted_iota(jnp.int32, (2 * LANES, nk), 1)
    eall_ref[...] = jnp.where(c2 // SLC_BLOCK == s2, 1.0, 0.0).astype(eall_ref.dtype)


def _softmax_rows(l):
    m = jnp.max(l, axis=1, keepdims=True)
    e = jnp.exp(l - m)
    return e / jnp.sum(e, axis=1, keepdims=True)


SAMPLE_SPS = 2


def _t128(x):
    rows = x.shape[0]
    if rows < LANES:
        x = jnp.concatenate([x, jnp.zeros((LANES - rows, LANES), x.dtype)], axis=0)
    return x.T


def _nsa_sample_kernel(pt_ref, rb_ref, q_ref, gate_ref, kvs_ref, kvw_ref, kc_ref, vc_ref, win_ref, pool_ref,
                       o_ref, kt_ref, vt_ref, sem_ref, cbt_ref, tail_ref, wtab_ref, ov_ref, eall_ref,
                       *, ls, past, npages, page_size, ncp, wk, wkp, nk):
    b = pl.program_id(0)

    @pl.when(b == 0)
    def _():
        _nsa_sample_tables(rb_ref, cbt_ref, tail_ref, wtab_ref, ov_ref, eall_ref,
                           ls=ls, past=past, ncp=ncp, wk=wk, wkp=wkp, nk=nk)

    sps = kt_ref.shape[1]

    def page_copies(step, slot, p):
        cols = pl.ds(pl.multiple_of(p * page_size, page_size), page_size)
        cps = []
        for j in range(sps):
            pg = pt_ref[(step * sps + j) * npages + p]
            cps.append(pltpu.make_async_copy(pool_ref.at[pg, 0], kt_ref.at[slot, j, :, cols], sem_ref.at[0, slot]))
            cps.append(pltpu.make_async_copy(pool_ref.at[pg, 1], vt_ref.at[slot, j, :, cols], sem_ref.at[1, slot]))
        return cps

    slot = _gather_step(page_copies, npages)
    gw = NSA_GROUP * HEAD_DIM
    ng = NSA_KV_HEADS
    nrow = ng * NSA_GROUP * ls
    stack = lambda xs: xs[0] if len(xs) == 1 else jnp.concatenate(xs, axis=0)
    seq_rows = lambda x, j: x[j * nrow:(j + 1) * nrow]
    per_group = lambda ref: stack([ref[g] for g in range(ng)] * sps)
    qall, kwt, vwt = [], [], []
    for j in range(sps):
        tok = slice(j * ls, (j + 1) * ls)
        kvs = kvs_ref[tok, :]
        kt_ref[slot, j, :, past:nk] = _t128(kvs[:, 0:KV_WIDTH])
        vt_ref[slot, j, :, past:nk] = _t128(kvs[:, KV_WIDTH:])
        kvw = kvw_ref[tok, :]
        kwt.append(jnp.concatenate([win_ref[j, 0], _t128(kvw[:, 0:KV_WIDTH])], axis=1).astype(MXU))
        vwt.append(jnp.concatenate([win_ref[j, 1], _t128(kvw[:, KV_WIDTH:])], axis=1).astype(MXU))
        qall.append(stack([t for g in range(ng) for t in _place_heads(q_ref[tok, g * gw:(g + 1) * gw], g)]).astype(MXU))
    pc = _softmax_rows(stack([_mm_nt(qall[j], kc_ref[j]) for j in range(sps)]) + per_group(cbt_ref)).astype(MXU)
    o_cmp = [jnp.dot(seq_rows(pc, j), vc_ref[j].astype(MXU), preferred_element_type=F32) for j in range(sps)]
    impm = jnp.dot(pc, ov_ref[...], preferred_element_type=F32)
    nsel = sps * ng * ls
    imp = stack([sum(impm[((j * ng + g) * NSA_GROUP + r) * ls:((j * ng + g) * NSA_GROUP + r + 1) * ls]
                     for r in range(NSA_GROUP)) for j in range(sps) for g in range(ng)])
    imp = jnp.concatenate([imp, jnp.zeros((LANES - nsel, imp.shape[1]), F32)], axis=0)
    qpos_lane = past + lax.broadcasted_iota(jnp.int32, (1, LANES), 1) % ls
    selneg = ((_top_blocks(imp, qpos_lane)[0:nsel] - 1.0) * BIG).astype(MXU)
    selneg = stack([selneg[jg * ls:(jg + 1) * ls] for jg in range(sps * ng) for _ in range(NSA_GROUP)])
    mask = jnp.dot(selneg, eall_ref[...], preferred_element_type=F32)
    lsl = stack([jnp.dot(qall[j], kt_ref[slot, j].astype(MXU), preferred_element_type=F32) for j in range(sps)]) + mask
    lsl = jnp.concatenate([lsl[:, 0:nk - TAIL], lsl[:, nk - TAIL:] + per_group(tail_ref)], axis=1)
    ps = _softmax_rows(lsl)
    o_slc = [_mm_nt(seq_rows(ps, j), vt_ref[slot, j]) for j in range(sps)]
    pw = _softmax_rows(stack([jnp.dot(qall[j], kwt[j], preferred_element_type=F32) for j in range(sps)])
                       + per_group(wtab_ref))
    o_win = [_mm_nt(seq_rows(pw, j), vwt[j]) for j in range(sps)]
    for j in range(sps):
        tok = slice(j * ls, (j + 1) * ls)
        for g in range(ng):
            gates = gate_ref[tok, g * LANES:(g + 1) * LANES]
            ohs = []
            for r in range(NSA_GROUP):
                rows = slice((g * NSA_GROUP + r) * ls, (g * NSA_GROUP + r + 1) * ls)
                ohs.append(_gate_mix(gates, r, o_cmp[j][rows], o_slc[j][rows], o_win[j][rows]))
            o_ref[tok, g * gw:(g + 1) * gw] = _unplace_heads(ohs, g)


def _nsa_sample(rel_bias, q, gates, kvs, kvw, kc, vc, win_t, pool_t, page_table, ls):
    nseq, npages = page_table.shape
    page_size = pool_t.shape[3]
    past = npages * page_size
    ncp = kc.shape[1]
    wk = win_t.shape[3]
    wkp = wk + LANES
    nk = past + LANES
    assert past % SLC_BLOCK == 0 and ls % 8 == 0 and ls <= SLC_BLOCK and past >= wk and past // SLC_BLOCK + 1 <= 2 * LANES
    assert ncp % LANES == 0 and past >= CMP_BLOCK and page_size % LANES == 0 and wk % LANES == 0 and KV_WIDTH == LANES
    sps = SAMPLE_SPS if nseq % SAMPLE_SPS == 0 else 1
    row = lambda width: pl.BlockSpec((sps * ls, width), lambda b, pt: (b, 0))
    seq3 = lambda n, width: pl.BlockSpec((sps, n, width), lambda b, pt: (b, 0, 0))
    kw = 2 * KV_WIDTH
    nrow = NSA_GROUP * ls
    return pl.pallas_call(
        functools.partial(_nsa_sample_kernel, ls=ls, past=past, npages=npages, page_size=page_size, ncp=ncp,
                          wk=wk, wkp=wkp, nk=nk),
        grid_spec=pltpu.PrefetchScalarGridSpec(
            num_scalar_prefetch=1,
            grid=(nseq // sps,),
            in_specs=[pl.BlockSpec(memory_space=pltpu.SMEM), row(NSA_INNER), row(NSA_KV_HEADS * LANES), row(kw), row(kw),
                      seq3(ncp, LANES), seq3(ncp, LANES),
                      pl.BlockSpec((sps, 2, KV_WIDTH, wk), lambda b, pt: (b, 0, 0, 0)), pl.BlockSpec(memory_space=pl.ANY)],
            out_specs=row(NSA_INNER),
            scratch_shapes=[pltpu.VMEM((2, sps, KV_WIDTH, nk), F32), pltpu.VMEM((2, sps, KV_WIDTH, nk), F32),
                            pltpu.SemaphoreType.DMA((2, 2)),
                            pltpu.VMEM((NSA_KV_HEADS, nrow, ncp), F32), pltpu.VMEM((NSA_KV_HEADS, nrow, TAIL), F32),
                            pltpu.VMEM((NSA_KV_HEADS, nrow, wkp), F32),
                            pltpu.VMEM((ncp, 2 * LANES), MXU), pltpu.VMEM((2 * LANES, nk), MXU)],
        ),
        out_shape=jax.ShapeDtypeStruct((nseq * ls, NSA_INNER), F32),
        compiler_params=_cparams("arbitrary"),
    )(page_table.reshape(-1), rel_bias, q, gates, kvs, kvw, kc, vc, win_t, pool_t)


def _mix_out_kernel(x_ref, yssd_ref, onsa_ref, gn_ref, wo_ref, gq_ref, wq_ref, mkv_ref, wmo_ref, *rest,
                    rows_per_seq, ffn, final_norm):
    if ffn:
        g2_ref, win_ref, wout_ref = rest[0:3]
        gf_ref = rest[3] if final_norm else None
        out_ref, x2_ref, h_ref = rest[-3:]
        o_ref = x2_ref
    else:
        o_ref, = rest
    y = jnp.concatenate([yssd_ref[...], _rms(onsa_ref[...], gn_ref[...])], axis=1)
    x = x_ref[...] + jnp.dot(y.astype(MXU), wo_ref[...], preferred_element_type=F32)
    qm = jnp.dot(_rms(x, gq_ref[...]).astype(MXU), wq_ref[...], preferred_element_type=F32)
    transposed = len(mkv_ref.shape) == 4
    spb = mkv_ref.shape[0]
    m = mkv_ref.shape[3] if transposed else mkv_ref.shape[1]
    if transposed:
        side = lambda kv_i: jnp.concatenate([mkv_ref[s, kv_i] for s in range(spb)], axis=1).astype(MXU)
        km, vm = side(0), side(1)
        logits = lambda qh: jnp.dot(qh.astype(MXU), km, preferred_element_type=F32)
        attend = lambda p: _mm_nt(p, vm)
    else:
        kv = mkv_ref[...].reshape(spb * m, 2 * MEM_INNER)
        km = kv[:, 0:MEM_INNER].astype(MXU)
        vm = kv[:, MEM_INNER:].astype(MXU)
        logits = lambda qh: _mm_nt(qh, km)
        attend = lambda p: jnp.dot(p.astype(MXU), vm, preferred_element_type=F32)
    head = lax.broadcasted_iota(jnp.int32, qm.shape, 1) // HEAD_DIM
    if spb > 1:
        rseq = lax.broadcasted_iota(jnp.int32, (qm.shape[0], spb * m), 0) // rows_per_seq
        kseq = lax.broadcasted_iota(jnp.int32, (qm.shape[0], spb * m), 1) // m
        own = rseq == kseq
    o = jnp.zeros(qm.shape, F32)
    for h in range(MEM_HEADS):
        lg = logits(jnp.where(head == h, qm, 0.0)) * (HEAD_DIM ** -0.5)
        if spb > 1:
            lg = jnp.where(own, lg, NEG)
        o = o + jnp.where(head == h, attend(_softmax_rows(lg)), 0.0)
    o_ref[...] = x + jnp.dot(o.astype(MXU), wmo_ref[...], preferred_element_type=F32)
    if ffn:
        _ffn_apply(x2_ref, g2_ref, win_ref, wout_ref, gf_ref, out_ref, h_ref)


def _mix_out(x, yssd, onsa, g_nsa, w_out, g_memq, w_mem_q, mem_kv, w_mem_out, nseq, seq_len, ffn=None,
             tm=512, short_rows=64):
    t, d = x.shape
    mkv_block = (lambda spb: (spb,) + mem_kv.shape[1:])
    if seq_len < short_rows and nseq % (short_rows // seq_len) == 0:
        spb = short_rows // seq_len
        tm, nt = spb * seq_len, 1
    else:
        spb, tm = 1, min(tm, seq_len)
        nt = seq_len // tm
        assert nt * tm == seq_len
    row = lambda width: pl.BlockSpec((tm, width), lambda b, i: (b * nt + i, 0))
    args = [x, yssd, onsa, g_nsa.reshape(1, -1), w_out.astype(MXU), g_memq.reshape(1, d), w_mem_q.astype(MXU),
            mem_kv, w_mem_out.astype(MXU)]
    specs = [row(d), row(SSD_INNER), row(NSA_INNER), _const_spec((1, NSA_INNER)), _const_spec(w_out.shape),
             _const_spec((1, d)), _const_spec(w_mem_q.shape),
             pl.BlockSpec(mkv_block(spb), lambda b, i: (b,) + (0,) * (mem_kv.ndim - 1)),
             _const_spec(w_mem_out.shape)]
    scratch = []
    final_norm = False
    if ffn:
        g2, w_ffn_in, w_ffn_out, g_final = ffn
        assert w_ffn_out.shape[0] % FFN_CHUNK == 0
        final_norm = g_final is not None
        args += [g2.reshape(1, d), w_ffn_in.astype(MXU), w_ffn_out.astype(MXU)]
        specs += [_const_spec((1, d)), _const_spec(w_ffn_in.shape), _const_spec(w_ffn_out.shape)]
        if final_norm:
            args.append(g_final.reshape(1, d))
            specs.append(_const_spec((1, d)))
        scratch = [pltpu.VMEM((tm, d), F32), pltpu.VMEM((tm, d), MXU)]
    return pl.pallas_call(
        functools.partial(_mix_out_kernel, rows_per_seq=seq_len, ffn=bool(ffn), final_norm=final_norm),
        grid=(nseq // spb, nt),
        in_specs=specs,
        out_specs=row(d),
        out_shape=jax.ShapeDtypeStruct((t, d), F32),
        scratch_shapes=scratch,
        compiler_params=_cparams("parallel", "parallel"),
    )(*args)


def _norm_matmul_kernel(x_ref, g_ref, w_ref, o_ref):
    o_ref[...] = jnp.dot(_rms(x_ref[...], g_ref[...]).astype(MXU), w_ref[...], preferred_element_type=F32)


def _norm_matmul(x, g, w, tm=256):
    t, d = x.shape
    n = w.shape[1]
    tm = min(tm, t)
    assert t % tm == 0
    return pl.pallas_call(
        _norm_matmul_kernel,
        grid=(t // tm,),
        in_specs=[pl.BlockSpec((tm, d), lambda i: (i, 0)), _const_spec((1, d)), _const_spec(w.shape)],
        out_specs=pl.BlockSpec((tm, n), lambda i: (i, 0)),
        out_shape=jax.ShapeDtypeStruct((t, n), F32),
        compiler_params=_cparams("parallel"),
    )(x, g.reshape(1, d), w.astype(MXU))


def _trunk_layer(x, lw, cw, rel_bias, mem_kv, conv0, h0, nseq, seq_len, ffn2, sample=None):
    x1 = _ffn(x, lw['norm_ffn1'], lw['w_ffn1_in'], lw['w_ffn1_out'])
    outs = _proj(x1, lw['norm_mix'], lw['w_mix'], seq_len=None if sample else seq_len)
    z, xbc, q, kvc, kvs, kvw, gates, dt = outs[:8]
    y_ssd, h_new, conv_new = _ssd(z, xbc, dt, conv0, h0, lw['conv_w'], lw['conv_b'], lw['dt_bias'], lw['a_log'],
                                  lw['d_skip'], lw['norm_ssd_out'], nseq, seq_len)
    rows5 = lambda a: a.reshape(nseq, seq_len, 2, NSA_KV_HEADS, HEAD_DIM)
    if sample is None:
        kc, vc = _compress_prompt(kvc, cw, nseq, seq_len)
        o_nsa = _nsa_prompt(rel_bias, q, gates, kc, vc, *outs[8:12], nseq, seq_len)
        from_t = lambda a: jnp.transpose(a.reshape(nseq, 2, NSA_KV_HEADS, HEAD_DIM, seq_len), (0, 4, 1, 2, 3))
        kv_c, kv_s = from_t(outs[12]), from_t(outs[13])
    else:
        kc, vc = _compress_sample(sample['pool_c'], sample['page_table'], cw)
        o_nsa = _nsa_sample(rel_bias, q, gates, kvs, kvw, kc, vc, sample['win_t'], sample['pool_s'],
                            sample['page_table'], seq_len)
        kv_c, kv_s = rows5(kvc), rows5(kvs)
    if sample is None:
        x2 = _mix_out(x1, y_ssd, o_nsa, lw['norm_nsa_out'], lw['w_out'], lw['norm_mem_q'], lw['w_mem_q'], mem_kv,
                      lw['w_mem_out'], nseq, seq_len, ffn=ffn2)
    else:
        x2 = _mix_out(x1, y_ssd, o_nsa, lw['norm_nsa_out'], lw['w_out'], lw['norm_mem_q'], lw['w_mem_q'], mem_kv,
                      lw['w_mem_out'], nseq, seq_len)
        x2 = _ffn(x2, *ffn2)
    keep = min(WINDOW, seq_len)
    kv_w = kvw.reshape(nseq, seq_len, -1)[:, seq_len - keep:].reshape(nseq, keep, 2, NSA_KV_HEADS, HEAD_DIM)
    return x2, kv_c, kv_s, kv_w, h_new, conv_new


def kernel(x_prompt, x_sample, cache_kv_cmp, cache_kv_slc, state_win_kv, state_ssd, state_conv, cache_mem_kv, page_table, mem_prompt, norm_ffn1, w_ffn1_in, w_ffn1_out, norm_mix, w_in, conv_w, conv_b, dt_bias, a_log, d_skip, norm_ssd_out, cmp_pos_k, cmp_w1_k, cmp_w2_k, cmp_pos_v, cmp_w1_v, cmp_w2_v, norm_nsa_out, w_out, norm_mem_q, norm_mem_kv, w_mem_q, w_mem_kv, w_mem_out, norm_ffn2, w_ffn2_in, w_ffn2_out, rel_bias, norm_final):
    bp, lp, d = x_prompt.shape
    bs, lsq, _ = x_sample.shape
    depth = w_in.shape[0]
    xp = x_prompt.reshape(bp * lp, d)
    xs = x_sample.reshape(bs * lsq, d)
    outs_p = [[] for _ in range(6)]
    outs_s = [[] for _ in range(5)]
    for layer in range(depth):
        last = layer == depth - 1
        lw = {
            'norm_ffn1': norm_ffn1[layer], 'w_ffn1_in': w_ffn1_in[layer], 'w_ffn1_out': w_ffn1_out[layer],
            'norm_mix': norm_mix[layer], 'w_mix': _regroup_w_in(w_in[layer]), 'conv_w': conv_w[layer],
            'conv_b': conv_b[layer], 'dt_bias': dt_bias[layer], 'a_log': a_log[layer], 'd_skip': d_skip[layer],
            'norm_ssd_out': norm_ssd_out[layer], 'norm_nsa_out': norm_nsa_out[layer], 'w_out': w_out[layer],
            'norm_mem_q': norm_mem_q[layer], 'w_mem_q': w_mem_q[layer], 'w_mem_out': w_mem_out[layer],
        }
        cw = _compress_weights(cmp_pos_k[layer], cmp_w1_k[layer], cmp_w2_k[layer],
                               cmp_pos_v[layer], cmp_w1_v[layer], cmp_w2_v[layer])
        g_final = norm_final if last else None
        mem_kv_p = _norm_matmul(mem_prompt.reshape(-1, d), norm_mem_kv[layer], w_mem_kv[layer])
        mem_kv_p = mem_kv_p.reshape(bp, -1, 2 * MEM_INNER)
        conv0 = jnp.zeros((bp, SSD_CONV - 1, SSD_CONV_DIM), F32)
        h0 = jnp.zeros((bp, SSD_HEADS, HEAD_DIM, SSD_STATE), F32)
        ffn2 = (norm_ffn2[layer], w_ffn2_in[layer], w_ffn2_out[layer], g_final)
        xp, c_p, s_p, w_p, h_p, conv_p = _trunk_layer(xp, lw, cw, rel_bias, mem_kv_p, conv0, h0, bp, lp, ffn2)
        sample = dict(pool_c=_paged_view(cache_kv_cmp[layer]), pool_s=_paged_view(cache_kv_slc[layer]),
                      win_t=_paged_view(state_win_kv[layer]), page_table=page_table)
        mem_kv_s = jnp.transpose(cache_mem_kv[layer], (0, 2, 3, 4, 1)).reshape(bs, 2, MEM_INNER, -1)
        xs, c_s, s_s, w_s, h_s, conv_s = _trunk_layer(xs, lw, cw, rel_bias, mem_kv_s, state_conv[layer],
                                                      state_ssd[layer], bs, lsq, ffn2, sample=sample)
        w_s = jnp.concatenate([state_win_kv[layer], w_s], axis=1)[:, w_s.shape[1]:]
        for lst, v in zip(outs_p, (c_p, s_p, w_p, h_p, conv_p, mem_kv_p.reshape(bp, -1, 2, MEM_HEADS, HEAD_DIM))):
            lst.append(v)
        for lst, v in zip(outs_s, (c_s, s_s, w_s, h_s, conv_s)):
            lst.append(v)
    return (xp.reshape(bp, lp, d), xs.reshape(bs, lsq, d),
            *(jnp.stack(o) for o in outs_p), *(jnp.stack(o) for o in outs_s))
```

```python
import functools
import math

import numpy as np
import jax
import jax.numpy as jnp
from jax import lax
from jax.experimental import pallas as pl
from jax.experimental.pallas import tpu as pltpu

F32 = jnp.float32
MXU = jnp.bfloat16

HEAD_DIM = 64
SSD_HEADS = 8
SSD_GROUPS = 2
SSD_STATE = 128
SSD_CONV = 4
SSD_CHUNK = 128
NSA_HEADS = 8
NSA_KV_HEADS = 2
NSA_GROUP = NSA_HEADS // NSA_KV_HEADS
CMP_BLOCK = 32
CMP_STRIDE = 16
CMP_HIDDEN = 256
SLC_BLOCK = 64
SLC_TOPN = 16
WINDOW = 512
REL_BUCKETS = 32
REL_MAX_DIST = 128
MEM_HEADS = 4
Q_BLOCK = 128
EPS = 1e-6
NEG = -1e30

SSD_INNER = SSD_HEADS * HEAD_DIM
NSA_INNER = NSA_HEADS * HEAD_DIM
KV_WIDTH = NSA_KV_HEADS * HEAD_DIM
SSD_CONV_DIM = SSD_INNER + 2 * SSD_GROUPS * SSD_STATE
MEM_INNER = MEM_HEADS * HEAD_DIM
LANES = 128
VMEM_LIMIT = 56 * 1024 * 1024


def _cparams(*sem):
    return pltpu.CompilerParams(dimension_semantics=sem, vmem_limit_bytes=VMEM_LIMIT)


def _const_spec(shape):
    nd = len(shape)
    return pl.BlockSpec(shape, lambda *a: (0,) * nd, pipeline_mode=pl.Buffered(1))


def _mm(a, b):
    return jnp.dot(a.astype(MXU), b.astype(MXU), preferred_element_type=F32)


def _mm_nt(a, b):
    return lax.dot_general(a.astype(MXU), b.astype(MXU), (((1,), (1,)), ((), ())), preferred_element_type=F32)


def _rms(x, g):
    return x * lax.rsqrt(jnp.mean(x * x, axis=-1, keepdims=True) + EPS) * g


def _silu(x):
    return x * jax.nn.sigmoid(x)


FFN_CHUNK = 256


def _ffn_apply(x_ref, g_ref, win_ref, wout_ref, gf_ref, o_ref, h_ref):
    final_norm = gf_ref is not None
    nchunks = wout_ref.shape[0] // FFN_CHUNK
    x = x_ref[...]
    h_ref[...] = _rms(x, g_ref[...]).astype(h_ref.dtype)
    f = nchunks * FFN_CHUNK
    for j in range(nchunks):
        cols = slice(j * FFN_CHUNK, (j + 1) * FFN_CHUNK)
        h = h_ref[...]
        gate = jnp.dot(h, win_ref[:, cols], preferred_element_type=F32)
        up = jnp.dot(h, win_ref[:, f + j * FFN_CHUNK:f + (j + 1) * FFN_CHUNK], preferred_element_type=F32)
        a = (_silu(gate) * up).astype(h.dtype)
        t = jnp.dot(a, wout_ref[cols, :], preferred_element_type=F32)
        if j == 0:
            o_ref[...] = t
        else:
            o_ref[...] += t
    y = x_ref[...] + 0.5 * o_ref[...]
    if final_norm:
        y = _rms(y, gf_ref[...])
    o_ref[...] = y


def _ffn_kernel(x_ref, g_ref, win_ref, wout_ref, *rest, final_norm):
    gf_ref, o_ref, h_ref = rest if final_norm else (None,) + rest
    _ffn_apply(x_ref, g_ref, win_ref, wout_ref, gf_ref, o_ref, h_ref)


def _ffn(x, g, w_in, w_out, g_final=None, tm=512):
    t, d = x.shape
    f = w_out.shape[0]
    nchunks = f // FFN_CHUNK
    tm = min(tm, t)
    assert nchunks * FFN_CHUNK == f and t % tm == 0
    win = w_in.astype(MXU)
    wout = w_out.astype(MXU)
    final_norm = g_final is not None
    args = [x, g.reshape(1, d), win, wout]
    specs = [pl.BlockSpec((tm, d), lambda i: (i, 0)), _const_spec((1, d)),
             _const_spec(win.shape), _const_spec(wout.shape)]
    if final_norm:
        args.append(g_final.reshape(1, d))
        specs.append(_const_spec((1, d)))
    return pl.pallas_call(
        functools.partial(_ffn_kernel, final_norm=final_norm),
        grid=(t // tm,),
        in_specs=specs,
        out_specs=pl.BlockSpec((tm, d), lambda i: (i, 0)),
        out_shape=jax.ShapeDtypeStruct((t, d), F32),
        scratch_shapes=[pltpu.VMEM((tm, d), MXU)],
        compiler_params=_cparams("parallel"),
    )(*args)


PROJ_COLS = 3200
_O_Z, _O_XBC, _O_Q, _O_KVC, _O_KVS, _O_KVW, _O_G, _O_DT = 0, 512, 1536, 2048, 2304, 2560, 2816, 3072


def _regroup_w_in(w_in):
    o = np.cumsum([0, SSD_INNER, SSD_CONV_DIM, SSD_HEADS, NSA_INNER] + [KV_WIDTH] * 6 + [3 * NSA_HEADS])
    z, xbc, dt, q, kvs, gl = (w_in[:, o[0]:o[1]], w_in[:, o[1]:o[2]], w_in[:, o[2]:o[3]], w_in[:, o[3]:o[4]],
                              w_in[:, o[4]:o[10]], w_in[:, o[10]:o[11]])
    pad = lambda w: jnp.pad(w, ((0, 0), (0, LANES - w.shape[1])))
    ng = 3 * NSA_GROUP
    gls = [pad(gl[:, g * ng:(g + 1) * ng]) for g in range(NSA_KV_HEADS)]
    return jnp.concatenate([z, xbc, q, kvs] + gls + [pad(dt)], axis=1).astype(MXU)


def _proj_kernel(x_ref, g_ref, w_ref, z_ref, xbc_ref, q_ref, kvc_ref, kvs_ref, kvw_ref, gate_ref, dt_ref, *aug,
                 tiles_per_seq):
    h = _rms(x_ref[...], g_ref[...]).astype(w_ref.dtype)
    res = jnp.dot(h, w_ref[...], preferred_element_type=F32)
    tm = res.shape[0]
    z_ref[...] = res[:, _O_Z:_O_XBC]
    xbc_ref[...] = res[:, _O_XBC:_O_Q]
    q_ref[...] = res[:, _O_Q:_O_KVC] * (HEAD_DIM ** -0.5)
    kvc_ref[...] = res[:, _O_KVC:_O_KVS]
    kvs = res[:, _O_KVS:_O_KVW]
    kvw = res[:, _O_KVW:_O_G]
    kvs_ref[...] = kvs
    kvw_ref[...] = kvw
    gate_ref[...] = jax.nn.sigmoid(res[:, _O_G:_O_DT])
    dt_ref[...] = res[:, _O_DT:PROJ_COLS]
    if aug:
        ksa_ref, vsa_ref, kw_ref, vwa_ref, kvct_ref, kvst_ref = aug
        kvct_ref[0] = res[:, _O_KVC:_O_KVS].T
        kvst_ref[0] = kvs.T
        lane = lax.broadcasted_iota(jnp.int32, (tm, LANES), 1)
        row = lax.broadcasted_iota(jnp.int32, (tm, LANES), 0)
        pos = (pl.program_id(0) % tiles_per_seq) * tm + row
        onehot = jnp.where(lane == pos // SLC_BLOCK, 1.0, 0.0)
        ones_col = jnp.where(lane == 0, 1.0, 0.0)
        dt_ = ksa_ref.dtype
        ksa_ref[:, 0:LANES] = kvs[:, 0:LANES].astype(dt_)
        ksa_ref[:, LANES:] = onehot.astype(dt_)
        vsa_ref[:, 0:LANES] = kvs[:, LANES:].astype(dt_)
        vsa_ref[:, LANES:] = ones_col.astype(dt_)
        kw_ref[...] = kvw[:, 0:LANES].astype(dt_)
        vwa_ref[:, 0:LANES] = kvw[:, LANES:].astype(dt_)
        vwa_ref[:, LANES:] = ones_col.astype(dt_)


def _proj(x, g, w, seq_len=None, tm=512):
    t, d = x.shape
    tm = min(tm, t)
    assert t % tm == 0
    widths = [512, 1024, 512, 256, 256, 256, NSA_KV_HEADS * LANES, LANES]
    shapes = [jax.ShapeDtypeStruct((t, wd), F32) for wd in widths]
    specs = [pl.BlockSpec((tm, wd), lambda i: (i, 0)) for wd in widths]
    aug = seq_len is not None
    tps = seq_len // tm if aug else 1
    if aug:
        assert seq_len % tm == 0 and seq_len // SLC_BLOCK <= LANES
        for wd in (256, 256, LANES, 256):
            shapes.append(jax.ShapeDtypeStruct((t, wd), MXU))
            specs.append(pl.BlockSpec((tm, wd), lambda i: (i, 0)))
        for _ in range(2):
            shapes.append(jax.ShapeDtypeStruct((t // seq_len, 2 * KV_WIDTH, seq_len), F32))
            specs.append(pl.BlockSpec((1, 2 * KV_WIDTH, tm), lambda i: (i // tps, 0, i % tps)))
    return pl.pallas_call(
        functools.partial(_proj_kernel, tiles_per_seq=tps),
        grid=(t // tm,),
        in_specs=[pl.BlockSpec((tm, d), lambda i: (i, 0)), _const_spec((1, d)), _const_spec(w.shape)],
        out_specs=specs,
        out_shape=shapes,
        compiler_params=_cparams("parallel"),
    )(x, g.reshape(1, d), w)


HALO = 8


def _split3(x):
    hi = x.astype(jnp.bfloat16).astype(F32)
    r = x - hi
    mid = r.astype(jnp.bfloat16).astype(F32)
    return hi, mid, r - mid


def _ssd_kernel(z_ref, xbc_ref, dt_ref, conv0_ref, h0_ref, cw_ref, cb_ref, dtb_ref, alog_ref, dskip_ref, g_ref,
                y_ref, hout_ref, convout_ref, xwin_ref, h_ref, *, lv):
    q = SSD_CHUNK
    c = pl.program_id(1)

    @pl.when(c == 0)
    def _():
        xwin_ref[0:HALO, :] = conv0_ref[0]
        h_ref[...] = h0_ref[0]

    def padrows(v):
        if lv == q:
            return v
        return jnp.concatenate([v, jnp.zeros((q - lv, v.shape[1]), v.dtype)], axis=0)

    xwin_ref[HALO:HALO + q, :] = padrows(xbc_ref[...])
    acc = cb_ref[...] + cw_ref[SSD_CONV - 1:SSD_CONV, :] * xwin_ref[HALO:HALO + q, :]
    for k in range(SSD_CONV - 1):
        acc = acc + cw_ref[k:k + 1, :] * xwin_ref[pl.ds(HALO - (SSD_CONV - 1) + k, q), :]
    xc = _silu(acc)
    tail = xwin_ref[lv:lv + HALO, :]
    xwin_ref[0:HALO, :] = tail
    convout_ref[0] = tail

    xs = xc[:, 0:SSD_INNER]
    rowi = lax.broadcasted_iota(jnp.int32, (q, q), 0)
    coli = lax.broadcasted_iota(jnp.int32, (q, q), 1)
    causal = rowi >= coli
    tri = jnp.where(causal, 1.0, 0.0).astype(jnp.bfloat16)

    xdt = padrows(dt_ref[...]) + dtb_ref[...]
    dt = jnp.maximum(xdt, 0.0) + jnp.log1p(jnp.exp(-jnp.abs(xdt)))
    if lv != q:
        dt = jnp.where(rowi < lv, dt, 0.0)
    da = dt * (-jnp.exp(alog_ref[...]))
    acs = sum(jnp.dot(tri, p.astype(jnp.bfloat16), preferred_element_type=F32) for p in _split3(da))
    acs_t = acs.T
    dt_t = dt.T
    last = acs[q - 1:q, :]
    last_t = acs_t[:, q - 1:q]
    xs_t = [xs[:, p * LANES:(p + 1) * LANES].T for p in range(SSD_HEADS // 2)]
    lane_hi = lax.broadcasted_iota(jnp.int32, (q, LANES), 1) >= HEAD_DIM
    row_hi = lax.broadcasted_iota(jnp.int32, (LANES, q), 0) >= HEAD_DIM
    hpg = SSD_HEADS // SSD_GROUPS
    ys = []
    for g in range(SSD_GROUPS):
        bm = xc[:, SSD_INNER + g * SSD_STATE:SSD_INNER + (g + 1) * SSD_STATE]
        cm = xc[:, SSD_INNER + (SSD_GROUPS + g) * SSD_STATE:SSD_INNER + (SSD_GROUPS + g + 1) * SSD_STATE]
        cb = _mm_nt(cm, bm)
        for pp in range(hpg // 2):
            p = g * (hpg // 2) + pp
            xpair = xs[:, p * LANES:(p + 1) * LANES]
            yi = []
            for hh in range(2):
                h = 2 * p + hh
                seg = acs[:, h:h + 1] - acs_t[h:h + 1, :]
                decay = jnp.where(causal, jnp.exp(jnp.where(causal, seg, 0.0)), 0.0)
                yi.append(_mm(cb * decay * dt_t[h:h + 1, :], xpair))
            y_intra = jnp.where(lane_hi, yi[1], yi[0])
            h0, h1 = 2 * p, 2 * p + 1
            grow = jnp.where(lane_hi, jnp.exp(acs[:, h1:h1 + 1]), jnp.exp(acs[:, h0:h0 + 1]))
            y_inter = _mm_nt(cm, h_ref[p]) * grow
            ys.append(y_intra + y_inter)
            te = jnp.where(row_hi,
                           jnp.exp(last_t[h1:h1 + 1, :] - acs_t[h1:h1 + 1, :]) * dt_t[h1:h1 + 1, :],
                           jnp.exp(last_t[h0:h0 + 1, :] - acs_t[h0:h0 + 1, :]) * dt_t[h0:h0 + 1, :])
            states = _mm(xs_t[p] * te, bm)
            rdec = jnp.where(row_hi[:, 0:1], jnp.exp(last_t[h1:h1 + 1, :]), jnp.exp(last_t[h0:h0 + 1, :]))
            h_ref[p] = h_ref[p] * rdec + states
    y = jnp.concatenate(ys, axis=1) + dskip_ref[...] * xs
    y = _rms(y * _silu(padrows(z_ref[...])), g_ref[...])
    y_ref[...] = y[0:lv, :]
    hout_ref[0] = h_ref[...]


def _ssd(z, xbc, dt, conv0, h0, conv_w, conv_b, dt_bias, a_log, d_skip, norm_g, nseq, seq_len):
    q = SSD_CHUNK
    lv = min(q, seq_len)
    nch = seq_len // lv
    assert nch * lv == seq_len
    t = nseq * seq_len
    cdim = SSD_CONV_DIM
    keep = SSD_CONV - 1
    conv0p = jnp.pad(conv0, ((0, 0), (HALO - keep, 0), (0, 0)))
    padl = lambda v: jnp.pad(v.reshape(1, -1), ((0, 0), (0, LANES - v.size)))
    hp = h0.reshape(nseq, SSD_HEADS // 2, 2 * HEAD_DIM, SSD_STATE)
    row = lambda b, c: (b * nch + c, 0)
    y, hout, convout = pl.pallas_call(
        functools.partial(_ssd_kernel, lv=lv),
        grid=(nseq, nch),
        in_specs=[pl.BlockSpec((lv, SSD_INNER), row), pl.BlockSpec((lv, cdim), row), pl.BlockSpec((lv, LANES), row),
                  pl.BlockSpec((1, HALO, cdim), lambda b, c: (b, 0, 0)),
                  pl.BlockSpec((1,) + hp.shape[1:], lambda b, c: (b, 0, 0, 0)),
                  _const_spec((SSD_CONV, cdim)), _const_spec((1, cdim)), _const_spec((1, LANES)),
                  _const_spec((1, LANES)), _const_spec((1, SSD_INNER)), _const_spec((1, SSD_INNER))],
        out_specs=[pl.BlockSpec((lv, SSD_INNER), row),
                   pl.BlockSpec((1,) + hp.shape[1:], lambda b, c: (b, 0, 0, 0)),
                   pl.BlockSpec((1, HALO, cdim), lambda b, c: (b, 0, 0))],
        out_shape=[jax.ShapeDtypeStruct((t, SSD_INNER), F32), jax.ShapeDtypeStruct(hp.shape, F32),
                   jax.ShapeDtypeStruct((nseq, HALO, cdim), F32)],
        scratch_shapes=[pltpu.VMEM((HALO + q, cdim), F32), pltpu.VMEM(hp.shape[1:], F32)],
        compiler_params=_cparams("parallel", "arbitrary"),
    )(z, xbc, dt, conv0p, hp, conv_w, conv_b.reshape(1, cdim), padl(dt_bias), padl(a_log),
      jnp.repeat(d_skip, HEAD_DIM).reshape(1, SSD_INNER), norm_g.reshape(1, SSD_INNER))
    return y, hout.reshape(nseq, SSD_HEADS, HEAD_DIM, SSD_STATE), convout[:, HALO - keep:, :]


def _compress_weights(pe_k, w1_k, w2_k, pe_v, w1_v, w2_v):
    def w1_pair(w1):
        w = w1.reshape(CMP_BLOCK, HEAD_DIM, CMP_HIDDEN)
        zz = jnp.zeros_like(w)
        w = jnp.concatenate([jnp.concatenate([w, zz], axis=2), jnp.concatenate([zz, w], axis=2)], axis=1)
        return w.reshape(CMP_BLOCK // 2, 2 * KV_WIDTH, NSA_KV_HEADS * CMP_HIDDEN).astype(MXU)

    def w2_pair(w2):
        zz = jnp.zeros_like(w2)
        return jnp.concatenate([jnp.concatenate([w2, zz], axis=1), jnp.concatenate([zz, w2], axis=1)], axis=0).astype(MXU)

    pe = jnp.concatenate([pe_k, pe_k, pe_v, pe_v], axis=1)
    return pe, w1_pair(w1_k), w1_pair(w1_v), w2_pair(w2_k), w2_pair(w2_v)


def _compress_first_layer(row_of_chunks, pe_ref, wk_ref, wv_ref):
    half = CMP_STRIDE
    a0k = a1k = a0v = a1v = None
    for s2 in range(half // 2):
        rows = [row_of_chunks(2 * s2 + u) for u in range(2)]

        def lhs(j, kv):
            lanes = slice(kv * LANES, (kv + 1) * LANES)
            return jnp.concatenate([(rows[u][kv] + pe_ref[j * half + 2 * s2 + u:j * half + 2 * s2 + u + 1, lanes])
                                    for u in range(2)], axis=1).astype(MXU)

        t0k = jnp.dot(lhs(0, 0), wk_ref[s2], preferred_element_type=F32)
        t0v = jnp.dot(lhs(0, 1), wv_ref[s2], preferred_element_type=F32)
        t1k = jnp.dot(lhs(1, 0), wk_ref[half // 2 + s2], preferred_element_type=F32)
        t1v = jnp.dot(lhs(1, 1), wv_ref[half // 2 + s2], preferred_element_type=F32)
        a0k, a1k = (t0k, t1k) if s2 == 0 else (a0k + t0k, a1k + t1k)
        a0v, a1v = (t0v, t1v) if s2 == 0 else (a0v + t0v, a1v + t1v)
    return a0k, a1k, a0v, a1v


def _compress_compute(rows_of_parts, pe_ref, wk_ref, wv_ref, w2k_ref, w2v_ref, sh_ref, kc_ref, vc_ref, nc):
    parts = [_compress_first_layer(f, pe_ref, wk_ref, wv_ref) for f in rows_of_parts]
    a0k, a1k, a0v, a1v = (p[0] if len(parts) == 1 else jnp.concatenate(p, axis=0) for p in zip(*parts))
    sh_ref[nc:nc + 8, :] = jnp.zeros((8, sh_ref.shape[1]), F32)
    for a0, a1, w2_ref, o_ref in ((a0k, a1k, w2k_ref, kc_ref), (a0v, a1v, w2v_ref, vc_ref)):
        sh_ref[0:nc, :] = a1
        hid = a0 + sh_ref[pl.ds(1, nc), :]
        o_ref[...] = jnp.dot(_silu(hid).astype(MXU), w2_ref[...], preferred_element_type=F32)


def _compress_prompt_kernel(rows_ref, pe_ref, wk_ref, wv_ref, w2k_ref, w2v_ref, kc_ref, vc_ref, sh_ref, *, nc):
    rw = 2 * KV_WIDTH

    def row_of_chunks(s):
        return rows_ref[:, s * rw:s * rw + LANES], rows_ref[:, s * rw + LANES:(s + 1) * rw]

    _compress_compute([row_of_chunks], pe_ref, wk_ref, wv_ref, w2k_ref, w2v_ref, sh_ref, kc_ref.at[0], vc_ref.at[0], nc)


def _compress_specs(cw):
    return [_const_spec(w.shape) for w in cw]


def _compress_prompt(kv_cmp, cw, nseq, seq_len):
    nc = seq_len // CMP_STRIDE
    out = jax.ShapeDtypeStruct((nseq, nc, LANES), F32)
    return pl.pallas_call(
        functools.partial(_compress_prompt_kernel, nc=nc),
        grid=(nseq,),
        in_specs=[pl.BlockSpec((nc, CMP_STRIDE * 2 * KV_WIDTH), lambda b: (b, 0))] + _compress_specs(cw),
        out_specs=[pl.BlockSpec((1, nc, LANES), lambda b: (b, 0, 0))] * 2,
        out_shape=[out, out],
        scratch_shapes=[pltpu.VMEM((nc + 8, NSA_KV_HEADS * CMP_HIDDEN), F32)],
        compiler_params=_cparams("parallel"),
    )(kv_cmp.reshape(nseq * nc, CMP_STRIDE * 2 * KV_WIDTH), *cw)


PAGE_UNROLL = 8


def _gather_step(page_copies, npages):
    b = pl.program_id(0)
    nb = pl.num_programs(0)
    slot = b % 2
    assert npages % PAGE_UNROLL == 0

    def for_pages(fn):
        def body(q, c):
            for u in range(PAGE_UNROLL):
                fn(q * PAGE_UNROLL + u)
            return c
        lax.fori_loop(0, npages // PAGE_UNROLL, body, 0)

    def start_all(seq, sl):
        for_pages(lambda p: [cp.start() for cp in page_copies(seq, sl, p)])

    @pl.when(b == 0)
    def _():
        start_all(0, 0)

    @pl.when(b + 1 < nb)
    def _():
        start_all(b + 1, 1 - slot)

    for_pages(lambda p: [cp.wait() for cp in page_copies(b, slot, p)])
    return slot


def _paged_view(cache):
    n_pool, page = cache.shape[:2]
    return jnp.transpose(cache, (0, 2, 3, 4, 1)).reshape(n_pool, 2, KV_WIDTH, page)


CMP_PARTS = 4


def _compress_sample_kernel(pt_ref, pool_ref, pe_ref, wk_ref, wv_ref, w2k_ref, w2v_ref, kc_ref, vc_ref,
                            buf_ref, sem_ref, rk_ref, rv_ref, sh_ref, *, nc, npages, page_size):
    def page_copies(seq, slot, p):
        return [pltpu.make_async_copy(pool_ref.at[pt_ref[seq * npages + p]], buf_ref.at[slot, p], sem_ref.at[slot])]

    slot = _gather_step(page_copies, npages)
    nparts = rk_ref.shape[0]
    ppp = npages // nparts
    ncp = nc // nparts
    rows_of_parts = []
    for part in range(nparts):
        for u in range(ppp):
            rows = slice(u * page_size, (u + 1) * page_size)
            rk_ref[part, rows, :] = buf_ref[slot, part * ppp + u, 0].T
            rv_ref[part, rows, :] = buf_ref[slot, part * ppp + u, 1].T
        rows_of_parts.append(lambda s, part=part: (rk_ref[part, pl.ds(s, ncp, stride=CMP_STRIDE), :],
                                                   rv_ref[part, pl.ds(s, ncp, stride=CMP_STRIDE), :]))
    _compress_compute(rows_of_parts, pe_ref, wk_ref, wv_ref, w2k_ref, w2v_ref, sh_ref, kc_ref.at[0], vc_ref.at[0], nc)


def _compress_sample(pool_t, page_table, cw):
    nseq, npages = page_table.shape
    page_size = pool_t.shape[3]
    past = npages * page_size
    nc = past // CMP_STRIDE
    assert page_size == LANES and KV_WIDTH == LANES
    out = jax.ShapeDtypeStruct((nseq, nc, LANES), F32)
    cspecs = [pl.BlockSpec(w.shape, lambda b, pt, nd=w.ndim: (0,) * nd, pipeline_mode=pl.Buffered(1)) for w in cw]
    return pl.pallas_call(
        functools.partial(_compress_sample_kernel, nc=nc, npages=npages, page_size=page_size),
        grid_spec=pltpu.PrefetchScalarGridSpec(
            num_scalar_prefetch=1,
            grid=(nseq,),
            in_specs=[pl.BlockSpec(memory_space=pl.ANY)] + cspecs,
            out_specs=[pl.BlockSpec((1, nc, LANES), lambda b, pt: (b, 0, 0))] * 2,
            scratch_shapes=[pltpu.VMEM((2, npages, 2, KV_WIDTH, page_size), F32), pltpu.SemaphoreType.DMA((2,)),
                            pltpu.VMEM((CMP_PARTS, past // CMP_PARTS, KV_WIDTH), F32),
                            pltpu.VMEM((CMP_PARTS, past // CMP_PARTS, KV_WIDTH), F32),
                            pltpu.VMEM((nc + 8, NSA_KV_HEADS * CMP_HIDDEN), F32)],
        ),
        out_shape=[out, out],
        compiler_params=_cparams("arbitrary"),
    )(page_table.reshape(-1), pool_t, *cw)


def _bucket_thresholds():
    n = np.arange(0, 2 * REL_MAX_DIST)
    exact = REL_BUCKETS // 2
    out = []
    for ft in (np.float32, np.float64):
        nf = np.maximum(n, exact).astype(ft)
        large = exact + (np.log(nf / ft(exact)) / ft(math.log(REL_MAX_DIST / exact)) * ft(REL_BUCKETS - exact)).astype(np.int64)
        out.append(np.where(n < exact, n, np.minimum(large, REL_BUCKETS - 1)))
    assert (out[0] == out[1]).all() and out[1][REL_MAX_DIST] == REL_BUCKETS - 1
    return [int(np.argmax(out[1] >= b)) for b in range(1, REL_BUCKETS)]


_THR = _bucket_thresholds()


def _bias_tables(dist, rb_ref, heads):
    steps = [jnp.where(dist >= _THR[b - 1], 1.0, 0.0) for b in range(1, REL_BUCKETS)]
    far = dist >= REL_MAX_DIST
    out = []
    for h in heads:
        acc = jnp.full(dist.shape, rb_ref[0, h] - rb_ref[REL_BUCKETS - 1, h], F32)
        for b in range(1, REL_BUCKETS):
            acc = acc + steps[b - 1] * (rb_ref[b, h] - rb_ref[b - 1, h])
        out.append(jnp.where(dist < 0, NEG, jnp.where(far, 0.0, acc)))
    return out


BIG = 1e30
KT = 256
SLC_UNROLL = 8
SLC_CUTS = 4
N_BAND = 20
BAND_LO = 12
F_STEP = 2 * N_BAND
N_WK = 6
N_SK = 4
WIN_TILES = WINDOW // Q_BLOCK + 1


def _nsa_tables(rb_ref, st_ref, wt_ref, qft_ref, ov_ref, ncp):
    tq = Q_BLOCK
    heads = range(NSA_HEADS)
    t2 = lax.broadcasted_iota(jnp.int32, (tq, KT), 0)
    c2 = lax.broadcasted_iota(jnp.int32, (tq, KT), 1)
    for dd in range(N_SK - 1):
        tabs = _bias_tables(tq * dd + t2 - c2, rb_ref, heads)
        for h in heads:
            st_ref[h * N_SK + dd] = tabs[h]
    t1 = lax.broadcasted_iota(jnp.int32, (tq, LANES), 0)
    c1 = lax.broadcasted_iota(jnp.int32, (tq, LANES), 1)
    zeros = jnp.zeros((tq, LANES), F32)
    for d in range(2):
        tabs = _bias_tables(tq * d + t1 - c1, rb_ref, heads)
        for h in heads:
            wt_ref[h * N_WK + d] = tabs[h]
    for h in heads:
        st_ref[h * N_SK + N_SK - 1] = jnp.zeros((tq, KT), F32)
        wt_ref[h * N_WK + 2] = zeros
        wt_ref[h * N_WK + 3] = zeros
        wt_ref[h * N_WK + 4] = jnp.where(c1 > t1, 0.0, NEG)
        wt_ref[h * N_WK + 5] = zeros + NEG
    for g in range(NSA_KV_HEADS):
        f = c1 - (HEAD_DIM if g == 0 else 0)
        is_lo = f >= N_BAND
        u = jnp.where(is_lo, f - N_BAND, f) - BAND_LO
        live = (f >= 0) & (f < 2 * N_BAND)
        tabs = _bias_tables(t1 - CMP_STRIDE * u - (CMP_BLOCK - 1), rb_ref, range(g * NSA_GROUP, (g + 1) * NSA_GROUP))
        for r in range(NSA_GROUP):
            hi = tabs[r].astype(jnp.bfloat16).astype(F32)
            lo = jnp.where(tabs[r] <= NEG, 0.0, tabs[r] - hi)
            val = jnp.where(live, jnp.where(is_lo, lo, hi), 0.0)
            qft_ref[g * NSA_GROUP + r] = jnp.where(f == F_STEP, NEG, val)
    kk = lax.broadcasted_iota(jnp.int32, (ncp, LANES), 0)
    ss = lax.broadcasted_iota(jnp.int32, (ncp, LANES), 1)
    ratio = SLC_BLOCK // CMP_STRIDE
    nover = (CMP_BLOCK - 1) // CMP_STRIDE
    ov = (kk >= ratio * ss - nover) & (kk < ratio * ss + ratio) & (kk < ncp - 1)
    ov_ref[...] = jnp.where(ov, 1.0, 0.0).astype(ov_ref.dtype)


def _lane_half(rows):
    return lax.broadcasted_iota(jnp.int32, (rows, LANES), 1) // HEAD_DIM


def _place_heads(qg, g):
    keep = _lane_half(qg.shape[0]) == g
    out = []
    for r in range(NSA_GROUP):
        v = qg[:, (r // 2) * LANES:(r // 2 + 1) * LANES]
        v = jnp.where(g == r % 2, v, pltpu.roll(v, HEAD_DIM, 1))
        out.append(jnp.where(keep, v, 0.0))
    return out


def _unplace_heads(ohs, g):
    keep = _lane_half(ohs[0].shape[0]) == g
    cols = []
    for c in range(NSA_GROUP // 2):
        pair = []
        for r in (2 * c, 2 * c + 1):
            oh = jnp.where(keep, ohs[r], 0.0)
            pair.append(jnp.where(g == r % 2, oh, pltpu.roll(oh, HEAD_DIM, 1)))
        cols.append(pair[0] + pair[1])
    return jnp.concatenate(cols, axis=1)


def _gate_mix(gates, r, o_cmp, o_slc, o_win, l_slc=None, l_win=None):
    c_slc = gates[:, 3 * r + 1:3 * r + 2]
    c_win = gates[:, 3 * r + 2:3 * r + 3]
    if l_slc is not None:
        c_slc = c_slc / l_slc
    if l_win is not None:
        c_win = c_win / l_win
    return gates[:, 3 * r:3 * r + 1] * o_cmp + c_slc * o_slc + c_win * o_win


def _top_blocks(imp, qpos_lane):
    ns = imp.shape[1]
    vals = jnp.concatenate([imp[:, c * LANES:(c + 1) * LANES].T for c in range(ns // LANES)], axis=0)
    srow = lax.broadcasted_iota(jnp.int32, vals.shape, 0)
    qblk = qpos_lane[0:1, :] // SLC_BLOCK
    forced = (srow == 0) | (srow == qblk) | (srow == qblk - 1)
    vals = jnp.where(forced, -2.0, vals)
    vals = jnp.where(srow <= qblk, vals, -1.0)
    srow_f = srow.astype(F32)
    sel = jnp.where(forced, 1.0, 0.0)
    for _ in range(SLC_TOPN - 3):
        mx = jnp.max(vals, axis=0, keepdims=True)
        idx = jnp.min(jnp.where(vals == mx, srow_f, float(ns)), axis=0, keepdims=True)
        pick = srow_f == idx
        sel = jnp.where(pick, 1.0, sel)
        vals = jnp.where(pick, -2.0, vals)
    sel = jnp.where(srow <= qblk, sel, 0.0)
    return jnp.concatenate([sel[c * LANES:(c + 1) * LANES].T for c in range(ns // LANES)], axis=1)


def _nsa_prompt_kernel(rb_ref, q_ref, gate_ref, kc_ref, vc_ref, ksa_ref, vsa_ref, kw_ref, vwa_ref, o_ref,
                       st_ref, wt_ref, qft_ref, ov_ref, lbuf_ref, m_ref, acc_ref, qs_ref, *, ncp):
    tq = Q_BLOCK
    i = pl.program_id(1)
    gw = NSA_GROUP * HEAD_DIM

    @pl.when((pl.program_id(0) == 0) & (i == 0))
    def _():
        _nsa_tables(rb_ref, st_ref, wt_ref, qft_ref, ov_ref, ncp)

    qpos_lane = i * tq + lax.broadcasted_iota(jnp.int32, (tq, LANES), 1)
    trow = lax.broadcasted_iota(jnp.int32, (NSA_GROUP * tq, 1), 0) % tq
    row_live = (i * tq + trow) >= CMP_BLOCK - 1
    kidx = lax.broadcasted_iota(jnp.int32, (ncp, LANES), 0)
    klane = lax.broadcasted_iota(jnp.int32, (ncp, LANES), 1)
    ngrp = i // (2 * SLC_UNROLL) + 1
    nplain = jnp.maximum(ngrp - 2, 0)
    j0 = jnp.maximum(i - (WIN_TILES - 1), 0)
    ws = pl.multiple_of(j0 * tq, tq)
    kinds = []
    for jj in range(WIN_TILES):
        d = i - j0 - jj
        kinds.append(jnp.where(d < 0, N_WK - 1, d))
    u12 = kidx - (tq // CMP_STRIDE) * i + BAND_LO

    o_cmp, accw = [], []
    for g in range(NSA_KV_HEADS):
        qp = _place_heads(q_ref[:, g * gw:(g + 1) * gw], g)
        heads = [g * NSA_GROUP + r for r in range(NSA_GROUP)]
        lw = _mm_nt(jnp.concatenate(qp, axis=0).astype(MXU), kw_ref[pl.ds(ws, WIN_TILES * tq), :])
        lw = lw + jnp.concatenate(
            [jnp.concatenate([wt_ref[h * N_WK + kinds[jj]] for jj in range(WIN_TILES)], axis=1) for h in heads], axis=0)
        ew = jnp.exp(lw - jnp.max(lw, axis=1, keepdims=True)).astype(MXU)
        accw.append(jnp.dot(ew, vwa_ref[pl.ds(ws, WIN_TILES * tq), :], preferred_element_type=F32))
        f = klane - HEAD_DIM * (1 - g)
        fa = jnp.where(f >= N_BAND, f - N_BAND, f)
        feat = ((fa == u12) & (f >= 0) & (f < 2 * N_BAND)) | ((f == F_STEP) & (u12 >= N_BAND - 1))
        ka = jnp.where(klane // HEAD_DIM == g, kc_ref[0], jnp.where(feat, 1.0, 0.0)).astype(MXU)
        qa = jnp.concatenate([qp[r] + qft_ref[heads[r]] for r in range(NSA_GROUP)], axis=0).astype(MXU)
        lc = _mm_nt(qa, ka)
        ec = jnp.exp(lc - jnp.max(lc, axis=1, keepdims=True))
        sc = jnp.sum(ec, axis=1, keepdims=True)
        pc = (ec * jnp.where(row_live, 1.0 / sc, 0.0)).astype(MXU)
        o_cmp.append(jnp.dot(pc, vc_ref[0].astype(MXU), preferred_element_type=F32))
        imp = sum(jnp.dot(pc[r * tq:(r + 1) * tq], ov_ref[...], preferred_element_type=F32) for r in range(NSA_GROUP))
        selneg = (_top_blocks(imp, qpos_lane) - 1.0) * BIG
        qs_ref[g] = jnp.concatenate([jnp.concatenate([qp[r], selneg], axis=1) for r in range(NSA_GROUP)],
                                    axis=0).astype(qs_ref.dtype)

    gk = SLC_UNROLL * KT

    def group_body(g, carry):
        m_ref[...] = jnp.full(m_ref.shape, NEG, F32)

        def stage(grp, lt):
            width = lt.shape[1]
            lbuf_ref[grp, :, 0:width] = lt
            mg = lt[:, 0:LANES]
            for c in range(1, width // LANES):
                mg = jnp.maximum(mg, lt[:, c * LANES:(c + 1) * LANES])
            m_ref[...] = jnp.maximum(m_ref[...], mg)

        def keys(ref, grp, width):
            return ref[pl.ds(pl.multiple_of(grp * gk, gk), width), :]

        def plain_group(grp, c):
            stage(grp, _mm_nt(qs_ref[g], keys(ksa_ref, grp, gk)))
            return c

        def near_logits(grp, width):
            tabs = []
            for r in range(NSA_GROUP):
                row = []
                for u in range(width // KT):
                    dd = i - 2 * (grp * SLC_UNROLL + u)
                    kind = jnp.where((dd < 0) | (dd >= N_SK), N_SK - 1, dd)
                    row.append(st_ref[(g * NSA_GROUP + r) * N_SK + kind])
                tabs.append(jnp.concatenate(row, axis=1))
            stage(grp, _mm_nt(qs_ref[g], keys(ksa_ref, grp, width)) + jnp.concatenate(tabs, axis=0))

        def near_group(grp, c):
            near_logits(grp, gk)
            return c

        def pv(grp, width):
            e = jnp.exp(lbuf_ref[grp, :, 0:width] - jnp.concatenate([m_ref[...]] * (width // LANES), axis=1))
            acc_ref[g] += jnp.dot(e.astype(MXU), keys(vsa_ref, grp, width), preferred_element_type=F32)

        def pv_group(grp, c):
            pv(grp, gk)
            return c

        last = ngrp - 1
        quarter = (i // 2 - SLC_UNROLL * last) // (SLC_UNROLL // SLC_CUTS)
        lax.fori_loop(0, nplain, plain_group, 0)
        lax.fori_loop(nplain, last, near_group, 0)
        for c in range(SLC_CUTS):
            pl.when(quarter == c)(functools.partial(near_logits, last, (c + 1) * gk // SLC_CUTS))
        m_ref[...] = jnp.broadcast_to(jnp.max(m_ref[...], axis=1, keepdims=True), m_ref.shape)
        acc_ref[g] = jnp.zeros(acc_ref.shape[1:], F32)
        lax.fori_loop(0, last, pv_group, 0)
        for c in range(SLC_CUTS):
            pl.when(quarter == c)(functools.partial(pv, last, (c + 1) * gk // SLC_CUTS))
        return carry

    lax.fori_loop(0, NSA_KV_HEADS, group_body, 0)
    for g in range(NSA_KV_HEADS):
        gates = gate_ref[:, g * LANES:(g + 1) * LANES]
        ohs = []
        for r in range(NSA_GROUP):
            rows = slice(r * tq, (r + 1) * tq)
            ohs.append(_gate_mix(gates, r, o_cmp[g][rows], acc_ref[g, rows, 0:LANES], accw[g][rows, 0:LANES],
                                 acc_ref[g, rows, LANES:LANES + 1], accw[g][rows, LANES:LANES + 1]))
        o_ref[:, g * gw:(g + 1) * gw] = _unplace_heads(ohs, g)


def _nsa_prompt(rel_bias, q, gates, kc, vc, ksa, vsa, kw, vwa, nseq, seq_len):
    tq = Q_BLOCK
    nq = seq_len // tq
    ncp = kc.shape[1]
    gk = SLC_UNROLL * KT
    ng = NSA_KV_HEADS
    rows = NSA_GROUP * tq
    assert seq_len % gk == 0 and seq_len >= WIN_TILES * tq and ncp % LANES == 0
    per_seq = lambda width: pl.BlockSpec((seq_len, width), lambda b, i: (b, 0), pipeline_mode=pl.Buffered(1))
    blk = lambda width: pl.BlockSpec((tq, width), lambda b, i: (b * nq + i, 0))
    cmp_spec = pl.BlockSpec((1, ncp, LANES), lambda b, i: (b, 0, 0))
    return pl.pallas_call(
        functools.partial(_nsa_prompt_kernel, ncp=ncp),
        grid=(nseq, nq),
        in_specs=[pl.BlockSpec(memory_space=pltpu.SMEM), blk(NSA_INNER), blk(ng * LANES), cmp_spec, cmp_spec,
                  per_seq(2 * LANES), per_seq(2 * LANES), per_seq(LANES), per_seq(2 * LANES)],
        out_specs=blk(NSA_INNER),
        out_shape=jax.ShapeDtypeStruct((nseq * seq_len, NSA_INNER), F32),
        scratch_shapes=[pltpu.VMEM((NSA_HEADS * N_SK, tq, KT), F32), pltpu.VMEM((NSA_HEADS * N_WK, tq, LANES), F32),
                        pltpu.VMEM((NSA_HEADS, tq, LANES), F32), pltpu.VMEM((ncp, LANES), MXU),
                        pltpu.VMEM((seq_len // gk, rows, gk), F32), pltpu.VMEM((rows, LANES), F32),
                        pltpu.VMEM((ng, rows, 2 * LANES), F32), pltpu.VMEM((ng, rows, 2 * LANES), MXU)],
        compiler_params=_cparams("arbitrary", "arbitrary"),
    )(rel_bias, q, gates, kc, vc, ksa, vsa, kw, vwa)


TAIL = 2 * LANES


def _nsa_sample_tables(rb_ref, cbt_ref, tail_ref, wtab_ref, ov_ref, eall_ref, *, ls, past, ncp, wk, wkp, nk):
    heads = range(NSA_HEADS)

    def per_group(tabs):
        return [jnp.concatenate(tabs[g * NSA_GROUP:(g + 1) * NSA_GROUP], axis=0) for g in range(NSA_KV_HEADS)]

    t = lax.broadcasted_iota(jnp.int32, (ls, ncp), 0)
    k = lax.broadcasted_iota(jnp.int32, (ls, ncp), 1)
    dist = past + t - CMP_STRIDE * k - (CMP_BLOCK - 1)
    dist = jnp.where(k < ncp - 1, dist, -1)
    for g, tab in enumerate(per_group(_bias_tables(dist, rb_ref, heads))):
        cbt_ref[g] = tab
    t = lax.broadcasted_iota(jnp.int32, (ls, TAIL), 0)
    c = lax.broadcasted_iota(jnp.int32, (ls, TAIL), 1) + (nk - TAIL)
    dist = jnp.where(c < past + ls, past + t - c, -1)
    for g, tab in enumerate(per_group(_bias_tables(dist, rb_ref, heads))):
        tail_ref[g] = tab
    t = lax.broadcasted_iota(jnp.int32, (ls, wkp), 0)
    c = lax.broadcasted_iota(jnp.int32, (ls, wkp), 1)
    dist = past + t - (past - wk + c)
    dist = jnp.where((c < wk + ls) & (dist < WINDOW), dist, -1)
    for g, tab in enumerate(per_group(_bias_tables(dist, rb_ref, heads))):
        wtab_ref[g] = tab
    kk = lax.broadcasted_iota(jnp.int32, (ncp, 2 * LANES), 0)
    ss = lax.broadcasted_iota(jnp.int32, (ncp, 2 * LANES), 1)
    ratio = SLC_BLOCK // CMP_STRIDE
    nover = (CMP_BLOCK - 1) // CMP_STRIDE
    ov = (kk >= ratio * ss - nover) & (kk < ratio * ss + ratio) & (kk < ncp - 1)
    ov_ref[...] = jnp.where(ov, 1.0, 0.0).astype(ov_ref.dtype)
    s2 = lax.broadcasted_iota(jnp.int32, (2 * LANES, nk), 0)
    c2 = lax.broadcasted_iota(jnp.int32, (2 * LANES, nk), 1)
    eall_ref[...] = jnp.where(c2 // SLC_BLOCK == s2, 1.0, 0.0).astype(eall_ref.dtype)


def _softmax_rows(l):
    m = jnp.max(l, axis=1, keepdims=True)
    e = jnp.exp(l - m)
    return e / jnp.sum(e, axis=1, keepdims=True)


SAMPLE_SPS = 2


def _t128(x):
    rows = x.shape[0]
    if rows < LANES:
        x = jnp.concatenate([x, jnp.zeros((LANES - rows, LANES), x.dtype)], axis=0)
    return x.T


def _nsa_sample_kernel(pt_ref, rb_ref, q_ref, gate_ref, kvs_ref, kvw_ref, kc_ref, vc_ref, win_ref, pool_ref,
                       o_ref, kt_ref, vt_ref, sem_ref, cbt_ref, tail_ref, wtab_ref, ov_ref, eall_ref,
                       *, ls, past, npages, page_size, ncp, wk, wkp, nk):
    b = pl.program_id(0)

    @pl.when(b == 0)
    def _():
        _nsa_sample_tables(rb_ref, cbt_ref, tail_ref, wtab_ref, ov_ref, eall_ref,
                           ls=ls, past=past, ncp=ncp, wk=wk, wkp=wkp, nk=nk)

    sps = kt_ref.shape[1]

    def page_copies(step, slot, p):
        cols = pl.ds(pl.multiple_of(p * page_size, page_size), page_size)
        cps = []
        for j in range(sps):
            pg = pt_ref[(step * sps + j) * npages + p]
            cps.append(pltpu.make_async_copy(pool_ref.at[pg, 0], kt_ref.at[slot, j, :, cols], sem_ref.at[0, slot]))
            cps.append(pltpu.make_async_copy(pool_ref.at[pg, 1], vt_ref.at[slot, j, :, cols], sem_ref.at[1, slot]))
        return cps

    slot = _gather_step(page_copies, npages)
    gw = NSA_GROUP * HEAD_DIM
    ng = NSA_KV_HEADS
    nrow = ng * NSA_GROUP * ls
    stack = lambda xs: xs[0] if len(xs) == 1 else jnp.concatenate(xs, axis=0)
    seq_rows = lambda x, j: x[j * nrow:(j + 1) * nrow]
    per_group = lambda ref: stack([ref[g] for g in range(ng)] * sps)
    qall, kwt, vwt = [], [], []
    for j in range(sps):
        tok = slice(j * ls, (j + 1) * ls)
        kvs = kvs_ref[tok, :]
        kt_ref[slot, j, :, past:nk] = _t128(kvs[:, 0:KV_WIDTH])
        vt_ref[slot, j, :, past:nk] = _t128(kvs[:, KV_WIDTH:])
        kvw = kvw_ref[tok, :]
        kwt.append(jnp.concatenate([win_ref[j, 0], _t128(kvw[:, 0:KV_WIDTH])], axis=1).astype(MXU))
        vwt.append(jnp.concatenate([win_ref[j, 1], _t128(kvw[:, KV_WIDTH:])], axis=1).astype(MXU))
        qall.append(stack([t for g in range(ng) for t in _place_heads(q_ref[tok, g * gw:(g + 1) * gw], g)]).astype(MXU))
    pc = _softmax_rows(stack([_mm_nt(qall[j], kc_ref[j]) for j in range(sps)]) + per_group(cbt_ref)).astype(MXU)
    o_cmp = [jnp.dot(seq_rows(pc, j), vc_ref[j].astype(MXU), preferred_element_type=F32) for j in range(sps)]
    impm = jnp.dot(pc, ov_ref[...], preferred_element_type=F32)
    nsel = sps * ng * ls
    imp = stack([sum(impm[((j * ng + g) * NSA_GROUP + r) * ls:((j * ng + g) * NSA_GROUP + r + 1) * ls]
                     for r in range(NSA_GROUP)) for j in range(sps) for g in range(ng)])
    imp = jnp.concatenate([imp, jnp.zeros((LANES - nsel, imp.shape[1]), F32)], axis=0)
    qpos_lane = past + lax.broadcasted_iota(jnp.int32, (1, LANES), 1) % ls
    selneg = ((_top_blocks(imp, qpos_lane)[0:nsel] - 1.0) * BIG).astype(MXU)
    selneg = stack([selneg[jg * ls:(jg + 1) * ls] for jg in range(sps * ng) for _ in range(NSA_GROUP)])
    mask = jnp.dot(selneg, eall_ref[...], preferred_element_type=F32)
    lsl = stack([jnp.dot(qall[j], kt_ref[slot, j].astype(MXU), preferred_element_type=F32) for j in range(sps)]) + mask
    lsl = jnp.concatenate([lsl[:, 0:nk - TAIL], lsl[:, nk - TAIL:] + per_group(tail_ref)], axis=1)
    ps = _softmax_rows(lsl)
    o_slc = [_mm_nt(seq_rows(ps, j), vt_ref[slot, j]) for j in range(sps)]
    pw = _softmax_rows(stack([jnp.dot(qall[j], kwt[j], preferred_element_type=F32) for j in range(sps)])
                       + per_group(wtab_ref))
    o_win = [_mm_nt(seq_rows(pw, j), vwt[j]) for j in range(sps)]
    for j in range(sps):
        tok = slice(j * ls, (j + 1) * ls)
        for g in range(ng):
            gates = gate_ref[tok, g * LANES:(g + 1) * LANES]
            ohs = []
            for r in range(NSA_GROUP):
                rows = slice((g * NSA_GROUP + r) * ls, (g * NSA_GROUP + r + 1) * ls)
                ohs.append(_gate_mix(gates, r, o_cmp[j][rows], o_slc[j][rows], o_win[j][rows]))
            o_ref[tok, g * gw:(g + 1) * gw] = _unplace_heads(ohs, g)


def _nsa_sample(rel_bias, q, gates, kvs, kvw, kc, vc, win_t, pool_t, page_table, ls):
    nseq, npages = page_table.shape
    page_size = pool_t.shape[3]
    past = npages * page_size
    ncp = kc.shape[1]
    wk = win_t.shape[3]
    wkp = wk + LANES
    nk = past + LANES
    assert past % SLC_BLOCK == 0 and ls % 8 == 0 and ls <= SLC_BLOCK and past >= wk and past // SLC_BLOCK + 1 <= 2 * LANES
    assert ncp % LANES == 0 and past >= CMP_BLOCK and page_size % LANES == 0 and wk % LANES == 0 and KV_WIDTH == LANES
    sps = SAMPLE_SPS if nseq % SAMPLE_SPS == 0 else 1
    row = lambda width: pl.BlockSpec((sps * ls, width), lambda b, pt: (b, 0))
    seq3 = lambda n, width: pl.BlockSpec((sps, n, width), lambda b, pt: (b, 0, 0))
    kw = 2 * KV_WIDTH
    nrow = NSA_GROUP * ls
    return pl.pallas_call(
        functools.partial(_nsa_sample_kernel, ls=ls, past=past, npages=npages, page_size=page_size, ncp=ncp,
                          wk=wk, wkp=wkp, nk=nk),
        grid_spec=pltpu.PrefetchScalarGridSpec(
            num_scalar_prefetch=1,
            grid=(nseq // sps,),
            in_specs=[pl.BlockSpec(memory_space=pltpu.SMEM), row(NSA_INNER), row(NSA_KV_HEADS * LANES), row(kw), row(kw),
                      seq3(ncp, LANES), seq3(ncp, LANES),
                      pl.BlockSpec((sps, 2, KV_WIDTH, wk), lambda b, pt: (b, 0, 0, 0)), pl.BlockSpec(memory_space=pl.ANY)],
            out_specs=row(NSA_INNER),
            scratch_shapes=[pltpu.VMEM((2, sps, KV_WIDTH, nk), F32), pltpu.VMEM((2, sps, KV_WIDTH, nk), F32),
                            pltpu.SemaphoreType.DMA((2, 2)),
                            pltpu.VMEM((NSA_KV_HEADS, nrow, ncp), F32), pltpu.VMEM((NSA_KV_HEADS, nrow, TAIL), F32),
                            pltpu.VMEM((NSA_KV_HEADS, nrow, wkp), F32),
                            pltpu.VMEM((ncp, 2 * LANES), MXU), pltpu.VMEM((2 * LANES, nk), MXU)],
        ),
        out_shape=jax.ShapeDtypeStruct((nseq * ls, NSA_INNER), F32),
        compiler_params=_cparams("arbitrary"),
    )(page_table.reshape(-1), rel_bias, q, gates, kvs, kvw, kc, vc, win_t, pool_t)


def _mix_out_kernel(x_ref, yssd_ref, onsa_ref, gn_ref, wo_ref, gq_ref, wq_ref, mkv_ref, wmo_ref, *rest,
                    rows_per_seq, ffn, final_norm):
    if ffn:
        g2_ref, win_ref, wout_ref = rest[0:3]
        gf_ref = rest[3] if final_norm else None
        out_ref, x2_ref, h_ref = rest[-3:]
        o_ref = x2_ref
    else:
        o_ref, = rest
    y = jnp.concatenate([yssd_ref[...], _rms(onsa_ref[...], gn_ref[...])], axis=1)
    x = x_ref[...] + jnp.dot(y.astype(MXU), wo_ref[...], preferred_element_type=F32)
    qm = jnp.dot(_rms(x, gq_ref[...]).astype(MXU), wq_ref[...], preferred_element_type=F32)
    transposed = len(mkv_ref.shape) == 4
    spb = mkv_ref.shape[0]
    m = mkv_ref.shape[3] if transposed else mkv_ref.shape[1]
    if transposed:
        side = lambda kv_i: jnp.concatenate([mkv_ref[s, kv_i] for s in range(spb)], axis=1).astype(MXU)
        km, vm = side(0), side(1)
        logits = lambda qh: jnp.dot(qh.astype(MXU), km, preferred_element_type=F32)
        attend = lambda p: _mm_nt(p, vm)
    else:
        kv = mkv_ref[...].reshape(spb * m, 2 * MEM_INNER)
        km = kv[:, 0:MEM_INNER].astype(MXU)
        vm = kv[:, MEM_INNER:].astype(MXU)
        logits = lambda qh: _mm_nt(qh, km)
        attend = lambda p: jnp.dot(p.astype(MXU), vm, preferred_element_type=F32)
    head = lax.broadcasted_iota(jnp.int32, qm.shape, 1) // HEAD_DIM
    if spb > 1:
        rseq = lax.broadcasted_iota(jnp.int32, (qm.shape[0], spb * m), 0) // rows_per_seq
        kseq = lax.broadcasted_iota(jnp.int32, (qm.shape[0], spb * m), 1) // m
        own = rseq == kseq
    o = jnp.zeros(qm.shape, F32)
    for h in range(MEM_HEADS):
        lg = logits(jnp.where(head == h, qm, 0.0)) * (HEAD_DIM ** -0.5)
        if spb > 1:
            lg = jnp.where(own, lg, NEG)
        o = o + jnp.where(head == h, attend(_softmax_rows(lg)), 0.0)
    o_ref[...] = x + jnp.dot(o.astype(MXU), wmo_ref[...], preferred_element_type=F32)
    if ffn:
        _ffn_apply(x2_ref, g2_ref, win_ref, wout_ref, gf_ref, out_ref, h_ref)


def _mix_out(x, yssd, onsa, g_nsa, w_out, g_memq, w_mem_q, mem_kv, w_mem_out, nseq, seq_len, ffn=None,
             tm=512, short_rows=64):
    t, d = x.shape
    mkv_block = (lambda spb: (spb,) + mem_kv.shape[1:])
    if seq_len < short_rows and nseq % (short_rows // seq_len) == 0:
        spb = short_rows // seq_len
        tm, nt = spb * seq_len, 1
    else:
        spb, tm = 1, min(tm, seq_len)
        nt = seq_len // tm
        assert nt * tm == seq_len
    row = lambda width: pl.BlockSpec((tm, width), lambda b, i: (b * nt + i, 0))
    args = [x, yssd, onsa, g_nsa.reshape(1, -1), w_out.astype(MXU), g_memq.reshape(1, d), w_mem_q.astype(MXU),
            mem_kv, w_mem_out.astype(MXU)]
    specs = [row(d), row(SSD_INNER), row(NSA_INNER), _const_spec((1, NSA_INNER)), _const_spec(w_out.shape),
             _const_spec((1, d)), _const_spec(w_mem_q.shape),
             pl.BlockSpec(mkv_block(spb), lambda b, i: (b,) + (0,) * (mem_kv.ndim - 1)),
             _const_spec(w_mem_out.shape)]
    scratch = []
    final_norm = False
    if ffn:
        g2, w_ffn_in, w_ffn_out, g_final = ffn
        assert w_ffn_out.shape[0] % FFN_CHUNK == 0
        final_norm = g_final is not None
        args += [g2.reshape(1, d), w_ffn_in.astype(MXU), w_ffn_out.astype(MXU)]
        specs += [_const_spec((1, d)), _const_spec(w_ffn_in.shape), _const_spec(w_ffn_out.shape)]
        if final_norm:
            args.append(g_final.reshape(1, d))
            specs.append(_const_spec((1, d)))
        scratch = [pltpu.VMEM((tm, d), F32), pltpu.VMEM((tm, d), MXU)]
    return pl.pallas_call(
        functools.partial(_mix_out_kernel, rows_per_seq=seq_len, ffn=bool(ffn), final_norm=final_norm),
        grid=(nseq // spb, nt),
        in_specs=specs,
        out_specs=row(d),
        out_shape=jax.ShapeDtypeStruct((t, d), F32),
        scratch_shapes=scratch,
        compiler_params=_cparams("parallel", "parallel"),
    )(*args)


def _norm_matmul_kernel(x_ref, g_ref, w_ref, o_ref):
    o_ref[...] = jnp.dot(_rms(x_ref[...], g_ref[...]).astype(MXU), w_ref[...], preferred_element_type=F32)


def _norm_matmul(x, g, w, tm=256):
    t, d = x.shape
    n = w.shape[1]
    tm = min(tm, t)
    assert t % tm == 0
    return pl.pallas_call(
        _norm_matmul_kernel,
        grid=(t // tm,),
        in_specs=[pl.BlockSpec((tm, d), lambda i: (i, 0)), _const_spec((1, d)), _const_spec(w.shape)],
        out_specs=pl.BlockSpec((tm, n), lambda i: (i, 0)),
        out_shape=jax.ShapeDtypeStruct((t, n), F32),
        compiler_params=_cparams("parallel"),
    )(x, g.reshape(1, d), w.astype(MXU))


def _trunk_layer(x, lw, cw, rel_bias, mem_kv, conv0, h0, nseq, seq_len, ffn2, sample=None):
    x1 = _ffn(x, lw['norm_ffn1'], lw['w_ffn1_in'], lw['w_ffn1_out'])
    outs = _proj(x1, lw['norm_mix'], lw['w_mix'], seq_len=None if sample else seq_len)
    z, xbc, q, kvc, kvs, kvw, gates, dt = outs[:8]
    y_ssd, h_new, conv_new = _ssd(z, xbc, dt, conv0, h0, lw['conv_w'], lw['conv_b'], lw['dt_bias'], lw['a_log'],
                                  lw['d_skip'], lw['norm_ssd_out'], nseq, seq_len)
    rows5 = lambda a: a.reshape(nseq, seq_len, 2, NSA_KV_HEADS, HEAD_DIM)
    if sample is None:
        kc, vc = _compress_prompt(kvc, cw, nseq, seq_len)
        o_nsa = _nsa_prompt(rel_bias, q, gates, kc, vc, *outs[8:12], nseq, seq_len)
        from_t = lambda a: jnp.transpose(a.reshape(nseq, 2, NSA_KV_HEADS, HEAD_DIM, seq_len), (0, 4, 1, 2, 3))
        kv_c, kv_s = from_t(outs[12]), from_t(outs[13])
    else:
        kc, vc = _compress_sample(sample['pool_c'], sample['page_table'], cw)
        o_nsa = _nsa_sample(rel_bias, q, gates, kvs, kvw, kc, vc, sample['win_t'], sample['pool_s'],
                            sample['page_table'], seq_len)
        kv_c, kv_s = rows5(kvc), rows5(kvs)
    if sample is None:
        x2 = _mix_out(x1, y_ssd, o_nsa, lw['norm_nsa_out'], lw['w_out'], lw['norm_mem_q'], lw['w_mem_q'], mem_kv,
                      lw['w_mem_out'], nseq, seq_len, ffn=ffn2)
    else:
        x2 = _mix_out(x1, y_ssd, o_nsa, lw['norm_nsa_out'], lw['w_out'], lw['norm_mem_q'], lw['w_mem_q'], mem_kv,
                      lw['w_mem_out'], nseq, seq_len)
        x2 = _ffn(x2, *ffn2)
    keep = min(WINDOW, seq_len)
    kv_w = kvw.reshape(nseq, seq_len, -1)[:, seq_len - keep:].reshape(nseq, keep, 2, NSA_KV_HEADS, HEAD_DIM)
    return x2, kv_c, kv_s, kv_w, h_new, conv_new


def kernel(x_prompt, x_sample, cache_kv_cmp, cache_kv_slc, state_win_kv, state_ssd, state_conv, cache_mem_kv, page_table, mem_prompt, norm_ffn1, w_ffn1_in, w_ffn1_out, norm_mix, w_in, conv_w, conv_b, dt_bias, a_log, d_skip, norm_ssd_out, cmp_pos_k, cmp_w1_k, cmp_w2_k, cmp_pos_v, cmp_w1_v, cmp_w2_v, norm_nsa_out, w_out, norm_mem_q, norm_mem_kv, w_mem_q, w_mem_kv, w_mem_out, norm_ffn2, w_ffn2_in, w_ffn2_out, rel_bias, norm_final):
    bp, lp, d = x_prompt.shape
    bs, lsq, _ = x_sample.shape
    depth = w_in.shape[0]
    xp = x_prompt.reshape(bp * lp, d)
    xs = x_sample.reshape(bs * lsq, d)
    outs_p = [[] for _ in range(6)]
    outs_s = [[] for _ in range(5)]
    for layer in range(depth):
        last = layer == depth - 1
        lw = {
            'norm_ffn1': norm_ffn1[layer], 'w_ffn1_in': w_ffn1_in[layer], 'w_ffn1_out': w_ffn1_out[layer],
            'norm_mix': norm_mix[layer], 'w_mix': _regroup_w_in(w_in[layer]), 'conv_w': conv_w[layer],
            'conv_b': conv_b[layer], 'dt_bias': dt_bias[layer], 'a_log': a_log[layer], 'd_skip': d_skip[layer],
            'norm_ssd_out': norm_ssd_out[layer], 'norm_nsa_out': norm_nsa_out[layer], 'w_out': w_out[layer],
            'norm_mem_q': norm_mem_q[layer], 'w_mem_q': w_mem_q[layer], 'w_mem_out': w_mem_out[layer],
        }
        cw = _compress_weights(cmp_pos_k[layer], cmp_w1_k[layer], cmp_w2_k[layer],
                               cmp_pos_v[layer], cmp_w1_v[layer], cmp_w2_v[layer])
        g_final = norm_final if last else None
        mem_kv_p = _norm_matmul(mem_prompt.reshape(-1, d), norm_mem_kv[layer], w_mem_kv[layer])
        mem_kv_p = mem_kv_p.reshape(bp, -1, 2 * MEM_INNER)
        conv0 = jnp.zeros((bp, SSD_CONV - 1, SSD_CONV_DIM), F32)
        h0 = jnp.zeros((bp, SSD_HEADS, HEAD_DIM, SSD_STATE), F32)
        ffn2 = (norm_ffn2[layer], w_ffn2_in[layer], w_ffn2_out[layer], g_final)
        xp, c_p, s_p, w_p, h_p, conv_p = _trunk_layer(xp, lw, cw, rel_bias, mem_kv_p, conv0, h0, bp, lp, ffn2)
        sample = dict(pool_c=_paged_view(cache_kv_cmp[layer]), pool_s=_paged_view(cache_kv_slc[layer]),
                      win_t=_paged_view(state_win_kv[layer]), page_table=page_table)
        mem_kv_s = jnp.transpose(cache_mem_kv[layer], (0, 2, 3, 4, 1)).reshape(bs, 2, MEM_INNER, -1)
        xs, c_s, s_s, w_s, h_s, conv_s = _trunk_layer(xs, lw, cw, rel_bias, mem_kv_s, state_conv[layer],
                                                      state_ssd[layer], bs, lsq, ffn2, sample=sample)
        w_s = jnp.concatenate([state_win_kv[layer], w_s], axis=1)[:, w_s.shape[1]:]
        for lst, v in zip(outs_p, (c_p, s_p, w_p, h_p, conv_p, mem_kv_p.reshape(bp, -1, 2, MEM_HEADS, HEAD_DIM))):
            lst.append(v)
        for lst, v in zip(outs_s, (c_s, s_s, w_s, h_s, conv_s)):
            lst.append(v)
    return (xp.reshape(bp, lp, d), xs.reshape(bs, lsq, d),
            *(jnp.stack(o) for o in outs_p), *(jnp.stack(o) for o in outs_s))
```
